```python
import math
import jax, jax.numpy as jnp
from jax import lax
import numpy as np

D_MODEL = 1024
BATCH = 2
SEQ = 8192
DEPTH = 4
DEC_BATCH = 32
DEC_SEQ = 32
PAST_LEN = 4096

CHUNK = 64
N_MIXERS = 2
N_LRU_LAYERS = (DEPTH + 1) // 2
N_SSD_LAYERS = DEPTH // 2
ALPHA = (2.0 * DEPTH) ** 0.25
BETA = (8.0 * DEPTH) ** -0.25
LN_EPS = 1e-5
RMS_EPS = 1e-5
CONV_WIDTH = 4
D_RNN = D_MODEL
LRU_BLOCKS = 16
LRU_BW = D_RNN // LRU_BLOCKS
LRU_C = 8.0
D_INNER = 2 * D_MODEL
SSD_HEAD_DIM = 64
SSD_HEADS = D_INNER // SSD_HEAD_DIM
SSD_GROUPS = 8
SSD_HPG = SSD_HEADS // SSD_GROUPS
SSD_STATE = 128
SSD_CONV_DIM = D_INNER + 2 * SSD_GROUPS * SSD_STATE
SSD_IN_DIM = D_INNER + SSD_CONV_DIM + SSD_HEADS
N_EXPERTS = 64
TOP_K = 6
N_EXPERT_GROUPS = 8
TOPK_GROUPS = 4
D_EXPERT = D_MODEL // 2
D_SHARED = D_EXPERT
ROUTE_SCALE = 1.0
MOE_BLOCK = 128

kernel_name = "hybrid_rglru_ssd_moe_stream_step"


def layer_norm(x, g, b):
    xf = x.astype(jnp.float32)
    mu = jnp.mean(xf, -1, keepdims=True)
    var = jnp.mean(jnp.square(xf - mu), -1, keepdims=True)
    return ((xf - mu) * lax.rsqrt(var + LN_EPS) * g.astype(jnp.float32) + b.astype(jnp.float32)).astype(x.dtype)


def causal_conv(x, buf, w, b):
    L = x.shape[1]
    xx = jnp.concatenate([buf.astype(x.dtype), x], axis=1)
    y = b + sum(w[k] * xx[:, k:k + L] for k in range(CONV_WIDTH))
    return y, xx[:, L:]


def _linear_combine(c1, c2):
    a1, b1 = c1
    a2, b2 = c2
    return a1 * a2, a2 * b1 + b2


def rglru_mixer(x, conv_buf, h0, w_in, conv_w, conv_b, w_rg, b_rg, w_ig, b_ig, lam, w_out):
    bsz, L, _ = x.shape
    gate, xr = jnp.split(x @ w_in, 2, axis=-1)
    xr, new_conv = causal_conv(xr, conv_buf, conv_w, conv_b)
    xb = xr.reshape(bsz, L, LRU_BLOCKS, LRU_BW)
    r = jax.nn.sigmoid(jnp.einsum('blhi,hij->blhj', xb, w_rg).reshape(bsz, L, D_RNN) + b_rg)
    ig = jax.nn.sigmoid(jnp.einsum('blhi,hij->blhj', xb, w_ig).reshape(bsz, L, D_RNN) + b_ig)
    log_a = LRU_C * r.astype(jnp.float32) * jax.nn.log_sigmoid(lam.astype(jnp.float32))
    a = jnp.exp(log_a)
    u = jnp.sqrt(-jnp.expm1(2.0 * log_a)) * (ig * xr).astype(jnp.float32)
    u = u.at[:, 0].add(a[:, 0] * h0.astype(jnp.float32))
    _, h = lax.associative_scan(_linear_combine, (a, u), axis=1)
    y = (h.astype(x.dtype) * jax.nn.gelu(gate, approximate=True)) @ w_out
    return y, new_conv, h[:, -1]


def ssd_scan(X, dt, A, Bm, Cm, h0):
    bsz, L = X.shape[:2]
    pad = (-L) % CHUNK
    if pad:
        X = jnp.pad(X, ((0, 0), (0, pad), (0, 0), (0, 0)))
        dt = jnp.pad(dt, ((0, 0), (0, pad), (0, 0)))
        Bm = jnp.pad(Bm, ((0, 0), (0, pad), (0, 0), (0, 0)))
        Cm = jnp.pad(Cm, ((0, 0), (0, pad), (0, 0), (0, 0)))
    Lp = L + pad
    nc = Lp // CHUNK
    Xd = (X * dt[..., None]).reshape(bsz, nc, CHUNK, SSD_GROUPS, SSD_HPG, SSD_HEAD_DIM)
    Adt = (dt * A).reshape(bsz, nc, CHUNK, SSD_GROUPS, SSD_HPG).transpose(0, 3, 4, 1, 2)
    Bc = Bm.reshape(bsz, nc, CHUNK, SSD_GROUPS, SSD_STATE)
    Cc = Cm.reshape(bsz, nc, CHUNK, SSD_GROUPS, SSD_STATE)
    cs = jnp.cumsum(Adt, axis=-1)
    mask = jnp.tril(jnp.ones((CHUNK, CHUNK), bool))
    Lmat = jnp.exp(jnp.where(mask, cs[..., :, None] - cs[..., None, :], -jnp.inf))
    CB = jnp.einsum('bclgn,bcsgn->bgcls', Cc, Bc)
    y_diag = jnp.einsum('bgcls,bgecls,bcsgep->bclgep', CB, Lmat, Xd)
    decay_states = jnp.exp(cs[..., -1:] - cs)
    chunk_states = jnp.einsum('bclgn,bgecl,bclgep->bcgepn', Bc, decay_states, Xd)
    chunk_decay = jnp.exp(cs[..., -1])

    def step(h, inp):
        s, dec = inp
        return dec[..., None, None] * h + s, h

    h_init = h0.reshape(bsz, SSD_GROUPS, SSD_HPG, SSD_HEAD_DIM, SSD_STATE)
    final, prev = lax.scan(step, h_init, (jnp.moveaxis(chunk_states, 1, 0), jnp.moveaxis(chunk_decay, 3, 0)))
    prev = jnp.moveaxis(prev, 0, 1)
    y_off = jnp.einsum('bclgn,bcgepn,bgecl->bclgep', Cc, prev, jnp.exp(cs))
    y = (y_diag + y_off).reshape(bsz, Lp, SSD_HEADS, SSD_HEAD_DIM)[:, :L]
    return y, final.reshape(bsz, SSD_HEADS, SSD_HEAD_DIM, SSD_STATE)


def gated_rmsnorm(y, z, w):
    g = y * jax.nn.silu(z.astype(jnp.float32))
    gg = g.reshape(*g.shape[:-1], SSD_GROUPS, D_INNER // SSD_GROUPS)
    gg = gg * lax.rsqrt(jnp.mean(jnp.square(gg), -1, keepdims=True) + RMS_EPS)
    return gg.reshape(g.shape) * w.astype(jnp.float32)


def ssd_mixer(x, conv_buf, h0, w_in, conv_w, conv_b, dt_bias, a_log, d_skip, norm_w, w_out):
    bsz, L, _ = x.shape
    zxbcdt = x @ w_in
    z = zxbcdt[..., :D_INNER]
    xbc = zxbcdt[..., D_INNER:D_INNER + SSD_CONV_DIM]
    dt = zxbcdt[..., D_INNER + SSD_CONV_DIM:]
    xbc, new_conv = causal_conv(xbc, conv_buf, conv_w, conv_b)
    xbc = jax.nn.silu(xbc).astype(jnp.float32)
    gn = SSD_GROUPS * SSD_STATE
    xs = xbc[..., :D_INNER].reshape(bsz, L, SSD_HEADS, SSD_HEAD_DIM)
    bm = xbc[..., D_INNER:D_INNER + gn].reshape(bsz, L, SSD_GROUPS, SSD_STATE)
    cm = xbc[..., D_INNER + gn:].reshape(bsz, L, SSD_GROUPS, SSD_STATE)
    dt = jax.nn.softplus(dt.astype(jnp.float32) + dt_bias.astype(jnp.float32))
    A = -jnp.exp(a_log.astype(jnp.float32))
    y, new_state = ssd_scan(xs, dt, A, bm, cm, h0.astype(jnp.float32))
    y = y + d_skip.astype(jnp.float32)[:, None] * xs
    y = gated_rmsnorm(y.reshape(bsz, L, D_INNER), z, norm_w)
    return y.astype(x.dtype) @ w_out, new_conv, new_state


def swiglu(x, wg, wu, wd):
    return (jax.nn.silu(x @ wg) * (x @ wu)) @ wd


def routed_experts(x2d, eidx, gates, wg, wu, wd):
    T = x2d.shape[0]
    A = T * TOP_K
    n_blocks = -(-A // MOE_BLOCK) + N_EXPERTS
    flat_e = eidx.reshape(A)
    order = jnp.argsort(flat_e)
    se = flat_e[order]
    st = (order // TOP_K).astype(jnp.int32)
    sw = gates.reshape(A)[order]
    counts = jnp.bincount(flat_e, length=N_EXPERTS)
    starts = jnp.cumsum(counts) - counts
    padded = (counts + MOE_BLOCK - 1) // MOE_BLOCK * MOE_BLOCK
    pends = jnp.cumsum(padded)
    dest = pends[se] - padded[se] + jnp.arange(A) - starts[se]
    slot_tok = jnp.full((n_blocks * MOE_BLOCK,), T, jnp.int32).at[dest].set(st)
    slot_w = jnp.zeros((n_blocks * MOE_BLOCK,), x2d.dtype).at[dest].set(sw.astype(x2d.dtype))
    block_e = jnp.minimum(jnp.searchsorted(pends, jnp.arange(n_blocks) * MOE_BLOCK, side='right'), N_EXPERTS - 1)
    x_pad = jnp.concatenate([x2d, jnp.zeros((1, D_MODEL), x2d.dtype)], axis=0)
    xb = x_pad[slot_tok].reshape(n_blocks, MOE_BLOCK, D_MODEL)
    out = lax.map(lambda a: swiglu(a[0], wg[a[1]], wu[a[1]], wd[a[1]]), (xb, block_e))
    out = out.reshape(-1, D_MODEL) * slot_w[:, None]
    return jnp.zeros((T + 1, D_MODEL), x2d.dtype).at[slot_tok].add(out)[:T]


def moe(x2d, router_w, router_bias, wg, wu, wd, sg, su, sd):
    T = x2d.shape[0]
    scores = jax.nn.sigmoid(x2d.astype(jnp.float32) @ router_w.astype(jnp.float32))
    biased = scores + router_bias.astype(jnp.float32)
    per_group = biased.reshape(T, N_EXPERT_GROUPS, N_EXPERTS // N_EXPERT_GROUPS)
    group_score = lax.top_k(per_group, 2)[0].sum(-1)
    _, gidx = lax.top_k(group_score, TOPK_GROUPS)
    gmask = jax.nn.one_hot(gidx, N_EXPERT_GROUPS).sum(1) > 0
    emask = jnp.repeat(gmask, N_EXPERTS // N_EXPERT_GROUPS, axis=1)
    _, eidx = lax.top_k(jnp.where(emask, biased, -jnp.inf), TOP_K)
    g = jnp.take_along_axis(scores, eidx, axis=1)
    g = g / jnp.sum(g, -1, keepdims=True) * ROUTE_SCALE
    return routed_experts(x2d, eidx, g, wg, wu, wd) + swiglu(x2d, sg, su, sd)


def trunk(x, st_lru_conv, st_lru_h, st_ssd_conv, st_ssd, p):
    lru_c, lru_h, ssd_c, ssd_s = [], [], [], []
    for i in range(DEPTH):
        j = i // N_MIXERS
        if i % N_MIXERS == 0:
            mix, c, h = rglru_mixer(x, st_lru_conv[j], st_lru_h[j], p['lru_w_in'][j], p['lru_conv_w'][j],
                                    p['lru_conv_b'][j], p['lru_w_rgate'][j], p['lru_b_rgate'][j],
                                    p['lru_w_igate'][j], p['lru_b_igate'][j], p['lru_lambda'][j], p['lru_w_out'][j])
            lru_c.append(c)
            lru_h.append(h)
        else:
            mix, c, s = ssd_mixer(x, st_ssd_conv[j], st_ssd[j], p['ssd_w_in'][j], p['ssd_conv_w'][j],
                                  p['ssd_conv_b'][j], p['ssd_dt_bias'][j], p['ssd_a_log'][j], p['ssd_d'][j],
                                  p['ssd_norm_w'][j], p['ssd_w_out'][j])
            ssd_c.append(c)
            ssd_s.append(s)
        x = layer_norm(ALPHA * x + mix, p['ln_mix_g'][i], p['ln_mix_b'][i])
        bsz, L, _ = x.shape
        f = moe(x.reshape(bsz * L, D_MODEL), p['router_w'][i], p['router_bias'][i], p['moe_w_gate'][i],
                p['moe_w_up'][i], p['moe_w_down'][i], p['shared_w_gate'][i], p['shared_w_up'][i],
                p['shared_w_down'][i]).reshape(bsz, L, D_MODEL)
        x = layer_norm(ALPHA * x + f, p['ln_ffn_g'][i], p['ln_ffn_b'][i])
    return x, jnp.stack(lru_c), jnp.stack(lru_h), jnp.stack(ssd_c), jnp.stack(ssd_s)


def setup_inputs(seed: int = 0) -> dict:
    key = jax.random.key(seed)
    ks = iter(jax.random.split(key, 48))
    f32 = jnp.float32

    def nrm(shape, scale):
        return jax.random.normal(next(ks), shape, f32) * scale

    NA, NB = N_LRU_LAYERS, N_SSD_LAYERS
    u = jax.random.uniform(next(ks), (NA, D_RNN), f32, minval=0.9, maxval=0.999)
    s = u ** (1.0 / LRU_C)
    lru_lambda = jnp.log(s) - jnp.log1p(-s)
    dt0 = jnp.exp(jax.random.uniform(next(ks), (NB, SSD_HEADS), f32, minval=math.log(1e-3), maxval=math.log(1e-1)))
    ssd_dt_bias = dt0 + jnp.log(-jnp.expm1(-dt0))
    ssd_a_log = jnp.log(jax.random.uniform(next(ks), (NB, SSD_HEADS), f32, minval=1.0, maxval=16.0))
    return {
        "x_prompt": nrm((BATCH, SEQ, D_MODEL), 1.0),
        "x_sample": nrm((DEC_BATCH, DEC_SEQ, D_MODEL), 1.0),
        "state_lru_conv": nrm((NA, DEC_BATCH, CONV_WIDTH - 1, D_RNN), 1.0),
        "state_lru_h": nrm((NA, DEC_BATCH, D_RNN), 0.5),
        "state_ssd_conv": nrm((NB, DEC_BATCH, CONV_WIDTH - 1, SSD_CONV_DIM), 1.0),
        "state_ssd": nrm((NB, DEC_BATCH, SSD_HEADS, SSD_HEAD_DIM, SSD_STATE), 0.1),
        "lru_w_in": nrm((NA, D_MODEL, 2 * D_RNN), D_MODEL ** -0.5),
        "lru_conv_w": nrm((NA, CONV_WIDTH, D_RNN), CONV_WIDTH ** -0.5),
        "lru_conv_b": nrm((NA, D_RNN), 0.02),
        "lru_w_rgate": nrm((NA, LRU_BLOCKS, LRU_BW, LRU_BW), LRU_BW ** -0.5),
        "lru_b_rgate": nrm((NA, D_RNN), 0.02),
        "lru_w_igate": nrm((NA, LRU_BLOCKS, LRU_BW, LRU_BW), LRU_BW ** -0.5),
        "lru_b_igate": nrm((NA, D_RNN), 0.02),
        "lru_lambda": lru_lambda,
        "lru_w_out": nrm((NA, D_RNN, D_MODEL), D_RNN ** -0.5 * BETA),
        "ssd_w_in": nrm((NB, D_MODEL, SSD_IN_DIM), D_MODEL ** -0.5),
        "ssd_conv_w": nrm((NB, CONV_WIDTH, SSD_CONV_DIM), CONV_WIDTH ** -0.5),
        "ssd_conv_b": nrm((NB, SSD_CONV_DIM), 0.02),
        "ssd_dt_bias": ssd_dt_bias,
        "ssd_a_log": ssd_a_log,
        "ssd_d": 1.0 + nrm((NB, SSD_HEADS), 0.02),
        "ssd_norm_w": 1.0 + nrm((NB, D_INNER), 0.02),
        "ssd_w_out": nrm((NB, D_INNER, D_MODEL), D_INNER ** -0.5 * BETA),
        "ln_mix_g": 1.0 + nrm((DEPTH, D_MODEL), 0.02),
        "ln_mix_b": nrm((DEPTH, D_MODEL), 0.02),
        "ln_ffn_g": 1.0 + nrm((DEPTH, D_MODEL), 0.02),
        "ln_ffn_b": nrm((DEPTH, D_MODEL), 0.02),
        "router_w": nrm((DEPTH, D_MODEL, N_EXPERTS), D_MODEL ** -0.5),
        "router_bias": nrm((DEPTH, N_EXPERTS), 0.01),
        "moe_w_gate": nrm((DEPTH, N_EXPERTS, D_MODEL, D_EXPERT), D_MODEL ** -0.5),
        "moe_w_up": nrm((DEPTH, N_EXPERTS, D_MODEL, D_EXPERT), D_MODEL ** -0.5),
        "moe_w_down": nrm((DEPTH, N_EXPERTS, D_EXPERT, D_MODEL), D_EXPERT ** -0.5 * BETA),
        "shared_w_gate": nrm((DEPTH, D_MODEL, D_SHARED), D_MODEL ** -0.5),
        "shared_w_up": nrm((DEPTH, D_MODEL, D_SHARED), D_MODEL ** -0.5),
        "shared_w_down": nrm((DEPTH, D_SHARED, D_MODEL), D_SHARED ** -0.5 * BETA),
    }


def reference(x_prompt, x_sample, state_lru_conv, state_lru_h, state_ssd_conv, state_ssd,
              lru_w_in, lru_conv_w, lru_conv_b, lru_w_rgate, lru_b_rgate, lru_w_igate, lru_b_igate,
              lru_lambda, lru_w_out, ssd_w_in, ssd_conv_w, ssd_conv_b, ssd_dt_bias, ssd_a_log, ssd_d,
              ssd_norm_w, ssd_w_out, ln_mix_g, ln_mix_b, ln_ffn_g, ln_ffn_b, router_w, router_bias,
              moe_w_gate, moe_w_up, moe_w_down, shared_w_gate, shared_w_up, shared_w_down):
    p = dict(lru_w_in=lru_w_in, lru_conv_w=lru_conv_w, lru_conv_b=lru_conv_b, lru_w_rgate=lru_w_rgate,
             lru_b_rgate=lru_b_rgate, lru_w_igate=lru_w_igate, lru_b_igate=lru_b_igate, lru_lambda=lru_lambda,
             lru_w_out=lru_w_out, ssd_w_in=ssd_w_in, ssd_conv_w=ssd_conv_w, ssd_conv_b=ssd_conv_b,
             ssd_dt_bias=ssd_dt_bias, ssd_a_log=ssd_a_log, ssd_d=ssd_d, ssd_norm_w=ssd_norm_w,
             ssd_w_out=ssd_w_out, ln_mix_g=ln_mix_g, ln_mix_b=ln_mix_b, ln_ffn_g=ln_ffn_g, ln_ffn_b=ln_ffn_b,
             router_w=router_w, router_bias=router_bias, moe_w_gate=moe_w_gate, moe_w_up=moe_w_up,
             moe_w_down=moe_w_down, shared_w_gate=shared_w_gate, shared_w_up=shared_w_up,
             shared_w_down=shared_w_down)
    bp = x_prompt.shape[0]
    dt_ = x_prompt.dtype
    z_lru_conv = jnp.zeros((N_LRU_LAYERS, bp, CONV_WIDTH - 1, D_RNN), dt_)
    z_lru_h = jnp.zeros((N_LRU_LAYERS, bp, D_RNN), jnp.float32)
    z_ssd_conv = jnp.zeros((N_SSD_LAYERS, bp, CONV_WIDTH - 1, SSD_CONV_DIM), dt_)
    z_ssd = jnp.zeros((N_SSD_LAYERS, bp, SSD_HEADS, SSD_HEAD_DIM, SSD_STATE), jnp.float32)
    y_prompt, p_lru_conv, p_lru_h, p_ssd_conv, p_ssd = trunk(x_prompt, z_lru_conv, z_lru_h, z_ssd_conv, z_ssd, p)
    y_sample, s_lru_conv, s_lru_h, s_ssd_conv, s_ssd = trunk(x_sample, state_lru_conv, state_lru_h,
                                                             state_ssd_conv, state_ssd, p)
    return (y_prompt, y_sample, p_lru_conv, p_lru_h, p_ssd_conv, p_ssd, s_lru_conv, s_lru_h, s_ssd_conv, s_ssd)
```

```python
import functools
import math

import jax
import jax.numpy as jnp
from jax import lax
from jax.experimental import pallas as pl
from jax.experimental.pallas import tpu as pltpu

F32 = jnp.float32
BF16 = jnp.bfloat16

LN_EPS = 1e-5
RMS_EPS = 1e-5
CONV_WIDTH = 4
LRU_C = 8.0
SSD_CHUNK = 64
SSD_GROUPS = 8
TOP_K = 6
N_EXPERT_GROUPS = 8
TOPK_GROUPS = 4
ROUTE_SCALE = 1.0

LANE = 128
SUBLANE = 8
MXU_DIM = 256
VMEM_LIMIT = 56 * 1024 * 1024

TM_PROJ = 512
TM_COMBINE = 128
BM_EXPERT = 256
TL_LRU = 256

HIGHEST = lax.Precision.HIGHEST
NT_DIMS = (((1,), (1,)), ((), ()))
TN_DIMS = (((0,), (0,)), ((), ()))


def _sigmoid(x):
    return 1.0 / (1.0 + jnp.exp(-x))


def _silu(x):
    return x * _sigmoid(x)


def _softplus(x):
    return jnp.maximum(x, 0.0) + jnp.log1p(jnp.exp(-jnp.abs(x)))


def _gelu_tanh(x):
    c = math.sqrt(2.0 / math.pi)
    return 0.5 * x * (1.0 + jnp.tanh(c * (x + 0.044715 * (x * x * x))))


def _layer_norm(v, g, b):
    mu = jnp.mean(v, axis=-1, keepdims=True)
    d = v - mu
    var = jnp.mean(d * d, axis=-1, keepdims=True)
    return d * lax.rsqrt(var + LN_EPS) * g + b


def _params(*sem):
    return pltpu.CompilerParams(dimension_semantics=sem, vmem_limit_bytes=VMEM_LIMIT)


def _mm_kernel(x_ref, w_ref, o_ref):
    o_ref[...] = jnp.dot(x_ref[...].astype(BF16), w_ref[...].astype(BF16),
                         preferred_element_type=F32)


def _in_proj(x, w, layer, n_cols, tn):
    T, K = x.shape
    tm = TM_PROJ
    return pl.pallas_call(
        _mm_kernel,
        grid=(T // tm, n_cols // tn),
        in_specs=[pl.BlockSpec((tm, K), lambda i, j: (i, 0)),
                  pl.BlockSpec((None, K, tn), lambda i, j: (layer, 0, j))],
        out_specs=pl.BlockSpec((tm, tn), lambda i, j: (i, j)),
        out_shape=jax.ShapeDtypeStruct((T, n_cols), F32),
        compiler_params=_params("parallel", "parallel"),
        name="in_proj",
    )(x, w)


def _small_proj(x, w):
    T, K = x.shape
    N = w.shape[1]
    tm = TM_PROJ
    return pl.pallas_call(
        _mm_kernel,
        grid=(T // tm,),
        in_specs=[pl.BlockSpec((tm, K), lambda i: (i, 0)),
                  pl.BlockSpec((K, N), lambda i: (0, 0))],
        out_specs=pl.BlockSpec((tm, N), lambda i: (i, 0)),
        out_shape=jax.ShapeDtypeStruct((T, N), F32),
        compiler_params=_params("parallel"),
        name="dt_proj",
    )(x, w)


CONV_PAD = SUBLANE
CONV_HIST = CONV_PAD - (CONV_WIDTH - 1)


def _conv_tile(xx_ref, n, cw, cb):
    acc = cb + cw[CONV_WIDTH - 1:CONV_WIDTH] * xx_ref[CONV_PAD:CONV_PAD + n, :]
    for k in range(CONV_WIDTH - 1):
        off = CONV_HIST + k
        acc = acc + cw[k:k + 1] * xx_ref[off:off + n, :]
    return acc


def _lru_seq_kernel(gate_ref, xr_ref, cbuf_ref, h0_ref, cw_ref, cb_ref, wrg_ref, brg_ref,
                    wig_ref, big_ref, lam_ref, out_ref, nconv_ref, hlast_ref,
                    xx_ref, h_ref, a_ref, u_ref):
    c = pl.program_id(1)
    n = xr_ref.shape[0]
    d = xr_ref.shape[1]

    @pl.when(c == 0)
    def _():
        xx_ref[CONV_HIST:CONV_PAD, :] = cbuf_ref[0]
        h_ref[...] = h0_ref[0]

    xx_ref[CONV_PAD:CONV_PAD + n, :] = xr_ref[...]
    xr = _conv_tile(xx_ref, n, cw_ref[...], cb_ref[...])
    tail = xx_ref[n + CONV_HIST:n + CONV_PAD, :]
    xx_ref[CONV_HIST:CONV_PAD, :] = tail

    xb = xr.astype(BF16)
    nblk = d // MXU_DIM
    r_parts, i_parts = [], []
    for g in range(nblk):
        seg = xb[:, g * MXU_DIM:(g + 1) * MXU_DIM]
        r_parts.append(jnp.dot(seg, wrg_ref[g], preferred_element_type=F32))
        i_parts.append(jnp.dot(seg, wig_ref[g], preferred_element_type=F32))
    r = _sigmoid(jnp.concatenate(r_parts, axis=-1) + brg_ref[...])
    ig = _sigmoid(jnp.concatenate(i_parts, axis=-1) + big_ref[...])
    log_sig_lam = -_softplus(-lam_ref[...])
    log_a = LRU_C * r * log_sig_lam
    a = jnp.exp(log_a)
    a_ref[...] = a
    u_ref[...] = jnp.sqrt(-jnp.tanh(log_a) * (a * a + 1.0)) * (ig * xr)

    def step(t, h):
        h = a_ref[pl.ds(t, 1), :] * h + u_ref[pl.ds(t, 1), :]
        u_ref[pl.ds(t, 1), :] = h
        return h

    h = lax.fori_loop(0, n, step, h_ref[...], unroll=8)
    h_ref[...] = h
    out_ref[...] = (u_ref[...] * _gelu_tanh(gate_ref[...])).astype(BF16)

    @pl.when(c == pl.num_programs(1) - 1)
    def _():
        nconv_ref[0] = tail
        hlast_ref[0] = h


def _lru_seq(gx, row0, bsz, seqlen, tl, conv_buf, h0, cw, cb, wrg, brg, wig, big, lam):
    d = cw.shape[1]
    nt = seqlen // tl
    blk0 = row0 // tl
    row_map = lambda b, c: (blk0 + b * nt + c, 0)
    vec = lambda a: a.reshape(1, d)
    full2 = lambda b, c: (0, 0)
    full3 = lambda b, c: (0, 0, 0)
    return pl.pallas_call(
        _lru_seq_kernel,
        grid=(bsz, nt),
        in_specs=[
            pl.BlockSpec((tl, d), row_map),
            pl.BlockSpec((tl, d), lambda b, c: (blk0 + b * nt + c, 1)),
            pl.BlockSpec((1, CONV_WIDTH - 1, d), lambda b, c: (b, 0, 0)),
            pl.BlockSpec((1, 1, d), lambda b, c: (b, 0, 0)),
            pl.BlockSpec((CONV_WIDTH, d), full2),
            pl.BlockSpec((1, d), full2),
            pl.BlockSpec(wrg.shape, full3),
            pl.BlockSpec((1, d), full2),
            pl.BlockSpec(wig.shape, full3),
            pl.BlockSpec((1, d), full2),
            pl.BlockSpec((1, d), full2),
        ],
        out_specs=[
            pl.BlockSpec((tl, d), lambda b, c: (b * nt + c, 0)),
            pl.BlockSpec((1, CONV_WIDTH - 1, d), lambda b, c: (b, 0, 0)),
            pl.BlockSpec((1, 1, d), lambda b, c: (b, 0, 0)),
        ],
        out_shape=[
            jax.ShapeDtypeStruct((bsz * seqlen, d), BF16),
            jax.ShapeDtypeStruct((bsz, CONV_WIDTH - 1, d), F32),
            jax.ShapeDtypeStruct((bsz, 1, d), F32),
        ],
        scratch_shapes=[
            pltpu.VMEM((tl + CONV_PAD, d), F32),
            pltpu.VMEM((1, d), F32),
            pltpu.VMEM((tl, d), F32),
            pltpu.VMEM((tl, d), F32),
        ],
        compiler_params=_params("arbitrary", "arbitrary"),
        name="lru_seq",
    )(gx, gx, conv_buf, h0.reshape(bsz, 1, d), cw, vec(cb), wrg, vec(brg), wig, vec(big), vec(lam))


def _block_diag_gate(w):
    nb, bw, _ = w.shape
    per = MXU_DIM // bw
    w4 = w.reshape(nb // per, per, bw, bw)
    eye = jnp.eye(per, dtype=w.dtype)
    out = w4[:, :, :, None, :] * eye[None, :, None, :, None]
    return out.reshape(nb // per, MXU_DIM, MXU_DIM).astype(BF16)


def _ssd_seq_kernel(z_ref, xs_ref, bc_ref, dt_ref, cbuf_ref, h0_ref, cw_ref, cb_ref, dtb_ref,
                    alog_ref, dsk_ref, nw_ref, out_ref, nconv_ref, st_ref, xx_ref, y_ref):
    c = pl.program_id(1)
    q = xs_ref.shape[0]
    d_inner = xs_ref.shape[1]
    n_heads = st_ref.shape[1]
    p_dim = st_ref.shape[2]
    n_state = st_ref.shape[3]
    hpg = n_heads // SSD_GROUPS

    @pl.when(c == 0)
    def _():
        xx_ref[CONV_HIST:CONV_PAD, :] = cbuf_ref[0]
        st_ref[...] = h0_ref[...]

    xx_ref[CONV_PAD:CONV_PAD + q, 0:d_inner] = xs_ref[...]
    xx_ref[CONV_PAD:CONV_PAD + q, d_inner:] = bc_ref[...]
    act = _silu(_conv_tile(xx_ref, q, cw_ref[...], cb_ref[...]))
    tail = xx_ref[q + CONV_HIST:q + CONV_PAD, :]
    xx_ref[CONV_HIST:CONV_PAD, :] = tail

    @pl.when(c == pl.num_programs(1) - 1)
    def _():
        nconv_ref[0] = tail

    dt = _softplus(dt_ref[...] + dtb_ref[...])
    adt = dt * (-jnp.exp(alog_ref[...]))
    row = lax.broadcasted_iota(jnp.int32, (q, q), 0)
    col = lax.broadcasted_iota(jnp.int32, (q, q), 1)
    causal = col <= row
    tril = causal.astype(F32)
    cs = jnp.dot(tril, adt, preferred_element_type=F32, precision=HIGHEST)
    eye_h = (lax.broadcasted_iota(jnp.int32, (n_heads, n_heads), 0)
             == lax.broadcasted_iota(jnp.int32, (n_heads, n_heads), 1)).astype(F32)
    cs_t = lax.dot_general(eye_h, cs, NT_DIMS, preferred_element_type=F32, precision=HIGHEST)
    dsk = dsk_ref[...]
    b_off = d_inner
    c_off = d_inner + SSD_GROUPS * n_state

    for g in range(SSD_GROUPS):
        bg = act[:, b_off + g * n_state:b_off + (g + 1) * n_state].astype(BF16)
        cg = act[:, c_off + g * n_state:c_off + (g + 1) * n_state].astype(BF16)
        cb_mat = lax.dot_general(cg, bg, NT_DIMS, preferred_element_type=F32)
        for e in range(hpg):
            h = g * hpg + e
            cs_h = cs[:, h:h + 1]
            cs_last = cs[q - 1:q, h:h + 1]
            lmat = jnp.exp(jnp.where(causal, cs_h - cs_t[h:h + 1, :], -jnp.inf))
            xs_h = act[:, h * p_dim:(h + 1) * p_dim]
            xd = xs_h * dt[:, h:h + 1]
            y = jnp.dot((cb_mat * lmat).astype(BF16), xd.astype(BF16), preferred_element_type=F32)
            prev = st_ref[0, h]
            y_off = lax.dot_general(cg, prev.astype(BF16), NT_DIMS, preferred_element_type=F32)
            y = y + y_off * jnp.exp(cs_h)
            xdec = (xd * jnp.exp(cs_last - cs_h)).astype(BF16)
            st_new = lax.dot_general(xdec, bg, TN_DIMS, preferred_element_type=F32)
            st_ref[0, h] = jnp.exp(cs_last) * prev + st_new
            y_ref[:, h * p_dim:(h + 1) * p_dim] = y + dsk[:, h:h + 1] * xs_h

    gated = y_ref[...] * _silu(z_ref[...])
    gw = d_inner // SSD_GROUPS
    nw = nw_ref[...]
    for g in range(SSD_GROUPS):
        seg = gated[:, g * gw:(g + 1) * gw]
        ms = jnp.mean(seg * seg, axis=-1, keepdims=True)
        out_ref[:, g * gw:(g + 1) * gw] = (seg * lax.rsqrt(ms + RMS_EPS) * nw[:, g * gw:(g + 1) * gw]).astype(BF16)


def _ssd_seq(zx, dt_raw, row0, bsz, seqlen, q, conv_buf, h0, cw, cb, dtb, alog, dsk, nw):
    n_heads, p_dim, n_state = h0.shape[1:]
    d_inner = n_heads * p_dim
    conv_dim = cw.shape[1]
    nt = seqlen // q
    blk0 = row0 // q
    full2 = lambda b, c: (0, 0)
    col = lambda j: (lambda b, c: (blk0 + b * nt + c, j))
    return pl.pallas_call(
        _ssd_seq_kernel,
        grid=(bsz, nt),
        in_specs=[
            pl.BlockSpec((q, d_inner), col(0)),
            pl.BlockSpec((q, d_inner), col(1)),
            pl.BlockSpec((q, d_inner), col(2)),
            pl.BlockSpec((q, n_heads), col(0)),
            pl.BlockSpec((1, CONV_WIDTH - 1, conv_dim), lambda b, c: (b, 0, 0)),
            pl.BlockSpec((1, n_heads, p_dim, n_state), lambda b, c: (b, 0, 0, 0)),
            pl.BlockSpec((CONV_WIDTH, conv_dim), full2),
            pl.BlockSpec((1, conv_dim), full2),
            pl.BlockSpec((1, n_heads), full2),
            pl.BlockSpec((1, n_heads), full2),
            pl.BlockSpec((1, n_heads), full2),
            pl.BlockSpec((1, d_inner), full2),
        ],
        out_specs=[
            pl.BlockSpec((q, d_inner), lambda b, c: (b * nt + c, 0)),
            pl.BlockSpec((1, CONV_WIDTH - 1, conv_dim), lambda b, c: (b, 0, 0)),
            pl.BlockSpec((1, n_heads, p_dim, n_state), lambda b, c: (b, 0, 0, 0)),
        ],
        out_shape=[
            jax.ShapeDtypeStruct((bsz * seqlen, d_inner), BF16),
            jax.ShapeDtypeStruct((bsz, CONV_WIDTH - 1, conv_dim), F32),
            jax.ShapeDtypeStruct((bsz, n_heads, p_dim, n_state), F32),
        ],
        scratch_shapes=[
            pltpu.VMEM((q + CONV_PAD, conv_dim), F32),
            pltpu.VMEM((q, d_inner), F32),
        ],
        compiler_params=_params("arbitrary", "arbitrary"),
        name="ssd_seq",
    )(zx, zx, zx, dt_raw, conv_buf, h0, cw, cb.reshape(1, -1), dtb.reshape(1, -1),
      alog.reshape(1, -1), dsk.reshape(1, -1), nw.reshape(1, -1))


def _proj_ln_router_kernel(alpha, h_ref, w_ref, x_ref, g_ref, b_ref, rw_ref, o_ref, s_ref, wb_ref):
    @pl.when(pl.program_id(0) == 0)
    def _():
        wb_ref[...] = w_ref[...].astype(BF16)

    mix = jnp.dot(h_ref[...], wb_ref[...], preferred_element_type=F32)
    x1 = _layer_norm(alpha * x_ref[...] + mix, g_ref[...], b_ref[...])
    o_ref[...] = x1
    logits = jnp.dot(x1, rw_ref[...], preferred_element_type=F32, precision=HIGHEST)
    s_ref[...] = _sigmoid(logits)


def _proj_ln_router(h, w, wl, x, alpha, ln_g, ln_b, router_w, layer):
    T, K = h.shape
    D = x.shape[1]
    E = router_w.shape[2]
    tm = TM_PROJ
    return pl.pallas_call(
        functools.partial(_proj_ln_router_kernel, alpha),
        grid=(T // tm,),
        in_specs=[
            pl.BlockSpec((tm, K), lambda i: (i, 0)),
            pl.BlockSpec((None, K, D), lambda i: (wl, 0, 0)),
            pl.BlockSpec((tm, D), lambda i: (i, 0)),
            pl.BlockSpec((None, 1, D), lambda i: (layer, 0, 0)),
            pl.BlockSpec((None, 1, D), lambda i: (layer, 0, 0)),
            pl.BlockSpec((None, D, E), lambda i: (layer, 0, 0)),
        ],
        out_specs=[pl.BlockSpec((tm, D), lambda i: (i, 0)),
                   pl.BlockSpec((tm, E), lambda i: (i, 0))],
        out_shape=[jax.ShapeDtypeStruct((T, D), F32), jax.ShapeDtypeStruct((T, E), F32)],
        scratch_shapes=[pltpu.VMEM((K, D), BF16)],
        compiler_params=_params("arbitrary"),
        name="proj_ln_router",
    )(h, w, x, ln_g.reshape(-1, 1, D), ln_b.reshape(-1, 1, D), router_w)


def _start_row_gather(idx_ref, base, n, src_hbm, dst, sem):
    def body(i, carry):
        r = idx_ref[base + i]
        pltpu.make_async_copy(src_hbm.at[pl.ds(r, 1), :], dst.at[pl.ds(i, 1), :], sem).start()
        return carry
    lax.fori_loop(0, n, body, 0, unroll=8)


def _wait_row_gather(n, src_hbm, dst, sem):
    pltpu.make_async_copy(src_hbm.at[pl.ds(0, n), :], dst, sem).wait()


def _expert_kernel(be_ref, nu_ref, tok_ref, x_hbm, wg_ref, wu_ref, wd_ref, y_ref,
                   xbuf, wgb, wub, wdb, sem):
    b = pl.program_id(0)
    bm = xbuf.shape[0]

    @pl.when(b < nu_ref[0])
    def _():
        _start_row_gather(tok_ref, b * bm, bm, x_hbm, xbuf, sem)
        new_expert = jnp.logical_or(b == 0, be_ref[b] != be_ref[jnp.maximum(b - 1, 0)])

        @pl.when(new_expert)
        def _():
            wgb[...] = wg_ref[...].astype(BF16)
            wub[...] = wu_ref[...].astype(BF16)
            wdb[...] = wd_ref[...].astype(BF16)

        _wait_row_gather(bm, x_hbm, xbuf, sem)
        xb = xbuf[...].astype(BF16)
        hg = jnp.dot(xb, wgb[...], preferred_element_type=F32)
        hu = jnp.dot(xb, wub[...], preferred_element_type=F32)
        hh = (_silu(hg) * hu).astype(BF16)
        y_ref[...] = jnp.dot(hh, wdb[...], preferred_element_type=F32)

    @pl.when(b >= nu_ref[0])
    def _():
        y_ref[...] = jnp.zeros_like(y_ref)


def _experts(x, slot_tok, block_e, n_used, wg, wu, wd, layer):
    T, D = x.shape
    DE = wg.shape[3]
    bm = BM_EXPERT
    nb = slot_tok.shape[0] // bm
    grid_spec = pltpu.PrefetchScalarGridSpec(
        num_scalar_prefetch=3,
        grid=(nb,),
        in_specs=[
            pl.BlockSpec(memory_space=pl.ANY),
            pl.BlockSpec((None, None, D, DE), lambda b, be, nu, tok: (layer, be[b], 0, 0)),
            pl.BlockSpec((None, None, D, DE), lambda b, be, nu, tok: (layer, be[b], 0, 0)),
            pl.BlockSpec((None, None, DE, D), lambda b, be, nu, tok: (layer, be[b], 0, 0)),
        ],
        out_specs=pl.BlockSpec((bm, D), lambda b, be, nu, tok: (b, 0)),
        scratch_shapes=[
            pltpu.VMEM((bm, D), F32),
            pltpu.VMEM((D, DE), BF16),
            pltpu.VMEM((D, DE), BF16),
            pltpu.VMEM((DE, D), BF16),
            pltpu.SemaphoreType.DMA,
        ],
    )
    return pl.pallas_call(
        _expert_kernel,
        grid_spec=grid_spec,
        out_shape=jax.ShapeDtypeStruct((nb * bm, D), F32),
        compiler_params=_params("arbitrary"),
        name="experts",
    )(block_e, n_used, slot_tok, x, wg, wu, wd)


def _combine_kernel(alpha, pos_ref, gates_ref, x_ref, y_hbm, sg_ref, su_ref, sd_ref, g_ref, b_ref,
                    o_ref, ybuf, sgb, sub, sdb, sem):
    i = pl.program_id(0)
    tm = x_ref.shape[0]
    n_rows = ybuf.shape[0]
    _start_row_gather(pos_ref, i * n_rows, n_rows, y_hbm, ybuf, sem)

    @pl.when(i == 0)
    def _():
        sgb[...] = sg_ref[...].astype(BF16)
        sub[...] = su_ref[...].astype(BF16)
        sdb[...] = sd_ref[...].astype(BF16)

    x = x_ref[...]
    xb = x.astype(BF16)
    hg = jnp.dot(xb, sgb[...], preferred_element_type=F32)
    hu = jnp.dot(xb, sub[...], preferred_element_type=F32)
    acc = jnp.dot((_silu(hg) * hu).astype(BF16), sdb[...], preferred_element_type=F32)
    _wait_row_gather(n_rows, y_hbm, ybuf, sem)
    gates = gates_ref[...]
    for k in range(TOP_K):
        acc = acc + gates[:, k:k + 1] * ybuf[k * tm:(k + 1) * tm, :]
    o_ref[...] = _layer_norm(alpha * x + acc, g_ref[...], b_ref[...])


def _combine(x, y_sorted, pos_tiles, gates, sg, su, sd, alpha, ln_g, ln_b, layer):
    T, D = x.shape
    DS = sg.shape[2]
    tm = TM_COMBINE
    grid_spec = pltpu.PrefetchScalarGridSpec(
        num_scalar_prefetch=1,
        grid=(T // tm,),
        in_specs=[
            pl.BlockSpec((tm, TOP_K), lambda i, pos: (i, 0)),
            pl.BlockSpec((tm, D), lambda i, pos: (i, 0)),
            pl.BlockSpec(memory_space=pl.ANY),
            pl.BlockSpec((None, D, DS), lambda i, pos: (layer, 0, 0)),
            pl.BlockSpec((None, D, DS), lambda i, pos: (layer, 0, 0)),
            pl.BlockSpec((None, DS, D), lambda i, pos: (layer, 0, 0)),
            pl.BlockSpec((None, 1, D), lambda i, pos: (layer, 0, 0)),
            pl.BlockSpec((None, 1, D), lambda i, pos: (layer, 0, 0)),
        ],
        out_specs=pl.BlockSpec((tm, D), lambda i, pos: (i, 0)),
        scratch_shapes=[
            pltpu.VMEM((TOP_K * tm, D), F32),
            pltpu.VMEM((D, DS), BF16),
            pltpu.VMEM((D, DS), BF16),
            pltpu.VMEM((DS, D), BF16),
            pltpu.SemaphoreType.DMA,
        ],
    )
    return pl.pallas_call(
        functools.partial(_combine_kernel, alpha),
        grid_spec=grid_spec,
        out_shape=jax.ShapeDtypeStruct((T, D), F32),
        compiler_params=_params("arbitrary"),
        name="combine",
    )(pos_tiles, gates, x, y_sorted, sg, su, sd, ln_g.reshape(-1, 1, D), ln_b.reshape(-1, 1, D))


def _route(scores, bias):
    T, E = scores.shape
    per = E // N_EXPERT_GROUPS
    biased = scores + bias
    group_score = lax.top_k(biased.reshape(T, N_EXPERT_GROUPS, per), 2)[0].sum(-1)
    _, gidx = lax.top_k(group_score, TOPK_GROUPS)
    gmask = jax.nn.one_hot(gidx, N_EXPERT_GROUPS).sum(1) > 0
    emask = jnp.repeat(gmask, per, axis=1)
    _, eidx = lax.top_k(jnp.where(emask, biased, -jnp.inf), TOP_K)
    g = jnp.take_along_axis(scores, eidx, axis=1)
    g = g / jnp.sum(g, -1, keepdims=True) * ROUTE_SCALE
    return eidx, g


def _dispatch(eidx, n_experts):
    T = eidx.shape[0]
    A = T * TOP_K
    bm = BM_EXPERT
    nb = -(-A // bm) + n_experts
    flat_e = eidx.reshape(A)
    onehot = (flat_e[:, None] == jnp.arange(n_experts, dtype=flat_e.dtype)[None, :]).astype(jnp.int32)
    csum = jnp.cumsum(onehot, axis=0)
    rank = jnp.sum(csum * onehot, axis=1) - 1
    counts = csum[-1]
    padded = (counts + bm - 1) // bm * bm
    pends = jnp.cumsum(padded)
    dest = (pends - padded)[flat_e] + rank
    slot_tok = jnp.zeros((nb * bm,), jnp.int32).at[dest].set(jnp.arange(A, dtype=jnp.int32) // TOP_K)
    block_e = jnp.minimum(jnp.searchsorted(pends, jnp.arange(nb, dtype=jnp.int32) * bm, side='right'),
                          n_experts - 1).astype(jnp.int32)
    n_used = (pends[-1:] // bm).astype(jnp.int32)
    tm = TM_COMBINE
    pos_tiles = dest.astype(jnp.int32).reshape(T // tm, tm, TOP_K).transpose(0, 2, 1).reshape(-1)
    return slot_tok, block_e, n_used, pos_tiles


def _moe(x1, scores, alpha, layer, router_bias, wg, wu, wd, sg, su, sd, ln_g, ln_b):
    eidx, gates = _route(scores, router_bias[layer])
    slot_tok, block_e, n_used, pos_tiles = _dispatch(eidx, scores.shape[1])
    y_sorted = _experts(x1, slot_tok, block_e, n_used, wg, wu, wd, layer)
    return _combine(x1, y_sorted, pos_tiles, gates, sg, su, sd, alpha, ln_g, ln_b, layer)


def kernel(x_prompt, x_sample, state_lru_conv, state_lru_h, state_ssd_conv, state_ssd, lru_w_in, lru_conv_w, lru_conv_b, lru_w_rgate, lru_b_rgate, lru_w_igate, lru_b_igate, lru_lambda, lru_w_out, ssd_w_in, ssd_conv_w, ssd_conv_b, ssd_dt_bias, ssd_a_log, ssd_d, ssd_norm_w, ssd_w_out, ln_mix_g, ln_mix_b, ln_ffn_g, ln_ffn_b, router_w, router_bias, moe_w_gate, moe_w_up, moe_w_down, shared_w_gate, shared_w_up, shared_w_down):
    bp, lp, d_model = x_prompt.shape
    bs, ls, _ = x_sample.shape
    depth = ln_mix_g.shape[0]
    alpha = (2.0 * depth) ** 0.25
    tp, ts = bp * lp, bs * ls
    n_heads, p_dim, n_state = state_ssd.shape[2:]
    d_inner = n_heads * p_dim
    conv_dim = ssd_conv_w.shape[2]
    d_rnn = lru_conv_w.shape[2]

    x = jnp.concatenate([x_prompt.reshape(tp, d_model), x_sample.reshape(ts, d_model)], axis=0)
    tl_p = min(TL_LRU, lp)
    tl_s = min(TL_LRU, ls)
    q_p = min(SSD_CHUNK, lp)
    q_s = min(SSD_CHUNK, ls)

    p_lru_conv, p_lru_h, p_ssd_conv, p_ssd = [], [], [], []
    s_lru_conv, s_lru_h, s_ssd_conv, s_ssd = [], [], [], []
    for i in range(depth):
        j = i // 2
        if i % 2 == 0:
            gx = _in_proj(x, lru_w_in, j, 2 * d_rnn, 512)
            wrg = _block_diag_gate(lru_w_rgate[j])
            wig = _block_diag_gate(lru_w_igate[j])
            common = (lru_conv_w[j], lru_conv_b[j], wrg, lru_b_rgate[j], wig, lru_b_igate[j], lru_lambda[j])
            hp, cp, lp_h = _lru_seq(gx, 0, bp, lp, tl_p, jnp.zeros((bp, CONV_WIDTH - 1, d_rnn), F32),
                                    jnp.zeros((bp, d_rnn), F32), *common)
            hs, cs_, ls_h = _lru_seq(gx, tp, bs, ls, tl_s, state_lru_conv[j], state_lru_h[j], *common)
            p_lru_conv.append(cp)
            p_lru_h.append(lp_h.reshape(bp, d_rnn))
            s_lru_conv.append(cs_)
            s_lru_h.append(ls_h.reshape(bs, d_rnn))
            mixed = jnp.concatenate([hp, hs], axis=0)
            w_out, wl = lru_w_out, j
        else:
            zx = _in_proj(x, ssd_w_in, j, d_inner + conv_dim, 512)
            dt_raw = _small_proj(x, ssd_w_in[j][:, d_inner + conv_dim:])
            common = (ssd_conv_w[j], ssd_conv_b[j], ssd_dt_bias[j], ssd_a_log[j], ssd_d[j], ssd_norm_w[j])
            hp, cp, sp = _ssd_seq(zx, dt_raw, 0, bp, lp, q_p, jnp.zeros((bp, CONV_WIDTH - 1, conv_dim), F32),
                                  jnp.zeros((bp, n_heads, p_dim, n_state), F32), *common)
            hs, cs_, ss = _ssd_seq(zx, dt_raw, tp, bs, ls, q_s, state_ssd_conv[j], state_ssd[j], *common)
            p_ssd_conv.append(cp)
            p_ssd.append(sp)
            s_ssd_conv.append(cs_)
            s_ssd.append(ss)
            mixed = jnp.concatenate([hp, hs], axis=0)
            w_out, wl = ssd_w_out, j
        x1, scores = _proj_ln_router(mixed, w_out, wl, x, alpha, ln_mix_g, ln_mix_b, router_w, i)
        x = _moe(x1, scores, alpha, i, router_bias, moe_w_gate, moe_w_up, moe_w_down,
                 shared_w_gate, shared_w_up, shared_w_down, ln_ffn_g, ln_ffn_b)

    y_prompt = x[:tp].reshape(bp, lp, d_model)
    y_sample = x[tp:].reshape(bs, ls, d_model)
    return (y_prompt, y_sample,
            jnp.stack(p_lru_conv), jnp.stack(p_lru_h), jnp.stack(p_ssd_conv), jnp.stack(p_ssd),
            jnp.stack(s_lru_conv), jnp.stack(s_lru_h), jnp.stack(s_ssd_conv), jnp.stack(s_ssd))
```

```python
import functools
import math

import jax
import jax.numpy as jnp
from jax import lax
from jax.experimental import pallas as pl
from jax.experimental.pallas import tpu as pltpu

F32 = jnp.float32
BF16 = jnp.bfloat16

LN_EPS = 1e-5
RMS_EPS = 1e-5
CONV_WIDTH = 4
LRU_C = 8.0
SSD_CHUNK = 64
SSD_GROUPS = 8
TOP_K = 6
N_EXPERT_GROUPS = 8
TOPK_GROUPS = 4
ROUTE_SCALE = 1.0

LANE = 128
SUBLANE = 8
MXU_DIM = 256
VMEM_LIMIT = 56 * 1024 * 1024

TM_PROJ = 512
TM_COMBINE = 128
BM_EXPERT = 256
TL_LRU = 256

HIGHEST = lax.Precision.HIGHEST
NT_DIMS = (((1,), (1,)), ((), ()))
TN_DIMS = (((0,), (0,)), ((), ()))


def _sigmoid(x):
    return 1.0 / (1.0 + jnp.exp(-x))


def _silu(x):
    return x * _sigmoid(x)


def _softplus(x):
    return jnp.maximum(x, 0.0) + jnp.log1p(jnp.exp(-jnp.abs(x)))


def _gelu_tanh(x):
    c = math.sqrt(2.0 / math.pi)
    return 0.5 * x * (1.0 + jnp.tanh(c * (x + 0.044715 * (x * x * x))))


def _layer_norm(v, g, b):
    mu = jnp.mean(v, axis=-1, keepdims=True)
    d = v - mu
    var = jnp.mean(d * d, axis=-1, keepdims=True)
    return d * lax.rsqrt(var + LN_EPS) * g + b


def _params(*sem):
    return pltpu.CompilerParams(dimension_semantics=sem, vmem_limit_bytes=VMEM_LIMIT)


def _mm_kernel(x_ref, w_ref, o_ref):
    o_ref[...] = jnp.dot(x_ref[...].astype(BF16), w_ref[...].astype(BF16),
                         preferred_element_type=F32)


def _in_proj(x, w, layer, n_cols, tn):
    T, K = x.shape
    tm = TM_PROJ
    return pl.pallas_call(
        _mm_kernel,
        grid=(T // tm, n_cols // tn),
        in_specs=[pl.BlockSpec((tm, K), lambda i, j: (i, 0)),
                  pl.BlockSpec((None, K, tn), lambda i, j: (layer, 0, j))],
        out_specs=pl.BlockSpec((tm, tn), lambda i, j: (i, j)),
        out_shape=jax.ShapeDtypeStruct((T, n_cols), F32),
        compiler_params=_params("parallel", "parallel"),
        name="in_proj",
    )(x, w)


def _small_proj(x, w):
    T, K = x.shape
    N = w.shape[1]
    tm = TM_PROJ
    return pl.pallas_call(
        _mm_kernel,
        grid=(T // tm,),
        in_specs=[pl.BlockSpec((tm, K), lambda i: (i, 0)),
                  pl.BlockSpec((K, N), lambda i: (0, 0))],
        out_specs=pl.BlockSpec((tm, N), lambda i: (i, 0)),
        out_shape=jax.ShapeDtypeStruct((T, N), F32),
        compiler_params=_params("parallel"),
        name="dt_proj",
    )(x, w)


CONV_PAD = SUBLANE
CONV_HIST = CONV_PAD - (CONV_WIDTH - 1)


def _conv_tile(xx_ref, n, cw, cb):
    acc = cb + cw[CONV_WIDTH - 1:CONV_WIDTH] * xx_ref[CONV_PAD:CONV_PAD + n, :]
    for k in range(CONV_WIDTH - 1):
        off = CONV_HIST + k
        acc = acc + cw[k:k + 1] * xx_ref[off:off + n, :]
    return acc


def _lru_seq_kernel(gate_ref, xr_ref, cbuf_ref, h0_ref, cw_ref, cb_ref, wrg_ref, brg_ref,
                    wig_ref, big_ref, lam_ref, out_ref, nconv_ref, hlast_ref,
                    xx_ref, h_ref, a_ref, u_ref):
    c = pl.program_id(1)
    n = xr_ref.shape[0]
    d = xr_ref.shape[1]

    @pl.when(c == 0)
    def _():
        xx_ref[CONV_HIST:CONV_PAD, :] = cbuf_ref[0]
        h_ref[...] = h0_ref[0]

    xx_ref[CONV_PAD:CONV_PAD + n, :] = xr_ref[...]
    xr = _conv_tile(xx_ref, n, cw_ref[...], cb_ref[...])
    tail = xx_ref[n + CONV_HIST:n + CONV_PAD, :]
    xx_ref[CONV_HIST:CONV_PAD, :] = tail

    xb = xr.astype(BF16)
    nblk = d // MXU_DIM
    r_parts, i_parts = [], []
    for g in range(nblk):
        seg = xb[:, g * MXU_DIM:(g + 1) * MXU_DIM]
        r_parts.append(jnp.dot(seg, wrg_ref[g], preferred_element_type=F32))
        i_parts.append(jnp.dot(seg, wig_ref[g], preferred_element_type=F32))
    r = _sigmoid(jnp.concatenate(r_parts, axis=-1) + brg_ref[...])
    ig = _sigmoid(jnp.concatenate(i_parts, axis=-1) + big_ref[...])
    log_sig_lam = -_softplus(-lam_ref[...])
    log_a = LRU_C * r * log_sig_lam
    a = jnp.exp(log_a)
    a_ref[...] = a
    u_ref[...] = jnp.sqrt(-jnp.tanh(log_a) * (a * a + 1.0)) * (ig * xr)

    def step(t, h):
        h = a_ref[pl.ds(t, 1), :] * h + u_ref[pl.ds(t, 1), :]
        u_ref[pl.ds(t, 1), :] = h
        return h

    h = lax.fori_loop(0, n, step, h_ref[...], unroll=8)
    h_ref[...] = h
    out_ref[...] = (u_ref[...] * _gelu_tanh(gate_ref[...])).astype(BF16)

    @pl.when(c == pl.num_programs(1) - 1)
    def _():
        nconv_ref[0] = tail
        hlast_ref[0] = h


def _lru_seq(gx, row0, bsz, seqlen, tl, conv_buf, h0, cw, cb, wrg, brg, wig, big, lam):
    d = cw.shape[1]
    nt = seqlen // tl
    blk0 = row0 // tl
    row_map = lambda b, c: (blk0 + b * nt + c, 0)
    vec = lambda a: a.reshape(1, d)
    full2 = lambda b, c: (0, 0)
    full3 = lambda b, c: (0, 0, 0)
    return pl.pallas_call(
        _lru_seq_kernel,
        grid=(bsz, nt),
        in_specs=[
            pl.BlockSpec((tl, d), row_map),
            pl.BlockSpec((tl, d), lambda b, c: (blk0 + b * nt + c, 1)),
            pl.BlockSpec((1, CONV_WIDTH - 1, d), lambda b, c: (b, 0, 0)),
            pl.BlockSpec((1, 1, d), lambda b, c: (b, 0, 0)),
            pl.BlockSpec((CONV_WIDTH, d), full2),
            pl.BlockSpec((1, d), full2),
            pl.BlockSpec(wrg.shape, full3),
            pl.BlockSpec((1, d), full2),
            pl.BlockSpec(wig.shape, full3),
            pl.BlockSpec((1, d), full2),
            pl.BlockSpec((1, d), full2),
        ],
        out_specs=[
            pl.BlockSpec((tl, d), lambda b, c: (b * nt + c, 0)),
            pl.BlockSpec((1, CONV_WIDTH - 1, d), lambda b, c: (b, 0, 0)),
            pl.BlockSpec((1, 1, d), lambda b, c: (b, 0, 0)),
        ],
        out_shape=[
            jax.ShapeDtypeStruct((bsz * seqlen, d), BF16),
            jax.ShapeDtypeStruct((bsz, CONV_WIDTH - 1, d), F32),
            jax.ShapeDtypeStruct((bsz, 1, d), F32),
        ],
        scratch_shapes=[
            pltpu.VMEM((tl + CONV_PAD, d), F32),
            pltpu.VMEM((1, d), F32),
            pltpu.VMEM((tl, d), F32),
            pltpu.VMEM((tl, d), F32),
        ],
        compiler_params=_params("arbitrary", "arbitrary"),
        name="lru_seq",
    )(gx, gx, conv_buf, h0.reshape(bsz, 1, d), cw, vec(cb), wrg, vec(brg), wig, vec(big), vec(lam))


def _block_diag_gate(w):
    nb, bw, _ = w.shape
    per = MXU_DIM // bw
    w4 = w.reshape(nb // per, per, bw, bw)
    eye = jnp.eye(per, dtype=w.dtype)
    out = w4[:, :, :, None, :] * eye[None, :, None, :, None]
    return out.reshape(nb // per, MXU_DIM, MXU_DIM).astype(BF16)


def _ssd_seq_kernel(z_ref, xs_ref, bc_ref, dt_ref, cbuf_ref, h0_ref, cw_ref, cb_ref, dtb_ref,
                    alog_ref, dsk_ref, nw_ref, out_ref, nconv_ref, st_ref, xx_ref, y_ref):
    c = pl.program_id(1)
    q = xs_ref.shape[0]
    d_inner = xs_ref.shape[1]
    n_heads = st_ref.shape[1]
    p_dim = st_ref.shape[2]
    n_state = st_ref.shape[3]
    hpg = n_heads // SSD_GROUPS

    @pl.when(c == 0)
    def _():
        xx_ref[CONV_HIST:CONV_PAD, :] = cbuf_ref[0]
        st_ref[...] = h0_ref[...]

    xx_ref[CONV_PAD:CONV_PAD + q, 0:d_inner] = xs_ref[...]
    xx_ref[CONV_PAD:CONV_PAD + q, d_inner:] = bc_ref[...]
    act = _silu(_conv_tile(xx_ref, q, cw_ref[...], cb_ref[...]))
    tail = xx_ref[q + CONV_HIST:q + CONV_PAD, :]
    xx_ref[CONV_HIST:CONV_PAD, :] = tail

    @pl.when(c == pl.num_programs(1) - 1)
    def _():
        nconv_ref[0] = tail

    dt = _softplus(dt_ref[...] + dtb_ref[...])
    adt = dt * (-jnp.exp(alog_ref[...]))
    row = lax.broadcasted_iota(jnp.int32, (q, q), 0)
    col = lax.broadcasted_iota(jnp.int32, (q, q), 1)
    causal = col <= row
    tril = causal.astype(F32)
    cs = jnp.dot(tril, adt, preferred_element_type=F32, precision=HIGHEST)
    eye_h = (lax.broadcasted_iota(jnp.int32, (n_heads, n_heads), 0)
             == lax.broadcasted_iota(jnp.int32, (n_heads, n_heads), 1)).astype(F32)
    cs_t = lax.dot_general(eye_h, cs, NT_DIMS, preferred_element_type=F32, precision=HIGHEST)
    dsk = dsk_ref[...]
    b_off = d_inner
    c_off = d_inner + SSD_GROUPS * n_state

    for g in range(SSD_GROUPS):
        bg = act[:, b_off + g * n_state:b_off + (g + 1) * n_state].astype(BF16)
        cg = act[:, c_off + g * n_state:c_off + (g + 1) * n_state].astype(BF16)
        cb_mat = lax.dot_general(cg, bg, NT_DIMS, preferred_element_type=F32)
        for e in range(hpg):
            h = g * hpg + e
            cs_h = cs[:, h:h + 1]
            cs_last = cs[q - 1:q, h:h + 1]
            lmat = jnp.exp(jnp.where(causal, cs_h - cs_t[h:h + 1, :], -jnp.inf))
            xs_h = act[:, h * p_dim:(h + 1) * p_dim]
            xd = xs_h * dt[:, h:h + 1]
            y = jnp.dot((cb_mat * lmat).astype(BF16), xd.astype(BF16), preferred_element_type=F32)
            prev = st_ref[0, h]
            y_off = lax.dot_general(cg, prev.astype(BF16), NT_DIMS, preferred_element_type=F32)
            y = y + y_off * jnp.exp(cs_h)
            xdec = (xd * jnp.exp(cs_last - cs_h)).astype(BF16)
            st_new = lax.dot_general(xdec, bg, TN_DIMS, preferred_element_type=F32)
            st_ref[0, h] = jnp.exp(cs_last) * prev + st_new
            y_ref[:, h * p_dim:(h + 1) * p_dim] = y + dsk[:, h:h + 1] * xs_h

    gated = y_ref[...] * _silu(z_ref[...])
    gw = d_inner // SSD_GROUPS
    nw = nw_ref[...]
    for g in range(SSD_GROUPS):
        seg = gated[:, g * gw:(g + 1) * gw]
        ms = jnp.mean(seg * seg, axis=-1, keepdims=True)
        out_ref[:, g * gw:(g + 1) * gw] = (seg * lax.rsqrt(ms + RMS_EPS) * nw[:, g * gw:(g + 1) * gw]).astype(BF16)


def _ssd_seq(zx, dt_raw, row0, bsz, seqlen, q, conv_buf, h0, cw, cb, dtb, alog, dsk, nw):
    n_heads, p_dim, n_state = h0.shape[1:]
    d_inner = n_heads * p_dim
    conv_dim = cw.shape[1]
    nt = seqlen // q
    blk0 = row0 // q
    full2 = lambda b, c: (0, 0)
    col = lambda j: (lambda b, c: (blk0 + b * nt + c, j))
    return pl.pallas_call(
        _ssd_seq_kernel,
        grid=(bsz, nt),
        in_specs=[
            pl.BlockSpec((q, d_inner), col(0)),
            pl.BlockSpec((q, d_inner), col(1)),
            pl.BlockSpec((q, d_inner), col(2)),
            pl.BlockSpec((q, n_heads), col(0)),
            pl.BlockSpec((1, CONV_WIDTH - 1, conv_dim), lambda b, c: (b, 0, 0)),
            pl.BlockSpec((1, n_heads, p_dim, n_state), lambda b, c: (b, 0, 0, 0)),
            pl.BlockSpec((CONV_WIDTH, conv_dim), full2),
            pl.BlockSpec((1, conv_dim), full2),
            pl.BlockSpec((1, n_heads), full2),
            pl.BlockSpec((1, n_heads), full2),
            pl.BlockSpec((1, n_heads), full2),
            pl.BlockSpec((1, d_inner), full2),
        ],
        out_specs=[
            pl.BlockSpec((q, d_inner), lambda b, c: (b * nt + c, 0)),
            pl.BlockSpec((1, CONV_WIDTH - 1, conv_dim), lambda b, c: (b, 0, 0)),
            pl.BlockSpec((1, n_heads, p_dim, n_state), lambda b, c: (b, 0, 0, 0)),
        ],
        out_shape=[
            jax.ShapeDtypeStruct((bsz * seqlen, d_inner), BF16),
            jax.ShapeDtypeStruct((bsz, CONV_WIDTH - 1, conv_dim), F32),
            jax.ShapeDtypeStruct((bsz, n_heads, p_dim, n_state), F32),
        ],
        scratch_shapes=[
            pltpu.VMEM((q + CONV_PAD, conv_dim), F32),
            pltpu.VMEM((q, d_inner), F32),
        ],
        compiler_params=_params("arbitrary", "arbitrary"),
        name="ssd_seq",
    )(zx, zx, zx, dt_raw, conv_buf, h0, cw, cb.reshape(1, -1), dtb.reshape(1, -1),
      alog.reshape(1, -1), dsk.reshape(1, -1), nw.reshape(1, -1))


def _first_argmax(v, rid, n):
    m = jnp.max(v, axis=0, keepdims=True)
    idx = jnp.min(jnp.where(v == m, rid, float(n)), axis=0, keepdims=True)
    return m, idx


def _route_tile(scores_t, bias_col):
    n_exp, tm = scores_t.shape
    per = n_exp // N_EXPERT_GROUPS
    neg = -jnp.inf
    biased = scores_t + bias_col
    rid_g = lax.broadcasted_iota(jnp.int32, (per, tm), 0).astype(F32)
    tiles, gs_rows = [], []
    for g in range(N_EXPERT_GROUPS):
        v = biased[g * per:(g + 1) * per, :]
        tiles.append(v)
        m1, i1 = _first_argmax(v, rid_g, per)
        m2 = jnp.max(jnp.where(rid_g == i1, neg, v), axis=0, keepdims=True)
        gs_rows.append(m1 + m2)
    gs = jnp.concatenate(gs_rows, axis=0)
    rid_grp = lax.broadcasted_iota(jnp.int32, (N_EXPERT_GROUPS, tm), 0).astype(F32)
    gsel = jnp.zeros((N_EXPERT_GROUPS, tm), F32)
    for _ in range(TOPK_GROUPS):
        _, gi = _first_argmax(gs, rid_grp, N_EXPERT_GROUPS)
        hit = rid_grp == gi
        gsel = jnp.where(hit, 1.0, gsel)
        gs = jnp.where(hit, neg, gs)
    masked = jnp.concatenate(
        [jnp.where(gsel[g:g + 1, :] > 0.0, tiles[g], neg) for g in range(N_EXPERT_GROUPS)], axis=0)
    rid = lax.broadcasted_iota(jnp.int32, (n_exp, tm), 0).astype(F32)
    sel = jnp.zeros((n_exp, tm), F32)
    ids, picked = [], []
    for _ in range(TOP_K):
        _, ei = _first_argmax(masked, rid, n_exp)
        hit = rid == ei
        ids.append(ei)
        picked.append(jnp.sum(jnp.where(hit, scores_t, 0.0), axis=0, keepdims=True))
        sel = jnp.where(hit, 1.0, sel)
        masked = jnp.where(hit, neg, masked)
    total = picked[0]
    for p in picked[1:]:
        total = total + p
    gates = [p / total * ROUTE_SCALE for p in picked]
    return sel, rid, ids, gates


def _proj_ln_router_kernel(alpha, h_ref, w_ref, x_ref, g_ref, b_ref, rwt_ref, rb_ref,
                           o_ref, eid_ref, gate_ref, rank_ref, cnt_ref, wb_ref, tri_ref):
    i = pl.program_id(0)
    tm = x_ref.shape[0]

    @pl.when(i == 0)
    def _():
        wb_ref[...] = w_ref[...].astype(BF16)
        r = lax.broadcasted_iota(jnp.int32, (tm, tm), 0)
        c = lax.broadcasted_iota(jnp.int32, (tm, tm), 1)
        tri_ref[...] = jnp.where(r < c, 1.0, 0.0).astype(BF16)
        cnt_ref[...] = jnp.zeros_like(cnt_ref)

    mix = jnp.dot(h_ref[...], wb_ref[...], preferred_element_type=F32)
    x1 = _layer_norm(alpha * x_ref[...] + mix, g_ref[...], b_ref[...])
    o_ref[...] = x1
    logits_t = lax.dot_general(rwt_ref[...], x1, NT_DIMS, preferred_element_type=F32, precision=HIGHEST)
    sel, rid, ids, gates = _route_tile(_sigmoid(logits_t), rb_ref[...])

    sel_b = sel.astype(BF16)
    before = jnp.dot(sel_b, tri_ref[...], preferred_element_type=F32) + cnt_ref[:, 0:1]
    pad_rows = eid_ref.shape[0] - TOP_K
    zrow = jnp.zeros((pad_rows, tm), F32)
    ranks = [jnp.sum(jnp.where(rid == ei, before, 0.0), axis=0, keepdims=True) for ei in ids]
    eid_ref[...] = jnp.concatenate(ids + [zrow], axis=0).astype(jnp.int32)
    gate_ref[...] = jnp.concatenate(gates + [zrow], axis=0)
    rank_ref[...] = jnp.concatenate(ranks + [zrow], axis=0).astype(jnp.int32)
    ones = jnp.ones((tm, cnt_ref.shape[1]), BF16)
    cnt_ref[...] = cnt_ref[...] + jnp.dot(sel_b, ones, preferred_element_type=F32)


def _proj_ln_router(h, w, wl, x, alpha, ln_g, ln_b, router_w_t, router_bias_col):
    T, K = h.shape
    D = x.shape[1]
    E = router_w_t.shape[0]
    tm = TM_PROJ
    kp = SUBLANE
    row_spec = pl.BlockSpec((kp, tm), lambda i: (0, i))
    return pl.pallas_call(
        functools.partial(_proj_ln_router_kernel, alpha),
        grid=(T // tm,),
        in_specs=[
            pl.BlockSpec((tm, K), lambda i: (i, 0)),
            pl.BlockSpec((None, K, D), lambda i: (wl, 0, 0)),
            pl.BlockSpec((tm, D), lambda i: (i, 0)),
            pl.BlockSpec((1, D), lambda i: (0, 0)),
            pl.BlockSpec((1, D), lambda i: (0, 0)),
            pl.BlockSpec((E, D), lambda i: (0, 0)),
            pl.BlockSpec((E, 1), lambda i: (0, 0)),
        ],
        out_specs=[pl.BlockSpec((tm, D), lambda i: (i, 0)), row_spec, row_spec, row_spec,
                   pl.BlockSpec((E, LANE), lambda i: (0, 0))],
        out_shape=[jax.ShapeDtypeStruct((T, D), F32),
                   jax.ShapeDtypeStruct((kp, T), jnp.int32),
                   jax.ShapeDtypeStruct((kp, T), F32),
                   jax.ShapeDtypeStruct((kp, T), jnp.int32),
                   jax.ShapeDtypeStruct((E, LANE), F32)],
        scratch_shapes=[pltpu.VMEM((K, D), BF16), pltpu.VMEM((tm, tm), BF16)],
        compiler_params=_params("arbitrary"),
        name="proj_ln_router",
    )(h, w, x, ln_g.reshape(1, D), ln_b.reshape(1, D), router_w_t, router_bias_col)


def _start_row_gather(idx_ref, base, n, src_hbm, dst, sem):
    def body(i, carry):
        r = idx_ref[base + i]
        pltpu.make_async_copy(src_hbm.at[pl.ds(r, 1), :], dst.at[pl.ds(i, 1), :], sem).start()
        return carry
    lax.fori_loop(0, n, body, 0, unroll=8)


def _wait_row_gather(n, src_hbm, dst, sem):
    pltpu.make_async_copy(src_hbm.at[pl.ds(0, n), :], dst, sem).wait()


def _expert_kernel(be_ref, nu_ref, tok_ref, x_hbm, wg_ref, wu_ref, wd_ref, y_ref,
                   xbuf, wgb, wub, wdb, sem):
    b = pl.program_id(0)
    bm = xbuf.shape[0]

    @pl.when(b < nu_ref[0])
    def _():
        _start_row_gather(tok_ref, b * bm, bm, x_hbm, xbuf, sem)
        new_expert = jnp.logical_or(b == 0, be_ref[b] != be_ref[jnp.maximum(b - 1, 0)])

        @pl.when(new_expert)
        def _():
            wgb[...] = wg_ref[...].astype(BF16)
            wub[...] = wu_ref[...].astype(BF16)
            wdb[...] = wd_ref[...].astype(BF16)

        _wait_row_gather(bm, x_hbm, xbuf, sem)
        xb = xbuf[...].astype(BF16)
        hg = jnp.dot(xb, wgb[...], preferred_element_type=F32)
        hu = jnp.dot(xb, wub[...], preferred_element_type=F32)
        hh = (_silu(hg) * hu).astype(BF16)
        y_ref[...] = jnp.dot(hh, wdb[...], preferred_element_type=F32)

    @pl.when(b >= nu_ref[0])
    def _():
        y_ref[...] = jnp.zeros_like(y_ref)


def _experts(x, slot_tok, block_e, n_used, wg, wu, wd, layer):
    T, D = x.shape
    DE = wg.shape[3]
    bm = BM_EXPERT
    nb = slot_tok.shape[0] // bm
    grid_spec = pltpu.PrefetchScalarGridSpec(
        num_scalar_prefetch=3,
        grid=(nb,),
        in_specs=[
            pl.BlockSpec(memory_space=pl.ANY),
            pl.BlockSpec((None, None, D, DE), lambda b, be, nu, tok: (layer, be[b], 0, 0)),
            pl.BlockSpec((None, None, D, DE), lambda b, be, nu, tok: (layer, be[b], 0, 0)),
            pl.BlockSpec((None, None, DE, D), lambda b, be, nu, tok: (layer, be[b], 0, 0)),
        ],
        out_specs=pl.BlockSpec((bm, D), lambda b, be, nu, tok: (b, 0)),
        scratch_shapes=[
            pltpu.VMEM((bm, D), F32),
            pltpu.VMEM((D, DE), BF16),
            pltpu.VMEM((D, DE), BF16),
            pltpu.VMEM((DE, D), BF16),
            pltpu.SemaphoreType.DMA,
        ],
    )
    return pl.pallas_call(
        _expert_kernel,
        grid_spec=grid_spec,
        out_shape=jax.ShapeDtypeStruct((nb * bm, D), F32),
        compiler_params=_params("arbitrary"),
        name="experts",
    )(block_e, n_used, slot_tok, x, wg, wu, wd)


def _combine_kernel(alpha, pos_ref, gates_ref, x_ref, y_hbm, sg_ref, su_ref, sd_ref, g_ref, b_ref,
                    o_ref, ybuf, sgb, sub, sdb, sem):
    i = pl.program_id(0)
    tm = x_ref.shape[0]
    n_rows = ybuf.shape[0]
    _start_row_gather(pos_ref, i * n_rows, n_rows, y_hbm, ybuf, sem)

    @pl.when(i == 0)
    def _():
        sgb[...] = sg_ref[...].astype(BF16)
        sub[...] = su_ref[...].astype(BF16)
        sdb[...] = sd_ref[...].astype(BF16)

    x = x_ref[...]
    xb = x.astype(BF16)
    hg = jnp.dot(xb, sgb[...], preferred_element_type=F32)
    hu = jnp.dot(xb, sub[...], preferred_element_type=F32)
    acc = jnp.dot((_silu(hg) * hu).astype(BF16), sdb[...], preferred_element_type=F32)
    _wait_row_gather(n_rows, y_hbm, ybuf, sem)
    gates = gates_ref[...]
    for k in range(TOP_K):
        acc = acc + gates[:, k:k + 1] * ybuf[k * tm:(k + 1) * tm, :]
    o_ref[...] = _layer_norm(alpha * x + acc, g_ref[...], b_ref[...])


def _combine(x, y_sorted, pos_tiles, gates, sg, su, sd, alpha, ln_g, ln_b, layer):
    T, D = x.shape
    DS = sg.shape[2]
    tm = TM_COMBINE
    grid_spec = pltpu.PrefetchScalarGridSpec(
        num_scalar_prefetch=1,
        grid=(T // tm,),
        in_specs=[
            pl.BlockSpec((tm, TOP_K), lambda i, pos: (i, 0)),
            pl.BlockSpec((tm, D), lambda i, pos: (i, 0)),
            pl.BlockSpec(memory_space=pl.ANY),
            pl.BlockSpec((None, D, DS), lambda i, pos: (layer, 0, 0)),
            pl.BlockSpec((None, D, DS), lambda i, pos: (layer, 0, 0)),
            pl.BlockSpec((None, DS, D), lambda i, pos: (layer, 0, 0)),
            pl.BlockSpec((None, 1, D), lambda i, pos: (layer, 0, 0)),
            pl.BlockSpec((None, 1, D), lambda i, pos: (layer, 0, 0)),
        ],
        out_specs=pl.BlockSpec((tm, D), lambda i, pos: (i, 0)),
        scratch_shapes=[
            pltpu.VMEM((TOP_K * tm, D), F32),
            pltpu.VMEM((D, DS), BF16),
            pltpu.VMEM((D, DS), BF16),
            pltpu.VMEM((DS, D), BF16),
            pltpu.SemaphoreType.DMA,
        ],
    )
    return pl.pallas_call(
        functools.partial(_combine_kernel, alpha),
        grid_spec=grid_spec,
        out_shape=jax.ShapeDtypeStruct((T, D), F32),
        compiler_params=_params("arbitrary"),
        name="combine",
    )(pos_tiles, gates, x, y_sorted, sg, su, sd, ln_g.reshape(-1, 1, D), ln_b.reshape(-1, 1, D))


def _dispatch(eid_t, rank_t, counts):
    T = eid_t.shape[1]
    E = counts.shape[0]
    A = T * TOP_K
    bm = BM_EXPERT
    nb = -(-A // bm) + E
    cnt = counts[:, 0].astype(jnp.int32)
    padded = (cnt + bm - 1) // bm * bm
    pends = jnp.cumsum(padded)
    pstart = pends - padded
    eid = eid_t[:TOP_K]
    onehot = eid[:, :, None] == jnp.arange(E, dtype=jnp.int32)[None, None, :]
    dest = rank_t[:TOP_K] + jnp.sum(jnp.where(onehot, pstart[None, None, :], 0), axis=-1)
    tok = jnp.broadcast_to(jnp.arange(T, dtype=jnp.int32)[None, :], (TOP_K, T))
    slot_tok = jnp.zeros((nb * bm,), jnp.int32).at[dest.reshape(-1)].set(tok.reshape(-1))
    block_e = jnp.minimum(jnp.searchsorted(pends, jnp.arange(nb, dtype=jnp.int32) * bm, side='right'),
                          E - 1).astype(jnp.int32)
    n_used = (pends[-1:] // bm).astype(jnp.int32)
    tm = TM_COMBINE
    pos_tiles = dest.reshape(TOP_K, T // tm, tm).transpose(1, 0, 2).reshape(-1)
    return slot_tok, block_e, n_used, pos_tiles


def _moe(x1, eid_t, gate_t, rank_t, counts, alpha, layer, wg, wu, wd, sg, su, sd, ln_g, ln_b):
    slot_tok, block_e, n_used, pos_tiles = _dispatch(eid_t, rank_t, counts)
    y_sorted = _experts(x1, slot_tok, block_e, n_used, wg, wu, wd, layer)
    gates = gate_t[:TOP_K].T
    return _combine(x1, y_sorted, pos_tiles, gates, sg, su, sd, alpha, ln_g, ln_b, layer)


def kernel(x_prompt, x_sample, state_lru_conv, state_lru_h, state_ssd_conv, state_ssd, lru_w_in, lru_conv_w, lru_conv_b, lru_w_rgate, lru_b_rgate, lru_w_igate, lru_b_igate, lru_lambda, lru_w_out, ssd_w_in, ssd_conv_w, ssd_conv_b, ssd_dt_bias, ssd_a_log, ssd_d, ssd_norm_w, ssd_w_out, ln_mix_g, ln_mix_b, ln_ffn_g, ln_ffn_b, router_w, router_bias, moe_w_gate, moe_w_up, moe_w_down, shared_w_gate, shared_w_up, shared_w_down):
    bp, lp, d_model = x_prompt.shape
    bs, ls, _ = x_sample.shape
    depth = ln_mix_g.shape[0]
    alpha = (2.0 * depth) ** 0.25
    tp, ts = bp * lp, bs * ls
    n_heads, p_dim, n_state = state_ssd.shape[2:]
    d_inner = n_heads * p_dim
    conv_dim = ssd_conv_w.shape[2]
    d_rnn = lru_conv_w.shape[2]

    x = jnp.concatenate([x_prompt.reshape(tp, d_model), x_sample.reshape(ts, d_model)], axis=0)
    tl_p = min(TL_LRU, lp)
    tl_s = min(TL_LRU, ls)
    q_p = min(SSD_CHUNK, lp)
    q_s = min(SSD_CHUNK, ls)

    p_lru_conv, p_lru_h, p_ssd_conv, p_ssd = [], [], [], []
    s_lru_conv, s_lru_h, s_ssd_conv, s_ssd = [], [], [], []
    for i in range(depth):
        j = i // 2
        if i % 2 == 0:
            gx = _in_proj(x, lru_w_in, j, 2 * d_rnn, 512)
            wrg = _block_diag_gate(lru_w_rgate[j])
            wig = _block_diag_gate(lru_w_igate[j])
            common = (lru_conv_w[j], lru_conv_b[j], wrg, lru_b_rgate[j], wig, lru_b_igate[j], lru_lambda[j])
            hp, cp, lp_h = _lru_seq(gx, 0, bp, lp, tl_p, jnp.zeros((bp, CONV_WIDTH - 1, d_rnn), F32),
                                    jnp.zeros((bp, d_rnn), F32), *common)
            hs, cs_, ls_h = _lru_seq(gx, tp, bs, ls, tl_s, state_lru_conv[j], state_lru_h[j], *common)
            p_lru_conv.append(cp)
            p_lru_h.append(lp_h.reshape(bp, d_rnn))
            s_lru_conv.append(cs_)
            s_lru_h.append(ls_h.reshape(bs, d_rnn))
            mixed = jnp.concatenate([hp, hs], axis=0)
            w_out, wl = lru_w_out, j
        else:
            zx = _in_proj(x, ssd_w_in, j, d_inner + conv_dim, 512)
            dt_raw = _small_proj(x, ssd_w_in[j][:, d_inner + conv_dim:])
            common = (ssd_conv_w[j], ssd_conv_b[j], ssd_dt_bias[j], ssd_a_log[j], ssd_d[j], ssd_norm_w[j])
            hp, cp, sp = _ssd_seq(zx, dt_raw, 0, bp, lp, q_p, jnp.zeros((bp, CONV_WIDTH - 1, conv_dim), F32),
                                  jnp.zeros((bp, n_heads, p_dim, n_state), F32), *common)
            hs, cs_, ss = _ssd_seq(zx, dt_raw, tp, bs, ls, q_s, state_ssd_conv[j], state_ssd[j], *common)
            p_ssd_conv.append(cp)
            p_ssd.append(sp)
            s_ssd_conv.append(cs_)
            s_ssd.append(ss)
            mixed = jnp.concatenate([hp, hs], axis=0)
            w_out, wl = ssd_w_out, j
        x1, eid_t, gate_t, rank_t, counts = _proj_ln_router(
            mixed, w_out, wl, x, alpha, ln_mix_g[i], ln_mix_b[i], router_w[i].T, router_bias[i].reshape(-1, 1))
        x = _moe(x1, eid_t, gate_t, rank_t, counts, alpha, i, moe_w_gate, moe_w_up, moe_w_down,
                 shared_w_gate, shared_w_up, shared_w_down, ln_ffn_g, ln_ffn_b)

    y_prompt = x[:tp].reshape(bp, lp, d_model)
    y_sample = x[tp:].reshape(bs, ls, d_model)
    return (y_prompt, y_sample,
            jnp.stack(p_lru_conv), jnp.stack(p_lru_h), jnp.stack(p_ssd_conv), jnp.stack(p_ssd),
            jnp.stack(s_lru_conv), jnp.stack(s_lru_h), jnp.stack(s_ssd_conv), jnp.stack(s_ssd))
```

```python
import functools
import math

import jax
import jax.numpy as jnp
from jax import lax
from jax.experimental import pallas as pl
from jax.experimental.pallas import tpu as pltpu

F32 = jnp.float32
BF16 = jnp.bfloat16

LN_EPS = 1e-5
RMS_EPS = 1e-5
CONV_WIDTH = 4
LRU_C = 8.0
SSD_CHUNK = 64
SSD_GROUPS = 8
TOP_K = 6
N_EXPERT_GROUPS = 8
TOPK_GROUPS = 4
ROUTE_SCALE = 1.0

LANE = 128
SUBLANE = 8
MXU_DIM = 256
VMEM_LIMIT = 56 * 1024 * 1024

TM_PROJ = 512
TM_COMBINE = 128
BM_EXPERT = 256
TL_LRU = 256
TN_LRU_IN = 1024
TN_SSD_IN = 1536

HIGHEST = lax.Precision.HIGHEST
NT_DIMS = (((1,), (1,)), ((), ()))
TN_DIMS = (((0,), (0,)), ((), ()))


def _sigmoid(x):
    return 1.0 / (1.0 + jnp.exp(-x))


def _silu(x):
    return x * _sigmoid(x)


def _softplus(x):
    return jnp.maximum(x, 0.0) + jnp.log1p(jnp.exp(-jnp.abs(x)))


def _gelu_tanh(x):
    c = math.sqrt(2.0 / math.pi)
    return 0.5 * x * (1.0 + jnp.tanh(c * (x + 0.044715 * (x * x * x))))


def _layer_norm(v, g, b):
    mu = jnp.mean(v, axis=-1, keepdims=True)
    d = v - mu
    var = jnp.mean(d * d, axis=-1, keepdims=True)
    return d * lax.rsqrt(var + LN_EPS) * g + b


def _params(*sem):
    return pltpu.CompilerParams(dimension_semantics=sem, vmem_limit_bytes=VMEM_LIMIT)


def _mm_kernel(x_ref, w_ref, o_ref):
    o_ref[...] = jnp.dot(x_ref[...].astype(BF16), w_ref[...].astype(BF16),
                         preferred_element_type=F32)


def _in_proj_kernel(x_ref, w_ref, o_ref, wb_ref):
    @pl.when(pl.program_id(1) == 0)
    def _():
        wb_ref[...] = w_ref[...].astype(BF16)

    o_ref[...] = jnp.dot(x_ref[...].astype(BF16), wb_ref[...], preferred_element_type=F32)


def _in_proj(x, w, layer, n_cols, tn):
    T, K = x.shape
    tm = TM_PROJ
    return pl.pallas_call(
        _in_proj_kernel,
        grid=(n_cols // tn, T // tm),
        in_specs=[pl.BlockSpec((tm, K), lambda j, i: (i, 0)),
                  pl.BlockSpec((None, K, tn), lambda j, i: (layer, 0, j))],
        out_specs=pl.BlockSpec((tm, tn), lambda j, i: (i, j)),
        out_shape=jax.ShapeDtypeStruct((T, n_cols), F32),
        scratch_shapes=[pltpu.VMEM((K, tn), BF16)],
        compiler_params=_params("arbitrary", "arbitrary"),
        name="in_proj",
    )(x, w)


def _small_proj(x, w):
    T, K = x.shape
    N = w.shape[1]
    tm = TM_PROJ
    return pl.pallas_call(
        _mm_kernel,
        grid=(T // tm,),
        in_specs=[pl.BlockSpec((tm, K), lambda i: (i, 0)),
                  pl.BlockSpec((K, N), lambda i: (0, 0))],
        out_specs=pl.BlockSpec((tm, N), lambda i: (i, 0)),
        out_shape=jax.ShapeDtypeStruct((T, N), F32),
        compiler_params=_params("parallel"),
        name="dt_proj",
    )(x, w)


CONV_PAD = SUBLANE
CONV_HIST = CONV_PAD - (CONV_WIDTH - 1)


def _conv_tile(xx_ref, n, cw, cb):
    acc = cb + cw[CONV_WIDTH - 1:CONV_WIDTH] * xx_ref[CONV_PAD:CONV_PAD + n, :]
    for k in range(CONV_WIDTH - 1):
        off = CONV_HIST + k
        acc = acc + cw[k:k + 1] * xx_ref[off:off + n, :]
    return acc


def _lru_seq_kernel(gate_ref, xr_ref, cbuf_ref, h0_ref, cw_ref, cb_ref, wrg_ref, brg_ref,
                    wig_ref, big_ref, lam_ref, out_ref, nconv_ref, hlast_ref,
                    xx_ref, h_ref, a_ref, u_ref):
    c = pl.program_id(1)
    n = xr_ref.shape[0]
    d = xr_ref.shape[1]

    @pl.when(c == 0)
    def _():
        xx_ref[CONV_HIST:CONV_PAD, :] = cbuf_ref[0]
        h_ref[...] = h0_ref[0]

    xx_ref[CONV_PAD:CONV_PAD + n, :] = xr_ref[...]
    xr = _conv_tile(xx_ref, n, cw_ref[...], cb_ref[...])
    tail = xx_ref[n + CONV_HIST:n + CONV_PAD, :]
    xx_ref[CONV_HIST:CONV_PAD, :] = tail

    xb = xr.astype(BF16)
    nblk = d // MXU_DIM
    r_parts, i_parts = [], []
    for g in range(nblk):
        seg = xb[:, g * MXU_DIM:(g + 1) * MXU_DIM]
        r_parts.append(jnp.dot(seg, wrg_ref[g], preferred_element_type=F32))
        i_parts.append(jnp.dot(seg, wig_ref[g], preferred_element_type=F32))
    r = _sigmoid(jnp.concatenate(r_parts, axis=-1) + brg_ref[...])
    ig = _sigmoid(jnp.concatenate(i_parts, axis=-1) + big_ref[...])
    log_sig_lam = -_softplus(-lam_ref[...])
    log_a = LRU_C * r * log_sig_lam
    a = jnp.exp(log_a)
    a_ref[...] = a
    u_ref[...] = jnp.sqrt(-jnp.tanh(log_a) * (a * a + 1.0)) * (ig * xr)

    def step(t, h):
        h = a_ref[pl.ds(t, 1), :] * h + u_ref[pl.ds(t, 1), :]
        u_ref[pl.ds(t, 1), :] = h
        return h

    h = lax.fori_loop(0, n, step, h_ref[...], unroll=8)
    h_ref[...] = h
    out_ref[...] = (u_ref[...] * _gelu_tanh(gate_ref[...])).astype(BF16)

    @pl.when(c == pl.num_programs(1) - 1)
    def _():
        nconv_ref[0] = tail
        hlast_ref[0] = h


def _lru_seq(gx, row0, bsz, seqlen, tl, conv_buf, h0, cw, cb, wrg, brg, wig, big, lam):
    d = cw.shape[1]
    nt = seqlen // tl
    blk0 = row0 // tl
    row_map = lambda b, c: (blk0 + b * nt + c, 0)
    vec = lambda a: a.reshape(1, d)
    full2 = lambda b, c: (0, 0)
    full3 = lambda b, c: (0, 0, 0)
    return pl.pallas_call(
        _lru_seq_kernel,
        grid=(bsz, nt),
        in_specs=[
            pl.BlockSpec((tl, d), row_map),
            pl.BlockSpec((tl, d), lambda b, c: (blk0 + b * nt + c, 1)),
            pl.BlockSpec((1, CONV_WIDTH - 1, d), lambda b, c: (b, 0, 0)),
            pl.BlockSpec((1, 1, d), lambda b, c: (b, 0, 0)),
            pl.BlockSpec((CONV_WIDTH, d), full2),
            pl.BlockSpec((1, d), full2),
            pl.BlockSpec(wrg.shape, full3),
            pl.BlockSpec((1, d), full2),
            pl.BlockSpec(wig.shape, full3),
            pl.BlockSpec((1, d), full2),
            pl.BlockSpec((1, d), full2),
        ],
        out_specs=[
            pl.BlockSpec((tl, d), lambda b, c: (b * nt + c, 0)),
            pl.BlockSpec((1, CONV_WIDTH - 1, d), lambda b, c: (b, 0, 0)),
            pl.BlockSpec((1, 1, d), lambda b, c: (b, 0, 0)),
        ],
        out_shape=[
            jax.ShapeDtypeStruct((bsz * seqlen, d), BF16),
            jax.ShapeDtypeStruct((bsz, CONV_WIDTH - 1, d), F32),
            jax.ShapeDtypeStruct((bsz, 1, d), F32),
        ],
        scratch_shapes=[
            pltpu.VMEM((tl + CONV_PAD, d), F32),
            pltpu.VMEM((1, d), F32),
            pltpu.VMEM((tl, d), F32),
            pltpu.VMEM((tl, d), F32),
        ],
        compiler_params=_params("arbitrary", "arbitrary"),
        name="lru_seq",
    )(gx, gx, conv_buf, h0.reshape(bsz, 1, d), cw, vec(cb), wrg, vec(brg), wig, vec(big), vec(lam))


def _block_diag_gate(w):
    nb, bw, _ = w.shape
    per = MXU_DIM // bw
    w4 = w.reshape(nb // per, per, bw, bw)
    eye = jnp.eye(per, dtype=w.dtype)
    out = w4[:, :, :, None, :] * eye[None, :, None, :, None]
    return out.reshape(nb // per, MXU_DIM, MXU_DIM).astype(BF16)


def _ssd_seq_kernel(z_ref, xs_ref, bc_ref, dt_ref, cbuf_ref, h0_ref, cw_ref, cb_ref, dtb_ref,
                    alog_ref, dsk_ref, nw_ref, out_ref, nconv_ref, st_ref, xx_ref, y_ref):
    c = pl.program_id(1)
    q = xs_ref.shape[0]
    d_inner = xs_ref.shape[1]
    n_heads = st_ref.shape[1]
    p_dim = st_ref.shape[2]
    n_state = st_ref.shape[3]
    hpg = n_heads // SSD_GROUPS

    @pl.when(c == 0)
    def _():
        xx_ref[CONV_HIST:CONV_PAD, :] = cbuf_ref[0]
        st_ref[...] = h0_ref[...]

    xx_ref[CONV_PAD:CONV_PAD + q, 0:d_inner] = xs_ref[...]
    xx_ref[CONV_PAD:CONV_PAD + q, d_inner:] = bc_ref[...]
    act = _silu(_conv_tile(xx_ref, q, cw_ref[...], cb_ref[...]))
    tail = xx_ref[q + CONV_HIST:q + CONV_PAD, :]
    xx_ref[CONV_HIST:CONV_PAD, :] = tail

    @pl.when(c == pl.num_programs(1) - 1)
    def _():
        nconv_ref[0] = tail

    dt = _softplus(dt_ref[...] + dtb_ref[...])
    adt = dt * (-jnp.exp(alog_ref[...]))
    row = lax.broadcasted_iota(jnp.int32, (q, q), 0)
    col = lax.broadcasted_iota(jnp.int32, (q, q), 1)
    causal = col <= row
    tril = causal.astype(F32)
    cs = jnp.dot(tril, adt, preferred_element_type=F32, precision=HIGHEST)
    eye_h = (lax.broadcasted_iota(jnp.int32, (n_heads, n_heads), 0)
             == lax.broadcasted_iota(jnp.int32, (n_heads, n_heads), 1)).astype(F32)
    cs_t = lax.dot_general(eye_h, cs, NT_DIMS, preferred_element_type=F32, precision=HIGHEST)
    dsk = dsk_ref[...]
    b_off = d_inner
    c_off = d_inner + SSD_GROUPS * n_state

    for g in range(SSD_GROUPS):
        bg = act[:, b_off + g * n_state:b_off + (g + 1) * n_state].astype(BF16)
        cg = act[:, c_off + g * n_state:c_off + (g + 1) * n_state].astype(BF16)
        cb_mat = lax.dot_general(cg, bg, NT_DIMS, preferred_element_type=F32)
        for e in range(hpg):
            h = g * hpg + e
            cs_h = cs[:, h:h + 1]
            cs_last = cs[q - 1:q, h:h + 1]
            lmat = jnp.exp(jnp.where(causal, cs_h - cs_t[h:h + 1, :], -jnp.inf))
            xs_h = act[:, h * p_dim:(h + 1) * p_dim]
            xd = xs_h * dt[:, h:h + 1]
            y = jnp.dot((cb_mat * lmat).astype(BF16), xd.astype(BF16), preferred_element_type=F32)
            prev = st_ref[0, h]
            y_off = lax.dot_general(cg, prev.astype(BF16), NT_DIMS, preferred_element_type=F32)
            y = y + y_off * jnp.exp(cs_h)
            xdec = (xd * jnp.exp(cs_last - cs_h)).astype(BF16)
            st_new = lax.dot_general(xdec, bg, TN_DIMS, preferred_element_type=F32)
            st_ref[0, h] = jnp.exp(cs_last) * prev + st_new
            y_ref[:, h * p_dim:(h + 1) * p_dim] = y + dsk[:, h:h + 1] * xs_h

    gated = y_ref[...] * _silu(z_ref[...])
    gw = d_inner // SSD_GROUPS
    nw = nw_ref[...]
    for g in range(SSD_GROUPS):
        seg = gated[:, g * gw:(g + 1) * gw]
        ms = jnp.mean(seg * seg, axis=-1, keepdims=True)
        out_ref[:, g * gw:(g + 1) * gw] = (seg * lax.rsqrt(ms + RMS_EPS) * nw[:, g * gw:(g + 1) * gw]).astype(BF16)


def _ssd_seq(zx, dt_raw, row0, bsz, seqlen, q, conv_buf, h0, cw, cb, dtb, alog, dsk, nw):
    n_heads, p_dim, n_state = h0.shape[1:]
    d_inner = n_heads * p_dim
    conv_dim = cw.shape[1]
    nt = seqlen // q
    blk0 = row0 // q
    full2 = lambda b, c: (0, 0)
    col = lambda j: (lambda b, c: (blk0 + b * nt + c, j))
    return pl.pallas_call(
        _ssd_seq_kernel,
        grid=(bsz, nt),
        in_specs=[
            pl.BlockSpec((q, d_inner), col(0)),
            pl.BlockSpec((q, d_inner), col(1)),
            pl.BlockSpec((q, d_inner), col(2)),
            pl.BlockSpec((q, n_heads), col(0)),
            pl.BlockSpec((1, CONV_WIDTH - 1, conv_dim), lambda b, c: (b, 0, 0)),
            pl.BlockSpec((1, n_heads, p_dim, n_state), lambda b, c: (b, 0, 0, 0)),
            pl.BlockSpec((CONV_WIDTH, conv_dim), full2),
            pl.BlockSpec((1, conv_dim), full2),
            pl.BlockSpec((1, n_heads), full2),
            pl.BlockSpec((1, n_heads), full2),
            pl.BlockSpec((1, n_heads), full2),
            pl.BlockSpec((1, d_inner), full2),
        ],
        out_specs=[
            pl.BlockSpec((q, d_inner), lambda b, c: (b * nt + c, 0)),
            pl.BlockSpec((1, CONV_WIDTH - 1, conv_dim), lambda b, c: (b, 0, 0)),
            pl.BlockSpec((1, n_heads, p_dim, n_state), lambda b, c: (b, 0, 0, 0)),
        ],
        out_shape=[
            jax.ShapeDtypeStruct((bsz * seqlen, d_inner), BF16),
            jax.ShapeDtypeStruct((bsz, CONV_WIDTH - 1, conv_dim), F32),
            jax.ShapeDtypeStruct((bsz, n_heads, p_dim, n_state), F32),
        ],
        scratch_shapes=[
            pltpu.VMEM((q + CONV_PAD, conv_dim), F32),
            pltpu.VMEM((q, d_inner), F32),
        ],
        compiler_params=_params("arbitrary", "arbitrary"),
        name="ssd_seq",
    )(zx, zx, zx, dt_raw, conv_buf, h0, cw, cb.reshape(1, -1), dtb.reshape(1, -1),
      alog.reshape(1, -1), dsk.reshape(1, -1), nw.reshape(1, -1))


def _first_argmax(v, rid, n):
    m = jnp.max(v, axis=0, keepdims=True)
    idx = jnp.min(jnp.where(v == m, rid, float(n)), axis=0, keepdims=True)
    return m, idx


def _route_tile(scores_t, bias_col):
    n_exp, tm = scores_t.shape
    per = n_exp // N_EXPERT_GROUPS
    neg = -jnp.inf
    biased = scores_t + bias_col
    rid_g = lax.broadcasted_iota(jnp.int32, (per, tm), 0).astype(F32)
    tiles, gs_rows = [], []
    for g in range(N_EXPERT_GROUPS):
        v = biased[g * per:(g + 1) * per, :]
        tiles.append(v)
        m1, i1 = _first_argmax(v, rid_g, per)
        m2 = jnp.max(jnp.where(rid_g == i1, neg, v), axis=0, keepdims=True)
        gs_rows.append(m1 + m2)
    gs = jnp.concatenate(gs_rows, axis=0)
    rid_grp = lax.broadcasted_iota(jnp.int32, (N_EXPERT_GROUPS, tm), 0).astype(F32)
    gsel = jnp.zeros((N_EXPERT_GROUPS, tm), F32)
    for _ in range(TOPK_GROUPS):
        _, gi = _first_argmax(gs, rid_grp, N_EXPERT_GROUPS)
        hit = rid_grp == gi
        gsel = jnp.where(hit, 1.0, gsel)
        gs = jnp.where(hit, neg, gs)
    masked = jnp.concatenate(
        [jnp.where(gsel[g:g + 1, :] > 0.0, tiles[g], neg) for g in range(N_EXPERT_GROUPS)], axis=0)
    rid = lax.broadcasted_iota(jnp.int32, (n_exp, tm), 0).astype(F32)
    sel = jnp.zeros((n_exp, tm), F32)
    ids, picked = [], []
    for _ in range(TOP_K):
        _, ei = _first_argmax(masked, rid, n_exp)
        hit = rid == ei
        ids.append(ei)
        picked.append(jnp.sum(jnp.where(hit, scores_t, 0.0), axis=0, keepdims=True))
        sel = jnp.where(hit, 1.0, sel)
        masked = jnp.where(hit, neg, masked)
    total = picked[0]
    for p in picked[1:]:
        total = total + p
    gates = [p / total * ROUTE_SCALE for p in picked]
    return sel, rid, ids, gates


def _proj_ln_router_kernel(alpha, h_ref, w_ref, x_ref, g_ref, b_ref, rwt_ref, rb_ref,
                           o_ref, eid_ref, gate_ref, rank_ref, cnt_ref, wb_ref, tri_ref):
    i = pl.program_id(0)
    tm = x_ref.shape[0]

    @pl.when(i == 0)
    def _():
        wb_ref[...] = w_ref[...].astype(BF16)
        r = lax.broadcasted_iota(jnp.int32, (tm, tm), 0)
        c = lax.broadcasted_iota(jnp.int32, (tm, tm), 1)
        tri_ref[...] = jnp.where(r < c, 1.0, 0.0).astype(BF16)
        cnt_ref[...] = jnp.zeros_like(cnt_ref)

    mix = jnp.dot(h_ref[...], wb_ref[...], preferred_element_type=F32)
    x1 = _layer_norm(alpha * x_ref[...] + mix, g_ref[...], b_ref[...])
    o_ref[...] = x1
    logits_t = lax.dot_general(rwt_ref[...], x1, NT_DIMS, preferred_element_type=F32, precision=HIGHEST)
    sel, rid, ids, gates = _route_tile(_sigmoid(logits_t), rb_ref[...])

    sel_b = sel.astype(BF16)
    before = jnp.dot(sel_b, tri_ref[...], preferred_element_type=F32) + cnt_ref[:, 0:1]
    pad_rows = eid_ref.shape[0] - TOP_K
    zrow = jnp.zeros((pad_rows, tm), F32)
    ranks = [jnp.sum(jnp.where(rid == ei, before, 0.0), axis=0, keepdims=True) for ei in ids]
    eid_ref[...] = jnp.concatenate(ids + [zrow], axis=0).astype(jnp.int32)
    gate_ref[...] = jnp.concatenate(gates + [zrow], axis=0)
    rank_ref[...] = jnp.concatenate(ranks + [zrow], axis=0).astype(jnp.int32)
    ones = jnp.ones((tm, cnt_ref.shape[1]), BF16)
    cnt_ref[...] = cnt_ref[...] + jnp.dot(sel_b, ones, preferred_element_type=F32)


def _proj_ln_router(h, w, wl, x, alpha, ln_g, ln_b, router_w_t, router_bias_col):
    T, K = h.shape
    D = x.shape[1]
    E = router_w_t.shape[0]
    tm = TM_PROJ
    kp = SUBLANE
    row_spec = pl.BlockSpec((kp, tm), lambda i: (0, i))
    return pl.pallas_call(
        functools.partial(_proj_ln_router_kernel, alpha),
        grid=(T // tm,),
        in_specs=[
            pl.BlockSpec((tm, K), lambda i: (i, 0)),
            pl.BlockSpec((None, K, D), lambda i: (wl, 0, 0)),
            pl.BlockSpec((tm, D), lambda i: (i, 0)),
            pl.BlockSpec((1, D), lambda i: (0, 0)),
            pl.BlockSpec((1, D), lambda i: (0, 0)),
            pl.BlockSpec((E, D), lambda i: (0, 0)),
            pl.BlockSpec((E, 1), lambda i: (0, 0)),
        ],
        out_specs=[pl.BlockSpec((tm, D), lambda i: (i, 0)), row_spec, row_spec, row_spec,
                   pl.BlockSpec((E, LANE), lambda i: (0, 0))],
        out_shape=[jax.ShapeDtypeStruct((T, D), F32),
                   jax.ShapeDtypeStruct((kp, T), jnp.int32),
                   jax.ShapeDtypeStruct((kp, T), F32),
                   jax.ShapeDtypeStruct((kp, T), jnp.int32),
                   jax.ShapeDtypeStruct((E, LANE), F32)],
        scratch_shapes=[pltpu.VMEM((K, D), BF16), pltpu.VMEM((tm, tm), BF16)],
        compiler_params=_params("arbitrary"),
        name="proj_ln_router",
    )(h, w, x, ln_g.reshape(1, D), ln_b.reshape(1, D), router_w_t, router_bias_col)


def _start_row_gather(idx_ref, base, n, src_hbm, dst, sem):
    def body(j, carry):
        i0 = pl.multiple_of(j * SUBLANE, SUBLANE)
        for u in range(SUBLANE):
            r = idx_ref[base + i0 + u]
            pltpu.make_async_copy(src_hbm.at[pl.ds(r, 1), :], dst.at[pl.ds(i0 + u, 1), :], sem).start()
        return carry
    lax.fori_loop(0, n // SUBLANE, body, 0)


def _wait_row_gather(n, src_hbm, dst, sem):
    pltpu.make_async_copy(src_hbm.at[pl.ds(0, n), :], dst, sem).wait()


def _expert_kernel(be_ref, nu_ref, tok_ref, x_hbm, wg_ref, wu_ref, wd_ref, y_ref,
                   xbuf, wgb, wub, wdb, sem):
    b = pl.program_id(0)
    bm = xbuf.shape[1]
    n_used = nu_ref[0]
    slot = lax.rem(b, 2)

    @pl.when(jnp.logical_and(b == 0, n_used > 0))
    def _():
        _start_row_gather(tok_ref, 0, bm, x_hbm, xbuf.at[0], sem.at[0])

    @pl.when(b + 1 < n_used)
    def _():
        _start_row_gather(tok_ref, (b + 1) * bm, bm, x_hbm, xbuf.at[1 - slot], sem.at[1 - slot])

    @pl.when(b < n_used)
    def _():
        new_expert = jnp.logical_or(b == 0, be_ref[b] != be_ref[jnp.maximum(b - 1, 0)])

        @pl.when(new_expert)
        def _():
            wgb[...] = wg_ref[...].astype(BF16)
            wub[...] = wu_ref[...].astype(BF16)
            wdb[...] = wd_ref[...].astype(BF16)

        _wait_row_gather(bm, x_hbm, xbuf.at[slot], sem.at[slot])
        xb = xbuf[slot].astype(BF16)
        hg = jnp.dot(xb, wgb[...], preferred_element_type=F32)
        hu = jnp.dot(xb, wub[...], preferred_element_type=F32)
        hh = (_silu(hg) * hu).astype(BF16)
        y_ref[...] = jnp.dot(hh, wdb[...], preferred_element_type=F32)

    @pl.when(b >= n_used)
    def _():
        y_ref[...] = jnp.zeros_like(y_ref)


def _experts(x, slot_tok, block_e, n_used, wg, wu, wd, layer):
    T, D = x.shape
    DE = wg.shape[3]
    bm = BM_EXPERT
    nb = slot_tok.shape[0] // bm
    grid_spec = pltpu.PrefetchScalarGridSpec(
        num_scalar_prefetch=3,
        grid=(nb,),
        in_specs=[
            pl.BlockSpec(memory_space=pl.ANY),
            pl.BlockSpec((None, None, D, DE), lambda b, be, nu, tok: (layer, be[b], 0, 0)),
            pl.BlockSpec((None, None, D, DE), lambda b, be, nu, tok: (layer, be[b], 0, 0)),
            pl.BlockSpec((None, None, DE, D), lambda b, be, nu, tok: (layer, be[b], 0, 0)),
        ],
        out_specs=pl.BlockSpec((bm, D), lambda b, be, nu, tok: (b, 0)),
        scratch_shapes=[
            pltpu.VMEM((2, bm, D), F32),
            pltpu.VMEM((D, DE), BF16),
            pltpu.VMEM((D, DE), BF16),
            pltpu.VMEM((DE, D), BF16),
            pltpu.SemaphoreType.DMA((2,)),
        ],
    )
    return pl.pallas_call(
        _expert_kernel,
        grid_spec=grid_spec,
        out_shape=jax.ShapeDtypeStruct((nb * bm, D), F32),
        compiler_params=_params("arbitrary"),
        name="experts",
    )(block_e, n_used, slot_tok, x, wg, wu, wd)


def _combine_kernel(alpha, pos_ref, gates_ref, x_ref, y_hbm, sg_ref, su_ref, sd_ref, g_ref, b_ref,
                    o_ref, ybuf, sgb, sub, sdb, sem):
    i = pl.program_id(0)
    tm = x_ref.shape[0]
    n_rows = ybuf.shape[1]
    slot = lax.rem(i, 2)

    @pl.when(i == 0)
    def _():
        _start_row_gather(pos_ref, 0, n_rows, y_hbm, ybuf.at[0], sem.at[0])
        sgb[...] = sg_ref[...].astype(BF16)
        sub[...] = su_ref[...].astype(BF16)
        sdb[...] = sd_ref[...].astype(BF16)

    @pl.when(i + 1 < pl.num_programs(0))
    def _():
        _start_row_gather(pos_ref, (i + 1) * n_rows, n_rows, y_hbm, ybuf.at[1 - slot], sem.at[1 - slot])

    x = x_ref[...]
    xb = x.astype(BF16)
    hg = jnp.dot(xb, sgb[...], preferred_element_type=F32)
    hu = jnp.dot(xb, sub[...], preferred_element_type=F32)
    acc = jnp.dot((_silu(hg) * hu).astype(BF16), sdb[...], preferred_element_type=F32)
    _wait_row_gather(n_rows, y_hbm, ybuf.at[slot], sem.at[slot])
    gates = gates_ref[...]
    for k in range(TOP_K):
        acc = acc + gates[:, k:k + 1] * ybuf[slot, k * tm:(k + 1) * tm, :]
    o_ref[...] = _layer_norm(alpha * x + acc, g_ref[...], b_ref[...])


def _combine(x, y_sorted, pos_tiles, gates, sg, su, sd, alpha, ln_g, ln_b, layer):
    T, D = x.shape
    DS = sg.shape[2]
    tm = TM_COMBINE
    grid_spec = pltpu.PrefetchScalarGridSpec(
        num_scalar_prefetch=1,
        grid=(T // tm,),
        in_specs=[
            pl.BlockSpec((tm, TOP_K), lambda i, pos: (i, 0)),
            pl.BlockSpec((tm, D), lambda i, pos: (i, 0)),
            pl.BlockSpec(memory_space=pl.ANY),
            pl.BlockSpec((None, D, DS), lambda i, pos: (layer, 0, 0)),
            pl.BlockSpec((None, D, DS), lambda i, pos: (layer, 0, 0)),
            pl.BlockSpec((None, DS, D), lambda i, pos: (layer, 0, 0)),
            pl.BlockSpec((None, 1, D), lambda i, pos: (layer, 0, 0)),
            pl.BlockSpec((None, 1, D), lambda i, pos: (layer, 0, 0)),
        ],
        out_specs=pl.BlockSpec((tm, D), lambda i, pos: (i, 0)),
        scratch_shapes=[
            pltpu.VMEM((2, TOP_K * tm, D), F32),
            pltpu.VMEM((D, DS), BF16),
            pltpu.VMEM((D, DS), BF16),
            pltpu.VMEM((DS, D), BF16),
            pltpu.SemaphoreType.DMA((2,)),
        ],
    )
    return pl.pallas_call(
        functools.partial(_combine_kernel, alpha),
        grid_spec=grid_spec,
        out_shape=jax.ShapeDtypeStruct((T, D), F32),
        compiler_params=_params("arbitrary"),
        name="combine",
    )(pos_tiles, gates, x, y_sorted, sg, su, sd, ln_g.reshape(-1, 1, D), ln_b.reshape(-1, 1, D))


def _dispatch(eid_t, rank_t, counts):
    T = eid_t.shape[1]
    E = counts.shape[0]
    A = T * TOP_K
    bm = BM_EXPERT
    nb = -(-A // bm) + E
    cnt = counts[:, 0].astype(jnp.int32)
    padded = (cnt + bm - 1) // bm * bm
    pends = jnp.cumsum(padded)
    pstart = pends - padded
    eid = eid_t[:TOP_K]
    onehot = eid[:, :, None] == jnp.arange(E, dtype=jnp.int32)[None, None, :]
    dest = rank_t[:TOP_K] + jnp.sum(jnp.where(onehot, pstart[None, None, :], 0), axis=-1)
    tok = jnp.broadcast_to(jnp.arange(T, dtype=jnp.int32)[None, :], (TOP_K, T))
    slot_tok = jnp.zeros((nb * bm,), jnp.int32).at[dest.reshape(-1)].set(tok.reshape(-1))
    block_start = jnp.arange(nb, dtype=jnp.int32) * bm
    block_e = jnp.minimum(jnp.sum((pends[None, :] <= block_start[:, None]).astype(jnp.int32), axis=1), E - 1)
    n_used = (pends[-1:] // bm).astype(jnp.int32)
    tm = TM_COMBINE
    pos_tiles = dest.reshape(TOP_K, T // tm, tm).transpose(1, 0, 2).reshape(-1)
    return slot_tok, block_e, n_used, pos_tiles


def _moe(x1, eid_t, gate_t, rank_t, counts, alpha, layer, wg, wu, wd, sg, su, sd, ln_g, ln_b):
    slot_tok, block_e, n_used, pos_tiles = _dispatch(eid_t, rank_t, counts)
    y_sorted = _experts(x1, slot_tok, block_e, n_used, wg, wu, wd, layer)
    gates = gate_t[:TOP_K].T
    return _combine(x1, y_sorted, pos_tiles, gates, sg, su, sd, alpha, ln_g, ln_b, layer)


def kernel(x_prompt, x_sample, state_lru_conv, state_lru_h, state_ssd_conv, state_ssd, lru_w_in, lru_conv_w, lru_conv_b, lru_w_rgate, lru_b_rgate, lru_w_igate, lru_b_igate, lru_lambda, lru_w_out, ssd_w_in, ssd_conv_w, ssd_conv_b, ssd_dt_bias, ssd_a_log, ssd_d, ssd_norm_w, ssd_w_out, ln_mix_g, ln_mix_b, ln_ffn_g, ln_ffn_b, router_w, router_bias, moe_w_gate, moe_w_up, moe_w_down, shared_w_gate, shared_w_up, shared_w_down):
    bp, lp, d_model = x_prompt.shape
    bs, ls, _ = x_sample.shape
    depth = ln_mix_g.shape[0]
    alpha = (2.0 * depth) ** 0.25
    tp, ts = bp * lp, bs * ls
    n_heads, p_dim, n_state = state_ssd.shape[2:]
    d_inner = n_heads * p_dim
    conv_dim = ssd_conv_w.shape[2]
    d_rnn = lru_conv_w.shape[2]

    x = jnp.concatenate([x_prompt.reshape(tp, d_model), x_sample.reshape(ts, d_model)], axis=0)
    tl_p = min(TL_LRU, lp)
    tl_s = min(TL_LRU, ls)
    q_p = min(SSD_CHUNK, lp)
    q_s = min(SSD_CHUNK, ls)

    p_lru_conv, p_lru_h, p_ssd_conv, p_ssd = [], [], [], []
    s_lru_conv, s_lru_h, s_ssd_conv, s_ssd = [], [], [], []
    for i in range(depth):
        j = i // 2
        if i % 2 == 0:
            gx = _in_proj(x, lru_w_in, j, 2 * d_rnn, TN_LRU_IN)
            wrg = _block_diag_gate(lru_w_rgate[j])
            wig = _block_diag_gate(lru_w_igate[j])
            common = (lru_conv_w[j], lru_conv_b[j], wrg, lru_b_rgate[j], wig, lru_b_igate[j], lru_lambda[j])
            hp, cp, lp_h = _lru_seq(gx, 0, bp, lp, tl_p, jnp.zeros((bp, CONV_WIDTH - 1, d_rnn), F32),
                                    jnp.zeros((bp, d_rnn), F32), *common)
            hs, cs_, ls_h = _lru_seq(gx, tp, bs, ls, tl_s, state_lru_conv[j], state_lru_h[j], *common)
            p_lru_conv.append(cp)
            p_lru_h.append(lp_h.reshape(bp, d_rnn))
            s_lru_conv.append(cs_)
            s_lru_h.append(ls_h.reshape(bs, d_rnn))
            mixed = jnp.concatenate([hp, hs], axis=0)
            w_out, wl = lru_w_out, j
        else:
            zx = _in_proj(x, ssd_w_in, j, d_inner + conv_dim, TN_SSD_IN)
            dt_raw = _small_proj(x, ssd_w_in[j][:, d_inner + conv_dim:])
            common = (ssd_conv_w[j], ssd_conv_b[j], ssd_dt_bias[j], ssd_a_log[j], ssd_d[j], ssd_norm_w[j])
            hp, cp, sp = _ssd_seq(zx, dt_raw, 0, bp, lp, q_p, jnp.zeros((bp, CONV_WIDTH - 1, conv_dim), F32),
                                  jnp.zeros((bp, n_heads, p_dim, n_state), F32), *common)
            hs, cs_, ss = _ssd_seq(zx, dt_raw, tp, bs, ls, q_s, state_ssd_conv[j], state_ssd[j], *common)
            p_ssd_conv.append(cp)
            p_ssd.append(sp)
            s_ssd_conv.append(cs_)
            s_ssd.append(ss)
            mixed = jnp.concatenate([hp, hs], axis=0)
            w_out, wl = ssd_w_out, j
        x1, eid_t, gate_t, rank_t, counts = _proj_ln_router(
            mixed, w_out, wl, x, alpha, ln_mix_g[i], ln_mix_b[i], router_w[i].T, router_bias[i].reshape(-1, 1))
        x = _moe(x1, eid_t, gate_t, rank_t, counts, alpha, i, moe_w_gate, moe_w_up, moe_w_down,
                 shared_w_gate, shared_w_up, shared_w_down, ln_ffn_g, ln_ffn_b)

    y_prompt = x[:tp].reshape(bp, lp, d_model)
    y_sample = x[tp:].reshape(bs, ls, d_model)
    return (y_prompt, y_sample,
            jnp.stack(p_lru_conv), jnp.stack(p_lru_h), jnp.stack(p_ssd_conv), jnp.stack(p_ssd),
            jnp.stack(s_lru_conv), jnp.stack(s_lru_h), jnp.stack(s_ssd_conv), jnp.stack(s_ssd))
```

```python
import functools
import math

import jax
import jax.numpy as jnp
from jax import lax
from jax.experimental import pallas as pl
from jax.experimental.pallas import tpu as pltpu

F32 = jnp.float32
BF16 = jnp.bfloat16

LN_EPS = 1e-5
RMS_EPS = 1e-5
CONV_WIDTH = 4
LRU_C = 8.0
SSD_CHUNK = 64
SSD_GROUPS = 8
TOP_K = 6
N_EXPERT_GROUPS = 8
TOPK_GROUPS = 4
ROUTE_SCALE = 1.0

LANE = 128
SUBLANE = 8
MXU_DIM = 256
VMEM_LIMIT = 56 * 1024 * 1024

TM_PROJ = 512
TM_COMBINE = 128
BM_EXPERT = 256
TL_LRU = 256
TN_LRU_IN = 1024
TN_SSD_IN = 1536

HIGHEST = lax.Precision.HIGHEST
NT_DIMS = (((1,), (1,)), ((), ()))
TN_DIMS = (((0,), (0,)), ((), ()))


def _sigmoid(x):
    return 1.0 / (1.0 + jnp.exp(-x))


def _silu(x):
    return x * _sigmoid(x)


def _softplus(x):
    return jnp.maximum(x, 0.0) + jnp.log1p(jnp.exp(-jnp.abs(x)))


def _gelu_tanh(x):
    c = math.sqrt(2.0 / math.pi)
    return 0.5 * x * (1.0 + jnp.tanh(c * (x + 0.044715 * (x * x * x))))


def _layer_norm(v, g, b):
    mu = jnp.mean(v, axis=-1, keepdims=True)
    d = v - mu
    var = jnp.mean(d * d, axis=-1, keepdims=True)
    return d * lax.rsqrt(var + LN_EPS) * g + b


def _params(*sem):
    return pltpu.CompilerParams(dimension_semantics=sem, vmem_limit_bytes=VMEM_LIMIT)


def _mm_kernel(x_ref, w_ref, o_ref):
    o_ref[...] = jnp.dot(x_ref[...].astype(BF16), w_ref[...].astype(BF16),
                         preferred_element_type=F32)


def _in_proj_kernel(x_ref, w_ref, o_ref, wb_ref):
    @pl.when(pl.program_id(1) == 0)
    def _():
        wb_ref[...] = w_ref[...].astype(BF16)

    o_ref[...] = jnp.dot(x_ref[...].astype(BF16), wb_ref[...], preferred_element_type=F32)


def _in_proj(x, w, layer, n_cols, tn):
    T, K = x.shape
    tm = TM_PROJ
    return pl.pallas_call(
        _in_proj_kernel,
        grid=(n_cols // tn, T // tm),
        in_specs=[pl.BlockSpec((tm, K), lambda j, i: (i, 0)),
                  pl.BlockSpec((None, K, tn), lambda j, i: (layer, 0, j))],
        out_specs=pl.BlockSpec((tm, tn), lambda j, i: (i, j)),
        out_shape=jax.ShapeDtypeStruct((T, n_cols), F32),
        scratch_shapes=[pltpu.VMEM((K, tn), BF16)],
        compiler_params=_params("arbitrary", "arbitrary"),
        name="in_proj",
    )(x, w)


def _small_proj(x, w):
    T, K = x.shape
    N = w.shape[1]
    tm = TM_PROJ
    return pl.pallas_call(
        _mm_kernel,
        grid=(T // tm,),
        in_specs=[pl.BlockSpec((tm, K), lambda i: (i, 0)),
                  pl.BlockSpec((K, N), lambda i: (0, 0))],
        out_specs=pl.BlockSpec((tm, N), lambda i: (i, 0)),
        out_shape=jax.ShapeDtypeStruct((T, N), F32),
        compiler_params=_params("parallel"),
        name="dt_proj",
    )(x, w)


CONV_PAD = SUBLANE
CONV_HIST = CONV_PAD - (CONV_WIDTH - 1)


def _conv_tile(xx_ref, n, cw, cb):
    acc = cb + cw[CONV_WIDTH - 1:CONV_WIDTH] * xx_ref[CONV_PAD:CONV_PAD + n, :]
    for k in range(CONV_WIDTH - 1):
        off = CONV_HIST + k
        acc = acc + cw[k:k + 1] * xx_ref[off:off + n, :]
    return acc


def _lru_seq_kernel(gate_ref, xr_ref, cbuf_ref, h0_ref, cw_ref, cb_ref, wrg_ref, brg_ref,
                    wig_ref, big_ref, lam_ref, out_ref, nconv_ref, hlast_ref,
                    xx_ref, h_ref, a_ref, u_ref):
    c = pl.program_id(1)
    n = xr_ref.shape[0]
    d = xr_ref.shape[1]

    @pl.when(c == 0)
    def _():
        xx_ref[CONV_HIST:CONV_PAD, :] = cbuf_ref[0]
        h_ref[...] = h0_ref[0]

    xx_ref[CONV_PAD:CONV_PAD + n, :] = xr_ref[...]
    xr = _conv_tile(xx_ref, n, cw_ref[...], cb_ref[...])
    tail = xx_ref[n + CONV_HIST:n + CONV_PAD, :]
    xx_ref[CONV_HIST:CONV_PAD, :] = tail

    xb = xr.astype(BF16)
    nblk = d // MXU_DIM
    r_parts, i_parts = [], []
    for g in range(nblk):
        seg = xb[:, g * MXU_DIM:(g + 1) * MXU_DIM]
        r_parts.append(jnp.dot(seg, wrg_ref[g], preferred_element_type=F32))
        i_parts.append(jnp.dot(seg, wig_ref[g], preferred_element_type=F32))
    r = _sigmoid(jnp.concatenate(r_parts, axis=-1) + brg_ref[...])
    ig = _sigmoid(jnp.concatenate(i_parts, axis=-1) + big_ref[...])
    log_sig_lam = -_softplus(-lam_ref[...])
    log_a = LRU_C * r * log_sig_lam
    a = jnp.exp(log_a)
    a_ref[...] = a
    u_ref[...] = jnp.sqrt(-jnp.tanh(log_a) * (a * a + 1.0)) * (ig * xr)

    def step(t, h):
        h = a_ref[pl.ds(t, 1), :] * h + u_ref[pl.ds(t, 1), :]
        u_ref[pl.ds(t, 1), :] = h
        return h

    h = lax.fori_loop(0, n, step, h_ref[...], unroll=8)
    h_ref[...] = h
    out_ref[...] = (u_ref[...] * _gelu_tanh(gate_ref[...])).astype(BF16)

    @pl.when(c == pl.num_programs(1) - 1)
    def _():
        nconv_ref[0] = tail
        hlast_ref[0] = h


def _lru_seq(gx, row0, bsz, seqlen, tl, conv_buf, h0, cw, cb, wrg, brg, wig, big, lam):
    d = cw.shape[1]
    nt = seqlen // tl
    blk0 = row0 // tl
    row_map = lambda b, c: (blk0 + b * nt + c, 0)
    vec = lambda a: a.reshape(1, d)
    full2 = lambda b, c: (0, 0)
    full3 = lambda b, c: (0, 0, 0)
    return pl.pallas_call(
        _lru_seq_kernel,
        grid=(bsz, nt),
        in_specs=[
            pl.BlockSpec((tl, d), row_map),
            pl.BlockSpec((tl, d), lambda b, c: (blk0 + b * nt + c, 1)),
            pl.BlockSpec((1, CONV_WIDTH - 1, d), lambda b, c: (b, 0, 0)),
            pl.BlockSpec((1, 1, d), lambda b, c: (b, 0, 0)),
            pl.BlockSpec((CONV_WIDTH, d), full2),
            pl.BlockSpec((1, d), full2),
            pl.BlockSpec(wrg.shape, full3),
            pl.BlockSpec((1, d), full2),
            pl.BlockSpec(wig.shape, full3),
            pl.BlockSpec((1, d), full2),
            pl.BlockSpec((1, d), full2),
        ],
        out_specs=[
            pl.BlockSpec((tl, d), lambda b, c: (b * nt + c, 0)),
            pl.BlockSpec((1, CONV_WIDTH - 1, d), lambda b, c: (b, 0, 0)),
            pl.BlockSpec((1, 1, d), lambda b, c: (b, 0, 0)),
        ],
        out_shape=[
            jax.ShapeDtypeStruct((bsz * seqlen, d), BF16),
            jax.ShapeDtypeStruct((bsz, CONV_WIDTH - 1, d), F32),
            jax.ShapeDtypeStruct((bsz, 1, d), F32),
        ],
        scratch_shapes=[
            pltpu.VMEM((tl + CONV_PAD, d), F32),
            pltpu.VMEM((1, d), F32),
            pltpu.VMEM((tl, d), F32),
            pltpu.VMEM((tl, d), F32),
        ],
        compiler_params=_params("arbitrary", "arbitrary"),
        name="lru_seq",
    )(gx, gx, conv_buf, h0.reshape(bsz, 1, d), cw, vec(cb), wrg, vec(brg), wig, vec(big), vec(lam))


def _block_diag_gate(w):
    nb, bw, _ = w.shape
    per = MXU_DIM // bw
    w4 = w.reshape(nb // per, per, bw, bw)
    eye = jnp.eye(per, dtype=w.dtype)
    out = w4[:, :, :, None, :] * eye[None, :, None, :, None]
    return out.reshape(nb // per, MXU_DIM, MXU_DIM).astype(BF16)


def _ssd_seq_kernel(z_ref, xs_ref, bc_ref, dt_ref, cbuf_ref, h0_ref, cw_ref, cb_ref, dtb_ref,
                    alog_ref, dsk_ref, nw_ref, out_ref, nconv_ref, st_ref, xx_ref, y_ref):
    c = pl.program_id(1)
    q = xs_ref.shape[0]
    d_inner = xs_ref.shape[1]
    n_heads = st_ref.shape[1]
    p_dim = st_ref.shape[2]
    n_state = st_ref.shape[3]
    hpg = n_heads // SSD_GROUPS

    @pl.when(c == 0)
    def _():
        xx_ref[CONV_HIST:CONV_PAD, :] = cbuf_ref[0]
        st_ref[...] = h0_ref[...]

    xx_ref[CONV_PAD:CONV_PAD + q, 0:d_inner] = xs_ref[...]
    xx_ref[CONV_PAD:CONV_PAD + q, d_inner:] = bc_ref[...]
    act = _silu(_conv_tile(xx_ref, q, cw_ref[...], cb_ref[...]))
    tail = xx_ref[q + CONV_HIST:q + CONV_PAD, :]
    xx_ref[CONV_HIST:CONV_PAD, :] = tail

    @pl.when(c == pl.num_programs(1) - 1)
    def _():
        nconv_ref[0] = tail

    dt = _softplus(dt_ref[...] + dtb_ref[...])
    adt = dt * (-jnp.exp(alog_ref[...]))
    row = lax.broadcasted_iota(jnp.int32, (q, q), 0)
    col = lax.broadcasted_iota(jnp.int32, (q, q), 1)
    causal = col <= row
    tril = causal.astype(F32)
    cs = jnp.dot(tril, adt, preferred_element_type=F32, precision=HIGHEST)
    eye_h = (lax.broadcasted_iota(jnp.int32, (n_heads, n_heads), 0)
             == lax.broadcasted_iota(jnp.int32, (n_heads, n_heads), 1)).astype(F32)
    cs_t = lax.dot_general(eye_h, cs, NT_DIMS, preferred_element_type=F32, precision=HIGHEST)
    dsk = dsk_ref[...]
    b_off = d_inner
    c_off = d_inner + SSD_GROUPS * n_state

    for g in range(SSD_GROUPS):
        bg = act[:, b_off + g * n_state:b_off + (g + 1) * n_state].astype(BF16)
        cg = act[:, c_off + g * n_state:c_off + (g + 1) * n_state].astype(BF16)
        cb_mat = lax.dot_general(cg, bg, NT_DIMS, preferred_element_type=F32)
        for e in range(hpg):
            h = g * hpg + e
            cs_h = cs[:, h:h + 1]
            cs_last = cs[q - 1:q, h:h + 1]
            lmat = jnp.exp(jnp.where(causal, cs_h - cs_t[h:h + 1, :], -jnp.inf))
            xs_h = act[:, h * p_dim:(h + 1) * p_dim]
            xd = xs_h * dt[:, h:h + 1]
            y = jnp.dot((cb_mat * lmat).astype(BF16), xd.astype(BF16), preferred_element_type=F32)
            prev = st_ref[0, h]
            y_off = lax.dot_general(cg, prev.astype(BF16), NT_DIMS, preferred_element_type=F32)
            y = y + y_off * jnp.exp(cs_h)
            xdec = (xd * jnp.exp(cs_last - cs_h)).astype(BF16)
            st_new = lax.dot_general(xdec, bg, TN_DIMS, preferred_element_type=F32)
            st_ref[0, h] = jnp.exp(cs_last) * prev + st_new
            y_ref[:, h * p_dim:(h + 1) * p_dim] = y + dsk[:, h:h + 1] * xs_h

    gated = y_ref[...] * _silu(z_ref[...])
    gw = d_inner // SSD_GROUPS
    nw = nw_ref[...]
    for g in range(SSD_GROUPS):
        seg = gated[:, g * gw:(g + 1) * gw]
        ms = jnp.mean(seg * seg, axis=-1, keepdims=True)
        out_ref[:, g * gw:(g + 1) * gw] = (seg * lax.rsqrt(ms + RMS_EPS) * nw[:, g * gw:(g + 1) * gw]).astype(BF16)


def _ssd_seq(zx, dt_raw, row0, bsz, seqlen, q, conv_buf, h0, cw, cb, dtb, alog, dsk, nw):
    n_heads, p_dim, n_state = h0.shape[1:]
    d_inner = n_heads * p_dim
    conv_dim = cw.shape[1]
    nt = seqlen // q
    blk0 = row0 // q
    full2 = lambda b, c: (0, 0)
    col = lambda j: (lambda b, c: (blk0 + b * nt + c, j))
    return pl.pallas_call(
        _ssd_seq_kernel,
        grid=(bsz, nt),
        in_specs=[
            pl.BlockSpec((q, d_inner), col(0)),
            pl.BlockSpec((q, d_inner), col(1)),
            pl.BlockSpec((q, d_inner), col(2)),
            pl.BlockSpec((q, n_heads), col(0)),
            pl.BlockSpec((1, CONV_WIDTH - 1, conv_dim), lambda b, c: (b, 0, 0)),
            pl.BlockSpec((1, n_heads, p_dim, n_state), lambda b, c: (b, 0, 0, 0)),
            pl.BlockSpec((CONV_WIDTH, conv_dim), full2),
            pl.BlockSpec((1, conv_dim), full2),
            pl.BlockSpec((1, n_heads), full2),
            pl.BlockSpec((1, n_heads), full2),
            pl.BlockSpec((1, n_heads), full2),
            pl.BlockSpec((1, d_inner), full2),
        ],
        out_specs=[
            pl.BlockSpec((q, d_inner), lambda b, c: (b * nt + c, 0)),
            pl.BlockSpec((1, CONV_WIDTH - 1, conv_dim), lambda b, c: (b, 0, 0)),
            pl.BlockSpec((1, n_heads, p_dim, n_state), lambda b, c: (b, 0, 0, 0)),
        ],
        out_shape=[
            jax.ShapeDtypeStruct((bsz * seqlen, d_inner), BF16),
            jax.ShapeDtypeStruct((bsz, CONV_WIDTH - 1, conv_dim), F32),
            jax.ShapeDtypeStruct((bsz, n_heads, p_dim, n_state), F32),
        ],
        scratch_shapes=[
            pltpu.VMEM((q + CONV_PAD, conv_dim), F32),
            pltpu.VMEM((q, d_inner), F32),
        ],
        compiler_params=_params("arbitrary", "arbitrary"),
        name="ssd_seq",
    )(zx, zx, zx, dt_raw, conv_buf, h0, cw, cb.reshape(1, -1), dtb.reshape(1, -1),
      alog.reshape(1, -1), dsk.reshape(1, -1), nw.reshape(1, -1))


def _first_argmax(v, rid, n):
    m = jnp.max(v, axis=0, keepdims=True)
    idx = jnp.min(jnp.where(v == m, rid, float(n)), axis=0, keepdims=True)
    return m, idx


def _route_tile(scores_t, bias_col):
    n_exp, tm = scores_t.shape
    per = n_exp // N_EXPERT_GROUPS
    neg = -jnp.inf
    biased = scores_t + bias_col
    rid_g = lax.broadcasted_iota(jnp.int32, (per, tm), 0).astype(F32)
    tiles, gs_rows = [], []
    for g in range(N_EXPERT_GROUPS):
        v = biased[g * per:(g + 1) * per, :]
        tiles.append(v)
        m1, i1 = _first_argmax(v, rid_g, per)
        m2 = jnp.max(jnp.where(rid_g == i1, neg, v), axis=0, keepdims=True)
        gs_rows.append(m1 + m2)
    gs = jnp.concatenate(gs_rows, axis=0)
    rid_grp = lax.broadcasted_iota(jnp.int32, (N_EXPERT_GROUPS, tm), 0).astype(F32)
    gsel = jnp.zeros((N_EXPERT_GROUPS, tm), F32)
    for _ in range(TOPK_GROUPS):
        _, gi = _first_argmax(gs, rid_grp, N_EXPERT_GROUPS)
        hit = rid_grp == gi
        gsel = jnp.where(hit, 1.0, gsel)
        gs = jnp.where(hit, neg, gs)
    masked = jnp.concatenate(
        [jnp.where(gsel[g:g + 1, :] > 0.0, tiles[g], neg) for g in range(N_EXPERT_GROUPS)], axis=0)
    rid = lax.broadcasted_iota(jnp.int32, (n_exp, tm), 0).astype(F32)
    sel = jnp.zeros((n_exp, tm), F32)
    ids, picked = [], []
    for _ in range(TOP_K):
        _, ei = _first_argmax(masked, rid, n_exp)
        hit = rid == ei
        ids.append(ei)
        picked.append(jnp.sum(jnp.where(hit, scores_t, 0.0), axis=0, keepdims=True))
        sel = jnp.where(hit, 1.0, sel)
        masked = jnp.where(hit, neg, masked)
    total = picked[0]
    for p in picked[1:]:
        total = total + p
    gates = [p / total * ROUTE_SCALE for p in picked]
    return sel, rid, ids, gates


def _proj_ln_router_kernel(alpha, h_ref, w_ref, x_ref, g_ref, b_ref, rwt_ref, rb_ref,
                           o_ref, eid_ref, gate_ref, rank_ref, cnt_ref, wb_ref, tri_ref):
    i = pl.program_id(0)
    tm = x_ref.shape[0]

    @pl.when(i == 0)
    def _():
        wb_ref[...] = w_ref[...].astype(BF16)
        r = lax.broadcasted_iota(jnp.int32, (tm, tm), 0)
        c = lax.broadcasted_iota(jnp.int32, (tm, tm), 1)
        tri_ref[...] = jnp.where(r < c, 1.0, 0.0).astype(BF16)
        cnt_ref[...] = jnp.zeros_like(cnt_ref)

    mix = jnp.dot(h_ref[...], wb_ref[...], preferred_element_type=F32)
    x1 = _layer_norm(alpha * x_ref[...] + mix, g_ref[...], b_ref[...])
    o_ref[...] = x1
    logits_t = lax.dot_general(rwt_ref[...].astype(BF16), x1.astype(BF16), NT_DIMS,
                               preferred_element_type=F32)
    sel, rid, ids, gates = _route_tile(_sigmoid(logits_t), rb_ref[...])

    sel_b = sel.astype(BF16)
    before = jnp.dot(sel_b, tri_ref[...], preferred_element_type=F32) + cnt_ref[:, 0:1]
    pad_rows = eid_ref.shape[0] - TOP_K
    zrow = jnp.zeros((pad_rows, tm), F32)
    ranks = [jnp.sum(jnp.where(rid == ei, before, 0.0), axis=0, keepdims=True) for ei in ids]
    eid_ref[...] = jnp.concatenate(ids + [zrow], axis=0).astype(jnp.int32)
    gate_ref[...] = jnp.concatenate(gates + [zrow], axis=0)
    rank_ref[...] = jnp.concatenate(ranks + [zrow], axis=0).astype(jnp.int32)
    ones = jnp.ones((tm, cnt_ref.shape[1]), BF16)
    cnt_ref[...] = cnt_ref[...] + jnp.dot(sel_b, ones, preferred_element_type=F32)


def _proj_ln_router(h, w, wl, x, alpha, ln_g, ln_b, router_w_t, router_bias_col):
    T, K = h.shape
    D = x.shape[1]
    E = router_w_t.shape[0]
    tm = TM_PROJ
    kp = SUBLANE
    row_spec = pl.BlockSpec((kp, tm), lambda i: (0, i))
    return pl.pallas_call(
        functools.partial(_proj_ln_router_kernel, alpha),
        grid=(T // tm,),
        in_specs=[
            pl.BlockSpec((tm, K), lambda i: (i, 0)),
            pl.BlockSpec((None, K, D), lambda i: (wl, 0, 0)),
            pl.BlockSpec((tm, D), lambda i: (i, 0)),
            pl.BlockSpec((1, D), lambda i: (0, 0)),
            pl.BlockSpec((1, D), lambda i: (0, 0)),
            pl.BlockSpec((E, D), lambda i: (0, 0)),
            pl.BlockSpec((E, 1), lambda i: (0, 0)),
        ],
        out_specs=[pl.BlockSpec((tm, D), lambda i: (i, 0)), row_spec, row_spec, row_spec,
                   pl.BlockSpec((E, LANE), lambda i: (0, 0))],
        out_shape=[jax.ShapeDtypeStruct((T, D), F32),
                   jax.ShapeDtypeStruct((kp, T), jnp.int32),
                   jax.ShapeDtypeStruct((kp, T), F32),
                   jax.ShapeDtypeStruct((kp, T), jnp.int32),
                   jax.ShapeDtypeStruct((E, LANE), F32)],
        scratch_shapes=[pltpu.VMEM((K, D), BF16), pltpu.VMEM((tm, tm), BF16)],
        compiler_params=_params("arbitrary"),
        name="proj_ln_router",
    )(h, w, x, ln_g.reshape(1, D), ln_b.reshape(1, D), router_w_t, router_bias_col)


def _start_row_gather(idx_ref, base, n, src_hbm, dst, sem):
    for i in range(n):
        r = idx_ref[base + i]
        pltpu.make_async_copy(src_hbm.at[pl.ds(r, 1), :], dst.at[pl.ds(i, 1), :], sem).start(priority=i % 2)


def _wait_row_gather(n, src_hbm, dst, sem):
    pltpu.make_async_copy(src_hbm.at[pl.ds(0, n), :], dst, sem).wait()


def _expert_kernel(be_ref, nu_ref, tok_ref, x_hbm, wg_ref, wu_ref, wd_ref, y_ref,
                   xbuf, wgb, wub, wdb, sem):
    b = pl.program_id(0)
    bm = xbuf.shape[1]
    n_used = nu_ref[0]
    slot = lax.rem(b, 2)

    @pl.when(jnp.logical_and(b == 0, n_used > 0))
    def _():
        _start_row_gather(tok_ref, 0, bm, x_hbm, xbuf.at[0], sem.at[0])

    @pl.when(b + 1 < n_used)
    def _():
        _start_row_gather(tok_ref, (b + 1) * bm, bm, x_hbm, xbuf.at[1 - slot], sem.at[1 - slot])

    @pl.when(b < n_used)
    def _():
        new_expert = jnp.logical_or(b == 0, be_ref[b] != be_ref[jnp.maximum(b - 1, 0)])

        @pl.when(new_expert)
        def _():
            wgb[...] = wg_ref[...].astype(BF16)
            wub[...] = wu_ref[...].astype(BF16)
            wdb[...] = wd_ref[...].astype(BF16)

        _wait_row_gather(bm, x_hbm, xbuf.at[slot], sem.at[slot])
        xb = xbuf[slot].astype(BF16)
        hg = jnp.dot(xb, wgb[...], preferred_element_type=F32)
        hu = jnp.dot(xb, wub[...], preferred_element_type=F32)
        hh = (_silu(hg) * hu).astype(BF16)
        y_ref[...] = jnp.dot(hh, wdb[...], preferred_element_type=F32)

    @pl.when(b >= n_used)
    def _():
        y_ref[...] = jnp.zeros_like(y_ref)


def _experts(x, slot_tok, block_e, n_used, wg, wu, wd, layer):
    T, D = x.shape
    DE = wg.shape[3]
    bm = BM_EXPERT
    nb = slot_tok.shape[0] // bm
    grid_spec = pltpu.PrefetchScalarGridSpec(
        num_scalar_prefetch=3,
        grid=(nb,),
        in_specs=[
            pl.BlockSpec(memory_space=pl.ANY),
            pl.BlockSpec((None, None, D, DE), lambda b, be, nu, tok: (layer, be[b], 0, 0)),
            pl.BlockSpec((None, None, D, DE), lambda b, be, nu, tok: (layer, be[b], 0, 0)),
            pl.BlockSpec((None, None, DE, D), lambda b, be, nu, tok: (layer, be[b], 0, 0)),
        ],
        out_specs=pl.BlockSpec((bm, D), lambda b, be, nu, tok: (b, 0)),
        scratch_shapes=[
            pltpu.VMEM((2, bm, D), F32),
            pltpu.VMEM((D, DE), BF16),
            pltpu.VMEM((D, DE), BF16),
            pltpu.VMEM((DE, D), BF16),
            pltpu.SemaphoreType.DMA((2,)),
        ],
    )
    return pl.pallas_call(
        _expert_kernel,
        grid_spec=grid_spec,
        out_shape=jax.ShapeDtypeStruct((nb * bm, D), F32),
        compiler_params=_params("arbitrary"),
        name="experts",
    )(block_e, n_used, slot_tok, x, wg, wu, wd)


def _combine_kernel(alpha, pos_ref, gates_ref, x_ref, y_hbm, sg_ref, su_ref, sd_ref, g_ref, b_ref,
                    o_ref, ybuf, sgb, sub, sdb, sem):
    i = pl.program_id(0)
    tm = x_ref.shape[0]
    n_rows = ybuf.shape[1]
    slot = lax.rem(i, 2)

    @pl.when(i == 0)
    def _():
        _start_row_gather(pos_ref, 0, n_rows, y_hbm, ybuf.at[0], sem.at[0])
        sgb[...] = sg_ref[...].astype(BF16)
        sub[...] = su_ref[...].astype(BF16)
        sdb[...] = sd_ref[...].astype(BF16)

    @pl.when(i + 1 < pl.num_programs(0))
    def _():
        _start_row_gather(pos_ref, (i + 1) * n_rows, n_rows, y_hbm, ybuf.at[1 - slot], sem.at[1 - slot])

    x = x_ref[...]
    xb = x.astype(BF16)
    hg = jnp.dot(xb, sgb[...], preferred_element_type=F32)
    hu = jnp.dot(xb, sub[...], preferred_element_type=F32)
    acc = jnp.dot((_silu(hg) * hu).astype(BF16), sdb[...], preferred_element_type=F32)
    _wait_row_gather(n_rows, y_hbm, ybuf.at[slot], sem.at[slot])
    gates = gates_ref[...]
    for k in range(TOP_K):
        acc = acc + gates[:, k:k + 1] * ybuf[slot, k * tm:(k + 1) * tm, :]
    o_ref[...] = _layer_norm(alpha * x + acc, g_ref[...], b_ref[...])


def _combine(x, y_sorted, pos_tiles, gates, sg, su, sd, alpha, ln_g, ln_b, layer):
    T, D = x.shape
    DS = sg.shape[2]
    tm = TM_COMBINE
    grid_spec = pltpu.PrefetchScalarGridSpec(
        num_scalar_prefetch=1,
        grid=(T // tm,),
        in_specs=[
            pl.BlockSpec((tm, TOP_K), lambda i, pos: (i, 0)),
            pl.BlockSpec((tm, D), lambda i, pos: (i, 0)),
            pl.BlockSpec(memory_space=pl.ANY),
            pl.BlockSpec((None, D, DS), lambda i, pos: (layer, 0, 0)),
            pl.BlockSpec((None, D, DS), lambda i, pos: (layer, 0, 0)),
            pl.BlockSpec((None, DS, D), lambda i, pos: (layer, 0, 0)),
            pl.BlockSpec((None, 1, D), lambda i, pos: (layer, 0, 0)),
            pl.BlockSpec((None, 1, D), lambda i, pos: (layer, 0, 0)),
        ],
        out_specs=pl.BlockSpec((tm, D), lambda i, pos: (i, 0)),
        scratch_shapes=[
            pltpu.VMEM((2, TOP_K * tm, D), F32),
            pltpu.VMEM((D, DS), BF16),
            pltpu.VMEM((D, DS), BF16),
            pltpu.VMEM((DS, D), BF16),
            pltpu.SemaphoreType.DMA((2,)),
        ],
    )
    return pl.pallas_call(
        functools.partial(_combine_kernel, alpha),
        grid_spec=grid_spec,
        out_shape=jax.ShapeDtypeStruct((T, D), F32),
        compiler_params=_params("arbitrary"),
        name="combine",
    )(pos_tiles, gates, x, y_sorted, sg, su, sd, ln_g.reshape(-1, 1, D), ln_b.reshape(-1, 1, D))


def _dispatch(eid_t, rank_t, counts):
    T = eid_t.shape[1]
    E = counts.shape[0]
    A = T * TOP_K
    bm = BM_EXPERT
    nb = -(-A // bm) + E
    cnt = counts[:, 0].astype(jnp.int32)
    padded = (cnt + bm - 1) // bm * bm
    pends = jnp.cumsum(padded)
    pstart = pends - padded
    eid = eid_t[:TOP_K]
    onehot = eid[:, :, None] == jnp.arange(E, dtype=jnp.int32)[None, None, :]
    dest = rank_t[:TOP_K] + jnp.sum(jnp.where(onehot, pstart[None, None, :], 0), axis=-1)
    tok = jnp.broadcast_to(jnp.arange(T, dtype=jnp.int32)[None, :], (TOP_K, T))
    slot_tok = jnp.zeros((nb * bm,), jnp.int32).at[dest.reshape(-1)].set(tok.reshape(-1))
    block_start = jnp.arange(nb, dtype=jnp.int32) * bm
    block_e = jnp.minimum(jnp.sum((pends[None, :] <= block_start[:, None]).astype(jnp.int32), axis=1), E - 1)
    n_used = (pends[-1:] // bm).astype(jnp.int32)
    tm = TM_COMBINE
    pos_tiles = dest.reshape(TOP_K, T // tm, tm).transpose(1, 0, 2).reshape(-1)
    return slot_tok, block_e, n_used, pos_tiles


def _moe(x1, eid_t, gate_t, rank_t, counts, alpha, layer, wg, wu, wd, sg, su, sd, ln_g, ln_b):
    slot_tok, block_e, n_used, pos_tiles = _dispatch(eid_t, rank_t, counts)
    y_sorted = _experts(x1, slot_tok, block_e, n_used, wg, wu, wd, layer)
    gates = gate_t[:TOP_K].T
    return _combine(x1, y_sorted, pos_tiles, gates, sg, su, sd, alpha, ln_g, ln_b, layer)


def kernel(x_prompt, x_sample, state_lru_conv, state_lru_h, state_ssd_conv, state_ssd, lru_w_in, lru_conv_w, lru_conv_b, lru_w_rgate, lru_b_rgate, lru_w_igate, lru_b_igate, lru_lambda, lru_w_out, ssd_w_in, ssd_conv_w, ssd_conv_b, ssd_dt_bias, ssd_a_log, ssd_d, ssd_norm_w, ssd_w_out, ln_mix_g, ln_mix_b, ln_ffn_g, ln_ffn_b, router_w, router_bias, moe_w_gate, moe_w_up, moe_w_down, shared_w_gate, shared_w_up, shared_w_down):
    bp, lp, d_model = x_prompt.shape
    bs, ls, _ = x_sample.shape
    depth = ln_mix_g.shape[0]
    alpha = (2.0 * depth) ** 0.25
    tp, ts = bp * lp, bs * ls
    n_heads, p_dim, n_state = state_ssd.shape[2:]
    d_inner = n_heads * p_dim
    conv_dim = ssd_conv_w.shape[2]
    d_rnn = lru_conv_w.shape[2]

    x = jnp.concatenate([x_prompt.reshape(tp, d_model), x_sample.reshape(ts, d_model)], axis=0)
    tl_p = min(TL_LRU, lp)
    tl_s = min(TL_LRU, ls)
    q_p = min(SSD_CHUNK, lp)
    q_s = min(SSD_CHUNK, ls)

    p_lru_conv, p_lru_h, p_ssd_conv, p_ssd = [], [], [], []
    s_lru_conv, s_lru_h, s_ssd_conv, s_ssd = [], [], [], []
    for i in range(depth):
        j = i // 2
        if i % 2 == 0:
            gx = _in_proj(x, lru_w_in, j, 2 * d_rnn, TN_LRU_IN)
            wrg = _block_diag_gate(lru_w_rgate[j])
            wig = _block_diag_gate(lru_w_igate[j])
            common = (lru_conv_w[j], lru_conv_b[j], wrg, lru_b_rgate[j], wig, lru_b_igate[j], lru_lambda[j])
            hp, cp, lp_h = _lru_seq(gx, 0, bp, lp, tl_p, jnp.zeros((bp, CONV_WIDTH - 1, d_rnn), F32),
                                    jnp.zeros((bp, d_rnn), F32), *common)
            hs, cs_, ls_h = _lru_seq(gx, tp, bs, ls, tl_s, state_lru_conv[j], state_lru_h[j], *common)
            p_lru_conv.append(cp)
            p_lru_h.append(lp_h.reshape(bp, d_rnn))
            s_lru_conv.append(cs_)
            s_lru_h.append(ls_h.reshape(bs, d_rnn))
            mixed = jnp.concatenate([hp, hs], axis=0)
            w_out, wl = lru_w_out, j
        else:
            zx = _in_proj(x, ssd_w_in, j, d_inner + conv_dim, TN_SSD_IN)
            dt_raw = _small_proj(x, ssd_w_in[j][:, d_inner + conv_dim:])
            common = (ssd_conv_w[j], ssd_conv_b[j], ssd_dt_bias[j], ssd_a_log[j], ssd_d[j], ssd_norm_w[j])
            hp, cp, sp = _ssd_seq(zx, dt_raw, 0, bp, lp, q_p, jnp.zeros((bp, CONV_WIDTH - 1, conv_dim), F32),
                                  jnp.zeros((bp, n_heads, p_dim, n_state), F32), *common)
            hs, cs_, ss = _ssd_seq(zx, dt_raw, tp, bs, ls, q_s, state_ssd_conv[j], state_ssd[j], *common)
            p_ssd_conv.append(cp)
            p_ssd.append(sp)
            s_ssd_conv.append(cs_)
            s_ssd.append(ss)
            mixed = jnp.concatenate([hp, hs], axis=0)
            w_out, wl = ssd_w_out, j
        x1, eid_t, gate_t, rank_t, counts = _proj_ln_router(
            mixed, w_out, wl, x, alpha, ln_mix_g[i], ln_mix_b[i], router_w[i].T, router_bias[i].reshape(-1, 1))
        x = _moe(x1, eid_t, gate_t, rank_t, counts, alpha, i, moe_w_gate, moe_w_up, moe_w_down,
                 shared_w_gate, shared_w_up, shared_w_down, ln_ffn_g, ln_ffn_b)

    y_prompt = x[:tp].reshape(bp, lp, d_model)
    y_sample = x[tp:].reshape(bs, ls, d_model)
    return (y_prompt, y_sample,
            jnp.stack(p_lru_conv), jnp.stack(p_lru_h), jnp.stack(p_ssd_conv), jnp.stack(p_ssd),
            jnp.stack(s_lru_conv), jnp.stack(s_lru_h), jnp.stack(s_ssd_conv), jnp.stack(s_ssd))
```

```python
import functools
import math

import jax
import jax.numpy as jnp
from jax import lax
from jax.experimental import pallas as pl
from jax.experimental.pallas import tpu as pltpu

F32 = jnp.float32
BF16 = jnp.bfloat16

LN_EPS = 1e-5
RMS_EPS = 1e-5
CONV_WIDTH = 4
LRU_C = 8.0
SSD_CHUNK = 64
SSD_GROUPS = 8
TOP_K = 6
N_EXPERT_GROUPS = 8
TOPK_GROUPS = 4
ROUTE_SCALE = 1.0

LANE = 128
SUBLANE = 8
MXU_DIM = 256
VMEM_LIMIT = 56 * 1024 * 1024

TM_PROJ = 512
TM_COMBINE = 128
BM_EXPERT = 256
TL_LRU = 256
TN_LRU_IN = 1024
TN_SSD_IN = 1536

HIGHEST = lax.Precision.HIGHEST
NT_DIMS = (((1,), (1,)), ((), ()))
TN_DIMS = (((0,), (0,)), ((), ()))


def _sigmoid(x):
    return 1.0 / (1.0 + jnp.exp(-x))


def _silu(x):
    return x * _sigmoid(x)


def _softplus(x):
    return jnp.maximum(x, 0.0) + jnp.log1p(jnp.exp(-jnp.abs(x)))


def _gelu_tanh(x):
    c = math.sqrt(2.0 / math.pi)
    return 0.5 * x * (1.0 + jnp.tanh(c * (x + 0.044715 * (x * x * x))))


def _layer_norm(v, g, b):
    mu = jnp.mean(v, axis=-1, keepdims=True)
    d = v - mu
    var = jnp.mean(d * d, axis=-1, keepdims=True)
    return d * lax.rsqrt(var + LN_EPS) * g + b


def _params(*sem):
    return pltpu.CompilerParams(dimension_semantics=sem, vmem_limit_bytes=VMEM_LIMIT)


def _mm_kernel(x_ref, w_ref, o_ref):
    o_ref[...] = jnp.dot(x_ref[...].astype(BF16), w_ref[...].astype(BF16),
                         preferred_element_type=F32)


def _in_proj_kernel(x_ref, w_ref, o_ref, wb_ref):
    @pl.when(pl.program_id(1) == 0)
    def _():
        wb_ref[...] = w_ref[...].astype(BF16)

    o_ref[...] = jnp.dot(x_ref[...].astype(BF16), wb_ref[...], preferred_element_type=F32)


def _in_proj(x, w, layer, n_cols, tn):
    T, K = x.shape
    tm = TM_PROJ
    return pl.pallas_call(
        _in_proj_kernel,
        grid=(n_cols // tn, T // tm),
        in_specs=[pl.BlockSpec((tm, K), lambda j, i: (i, 0)),
                  pl.BlockSpec((None, K, tn), lambda j, i: (layer, 0, j))],
        out_specs=pl.BlockSpec((tm, tn), lambda j, i: (i, j)),
        out_shape=jax.ShapeDtypeStruct((T, n_cols), F32),
        scratch_shapes=[pltpu.VMEM((K, tn), BF16)],
        compiler_params=_params("arbitrary", "arbitrary"),
        name="in_proj",
    )(x, w)


def _small_proj(x, w):
    T, K = x.shape
    N = w.shape[1]
    tm = TM_PROJ
    return pl.pallas_call(
        _mm_kernel,
        grid=(T // tm,),
        in_specs=[pl.BlockSpec((tm, K), lambda i: (i, 0)),
                  pl.BlockSpec((K, N), lambda i: (0, 0))],
        out_specs=pl.BlockSpec((tm, N), lambda i: (i, 0)),
        out_shape=jax.ShapeDtypeStruct((T, N), F32),
        compiler_params=_params("parallel"),
        name="dt_proj",
    )(x, w)


CONV_PAD = SUBLANE
CONV_HIST = CONV_PAD - (CONV_WIDTH - 1)


def _conv_tile(xx_ref, n, cw, cb):
    acc = cb + cw[CONV_WIDTH - 1:CONV_WIDTH] * xx_ref[CONV_PAD:CONV_PAD + n, :]
    for k in range(CONV_WIDTH - 1):
        off = CONV_HIST + k
        acc = acc + cw[k:k + 1] * xx_ref[off:off + n, :]
    return acc


def _lru_seq_kernel(gate_ref, xr_ref, cbuf_ref, h0_ref, cw_ref, cb_ref, wrg_ref, brg_ref,
                    wig_ref, big_ref, lam_ref, out_ref, nconv_ref, hlast_ref,
                    xx_ref, h_ref, a_ref, u_ref):
    c = pl.program_id(1)
    n = xr_ref.shape[0]
    d = xr_ref.shape[1]

    @pl.when(c == 0)
    def _():
        xx_ref[CONV_HIST:CONV_PAD, :] = cbuf_ref[0]
        h_ref[...] = h0_ref[0]

    xx_ref[CONV_PAD:CONV_PAD + n, :] = xr_ref[...]
    xr = _conv_tile(xx_ref, n, cw_ref[...], cb_ref[...])
    tail = xx_ref[n + CONV_HIST:n + CONV_PAD, :]
    xx_ref[CONV_HIST:CONV_PAD, :] = tail

    xb = xr.astype(BF16)
    nblk = d // MXU_DIM
    r_parts, i_parts = [], []
    for g in range(nblk):
        seg = xb[:, g * MXU_DIM:(g + 1) * MXU_DIM]
        r_parts.append(jnp.dot(seg, wrg_ref[g], preferred_element_type=F32))
        i_parts.append(jnp.dot(seg, wig_ref[g], preferred_element_type=F32))
    r = _sigmoid(jnp.concatenate(r_parts, axis=-1) + brg_ref[...])
    ig = _sigmoid(jnp.concatenate(i_parts, axis=-1) + big_ref[...])
    log_sig_lam = -_softplus(-lam_ref[...])
    log_a = LRU_C * r * log_sig_lam
    a = jnp.exp(log_a)
    a_ref[...] = a
    u_ref[...] = jnp.sqrt(-jnp.tanh(log_a) * (a * a + 1.0)) * (ig * xr)

    def step(t, h):
        h = a_ref[pl.ds(t, 1), :] * h + u_ref[pl.ds(t, 1), :]
        u_ref[pl.ds(t, 1), :] = h
        return h

    h = lax.fori_loop(0, n, step, h_ref[...], unroll=8)
    h_ref[...] = h
    out_ref[...] = (u_ref[...] * _gelu_tanh(gate_ref[...])).astype(BF16)

    @pl.when(c == pl.num_programs(1) - 1)
    def _():
        nconv_ref[0] = tail
        hlast_ref[0] = h


def _lru_seq(gx, row0, bsz, seqlen, tl, conv_buf, h0, cw, cb, wrg, brg, wig, big, lam):
    d = cw.shape[1]
    nt = seqlen // tl
    blk0 = row0 // tl
    row_map = lambda b, c: (blk0 + b * nt + c, 0)
    vec = lambda a: a.reshape(1, d)
    full2 = lambda b, c: (0, 0)
    full3 = lambda b, c: (0, 0, 0)
    return pl.pallas_call(
        _lru_seq_kernel,
        grid=(bsz, nt),
        in_specs=[
            pl.BlockSpec((tl, d), row_map),
            pl.BlockSpec((tl, d), lambda b, c: (blk0 + b * nt + c, 1)),
            pl.BlockSpec((1, CONV_WIDTH - 1, d), lambda b, c: (b, 0, 0)),
            pl.BlockSpec((1, 1, d), lambda b, c: (b, 0, 0)),
            pl.BlockSpec((CONV_WIDTH, d), full2),
            pl.BlockSpec((1, d), full2),
            pl.BlockSpec(wrg.shape, full3),
            pl.BlockSpec((1, d), full2),
            pl.BlockSpec(wig.shape, full3),
            pl.BlockSpec((1, d), full2),
            pl.BlockSpec((1, d), full2),
        ],
        out_specs=[
            pl.BlockSpec((tl, d), lambda b, c: (b * nt + c, 0)),
            pl.BlockSpec((1, CONV_WIDTH - 1, d), lambda b, c: (b, 0, 0)),
            pl.BlockSpec((1, 1, d), lambda b, c: (b, 0, 0)),
        ],
        out_shape=[
            jax.ShapeDtypeStruct((bsz * seqlen, d), BF16),
            jax.ShapeDtypeStruct((bsz, CONV_WIDTH - 1, d), F32),
            jax.ShapeDtypeStruct((bsz, 1, d), F32),
        ],
        scratch_shapes=[
            pltpu.VMEM((tl + CONV_PAD, d), F32),
            pltpu.VMEM((1, d), F32),
            pltpu.VMEM((tl, d), F32),
            pltpu.VMEM((tl, d), F32),
        ],
        compiler_params=_params("arbitrary", "arbitrary"),
        name="lru_seq",
    )(gx, gx, conv_buf, h0.reshape(bsz, 1, d), cw, vec(cb), wrg, vec(brg), wig, vec(big), vec(lam))


def _block_diag_gate(w):
    nb, bw, _ = w.shape
    per = MXU_DIM // bw
    w4 = w.reshape(nb // per, per, bw, bw)
    eye = jnp.eye(per, dtype=w.dtype)
    out = w4[:, :, :, None, :] * eye[None, :, None, :, None]
    return out.reshape(nb // per, MXU_DIM, MXU_DIM).astype(BF16)


def _ssd_seq_kernel(z_ref, xs_ref, bc_ref, dt_ref, cbuf_ref, h0_ref, cw_ref, cb_ref, dtb_ref,
                    alog_ref, dsk_ref, nw_ref, out_ref, nconv_ref, st_ref, xx_ref, y_ref):
    c = pl.program_id(1)
    q = xs_ref.shape[0]
    d_inner = xs_ref.shape[1]
    n_heads = st_ref.shape[1]
    p_dim = st_ref.shape[2]
    n_state = st_ref.shape[3]
    hpg = n_heads // SSD_GROUPS

    @pl.when(c == 0)
    def _():
        xx_ref[CONV_HIST:CONV_PAD, :] = cbuf_ref[0]
        st_ref[...] = h0_ref[...]

    xx_ref[CONV_PAD:CONV_PAD + q, 0:d_inner] = xs_ref[...]
    xx_ref[CONV_PAD:CONV_PAD + q, d_inner:] = bc_ref[...]
    act = _silu(_conv_tile(xx_ref, q, cw_ref[...], cb_ref[...]))
    tail = xx_ref[q + CONV_HIST:q + CONV_PAD, :]
    xx_ref[CONV_HIST:CONV_PAD, :] = tail

    @pl.when(c == pl.num_programs(1) - 1)
    def _():
        nconv_ref[0] = tail

    dt = _softplus(dt_ref[...] + dtb_ref[...])
    adt = dt * (-jnp.exp(alog_ref[...]))
    row = lax.broadcasted_iota(jnp.int32, (q, q), 0)
    col = lax.broadcasted_iota(jnp.int32, (q, q), 1)
    causal = col <= row
    tril = causal.astype(F32)
    cs = jnp.dot(tril, adt, preferred_element_type=F32, precision=HIGHEST)
    eye_h = (lax.broadcasted_iota(jnp.int32, (n_heads, n_heads), 0)
             == lax.broadcasted_iota(jnp.int32, (n_heads, n_heads), 1)).astype(F32)
    cs_t = lax.dot_general(eye_h, cs, NT_DIMS, preferred_element_type=F32, precision=HIGHEST)
    dsk = dsk_ref[...]
    b_off = d_inner
    c_off = d_inner + SSD_GROUPS * n_state

    for g in range(SSD_GROUPS):
        bg = act[:, b_off + g * n_state:b_off + (g + 1) * n_state].astype(BF16)
        cg = act[:, c_off + g * n_state:c_off + (g + 1) * n_state].astype(BF16)
        cb_mat = lax.dot_general(cg, bg, NT_DIMS, preferred_element_type=F32)
        for e in range(hpg):
            h = g * hpg + e
            cs_h = cs[:, h:h + 1]
            cs_last = cs[q - 1:q, h:h + 1]
            lmat = jnp.exp(jnp.where(causal, cs_h - cs_t[h:h + 1, :], -jnp.inf))
            xs_h = act[:, h * p_dim:(h + 1) * p_dim]
            xd = xs_h * dt[:, h:h + 1]
            y = jnp.dot((cb_mat * lmat).astype(BF16), xd.astype(BF16), preferred_element_type=F32)
            prev = st_ref[0, h]
            y_off = lax.dot_general(cg, prev.astype(BF16), NT_DIMS, preferred_element_type=F32)
            y = y + y_off * jnp.exp(cs_h)
            xdec = (xd * jnp.exp(cs_last - cs_h)).astype(BF16)
            st_new = lax.dot_general(xdec, bg, TN_DIMS, preferred_element_type=F32)
            st_ref[0, h] = jnp.exp(cs_last) * prev + st_new
            y_ref[:, h * p_dim:(h + 1) * p_dim] = y + dsk[:, h:h + 1] * xs_h

    gated = y_ref[...] * _silu(z_ref[...])
    gw = d_inner // SSD_GROUPS
    nw = nw_ref[...]
    for g in range(SSD_GROUPS):
        seg = gated[:, g * gw:(g + 1) * gw]
        ms = jnp.mean(seg * seg, axis=-1, keepdims=True)
        out_ref[:, g * gw:(g + 1) * gw] = (seg * lax.rsqrt(ms + RMS_EPS) * nw[:, g * gw:(g + 1) * gw]).astype(BF16)


def _ssd_seq(zx, dt_raw, row0, bsz, seqlen, q, conv_buf, h0, cw, cb, dtb, alog, dsk, nw):
    n_heads, p_dim, n_state = h0.shape[1:]
    d_inner = n_heads * p_dim
    conv_dim = cw.shape[1]
    nt = seqlen // q
    blk0 = row0 // q
    full2 = lambda b, c: (0, 0)
    col = lambda j: (lambda b, c: (blk0 + b * nt + c, j))
    return pl.pallas_call(
        _ssd_seq_kernel,
        grid=(bsz, nt),
        in_specs=[
            pl.BlockSpec((q, d_inner), col(0)),
            pl.BlockSpec((q, d_inner), col(1)),
            pl.BlockSpec((q, d_inner), col(2)),
            pl.BlockSpec((q, n_heads), col(0)),
            pl.BlockSpec((1, CONV_WIDTH - 1, conv_dim), lambda b, c: (b, 0, 0)),
            pl.BlockSpec((1, n_heads, p_dim, n_state), lambda b, c: (b, 0, 0, 0)),
            pl.BlockSpec((CONV_WIDTH, conv_dim), full2),
            pl.BlockSpec((1, conv_dim), full2),
            pl.BlockSpec((1, n_heads), full2),
            pl.BlockSpec((1, n_heads), full2),
            pl.BlockSpec((1, n_heads), full2),
            pl.BlockSpec((1, d_inner), full2),
        ],
        out_specs=[
            pl.BlockSpec((q, d_inner), lambda b, c: (b * nt + c, 0)),
            pl.BlockSpec((1, CONV_WIDTH - 1, conv_dim), lambda b, c: (b, 0, 0)),
            pl.BlockSpec((1, n_heads, p_dim, n_state), lambda b, c: (b, 0, 0, 0)),
        ],
        out_shape=[
            jax.ShapeDtypeStruct((bsz * seqlen, d_inner), BF16),
            jax.ShapeDtypeStruct((bsz, CONV_WIDTH - 1, conv_dim), F32),
            jax.ShapeDtypeStruct((bsz, n_heads, p_dim, n_state), F32),
        ],
        scratch_shapes=[
            pltpu.VMEM((q + CONV_PAD, conv_dim), F32),
            pltpu.VMEM((q, d_inner), F32),
        ],
        compiler_params=_params("arbitrary", "arbitrary"),
        name="ssd_seq",
    )(zx, zx, zx, dt_raw, conv_buf, h0, cw, cb.reshape(1, -1), dtb.reshape(1, -1),
      alog.reshape(1, -1), dsk.reshape(1, -1), nw.reshape(1, -1))


def _first_argmax(v, rid, n):
    m = jnp.max(v, axis=0, keepdims=True)
    idx = jnp.min(jnp.where(v == m, rid, float(n)), axis=0, keepdims=True)
    return m, idx


def _route_tile(scores_t, bias_col):
    n_exp, tm = scores_t.shape
    per = n_exp // N_EXPERT_GROUPS
    neg = -jnp.inf
    biased = scores_t + bias_col
    rid_g = lax.broadcasted_iota(jnp.int32, (per, tm), 0).astype(F32)
    tiles, gs_rows = [], []
    for g in range(N_EXPERT_GROUPS):
        v = biased[g * per:(g + 1) * per, :]
        tiles.append(v)
        m1, i1 = _first_argmax(v, rid_g, per)
        m2 = jnp.max(jnp.where(rid_g == i1, neg, v), axis=0, keepdims=True)
        gs_rows.append(m1 + m2)
    gs = jnp.concatenate(gs_rows, axis=0)
    rid_grp = lax.broadcasted_iota(jnp.int32, (N_EXPERT_GROUPS, tm), 0).astype(F32)
    gsel = jnp.zeros((N_EXPERT_GROUPS, tm), F32)
    for _ in range(TOPK_GROUPS):
        _, gi = _first_argmax(gs, rid_grp, N_EXPERT_GROUPS)
        hit = rid_grp == gi
        gsel = jnp.where(hit, 1.0, gsel)
        gs = jnp.where(hit, neg, gs)
    masked = jnp.concatenate(
        [jnp.where(gsel[g:g + 1, :] > 0.0, tiles[g], neg) for g in range(N_EXPERT_GROUPS)], axis=0)
    rid = lax.broadcasted_iota(jnp.int32, (n_exp, tm), 0).astype(F32)
    sel = jnp.zeros((n_exp, tm), F32)
    ids, picked = [], []
    for _ in range(TOP_K):
        _, ei = _first_argmax(masked, rid, n_exp)
        hit = rid == ei
        ids.append(ei)
        picked.append(jnp.sum(jnp.where(hit, scores_t, 0.0), axis=0, keepdims=True))
        sel = jnp.where(hit, 1.0, sel)
        masked = jnp.where(hit, neg, masked)
    total = picked[0]
    for p in picked[1:]:
        total = total + p
    gates = [p / total * ROUTE_SCALE for p in picked]
    return sel, rid, ids, gates


def _proj_ln_router_kernel(alpha, h_ref, w_ref, x_ref, g_ref, b_ref, rwt_ref, rb_ref,
                           o_ref, eid_ref, gate_ref, rank_ref, cnt_ref, wb_ref, tri_ref):
    i = pl.program_id(0)
    tm = x_ref.shape[0]

    @pl.when(i == 0)
    def _():
        wb_ref[...] = w_ref[...].astype(BF16)
        r = lax.broadcasted_iota(jnp.int32, (tm, tm), 0)
        c = lax.broadcasted_iota(jnp.int32, (tm, tm), 1)
        tri_ref[...] = jnp.where(r < c, 1.0, 0.0).astype(BF16)
        cnt_ref[...] = jnp.zeros_like(cnt_ref)

    mix = jnp.dot(h_ref[...], wb_ref[...], preferred_element_type=F32)
    x1 = _layer_norm(alpha * x_ref[...] + mix, g_ref[...], b_ref[...])
    o_ref[...] = x1
    logits_t = lax.dot_general(rwt_ref[...].astype(BF16), x1.astype(BF16), NT_DIMS,
                               preferred_element_type=F32)
    sel, rid, ids, gates = _route_tile(_sigmoid(logits_t), rb_ref[...])

    sel_b = sel.astype(BF16)
    before = jnp.dot(sel_b, tri_ref[...], preferred_element_type=F32) + cnt_ref[:, 0:1]
    pad_rows = eid_ref.shape[0] - TOP_K
    zrow = jnp.zeros((pad_rows, tm), F32)
    ranks = [jnp.sum(jnp.where(rid == ei, before, 0.0), axis=0, keepdims=True) for ei in ids]
    eid_ref[...] = jnp.concatenate(ids + [zrow], axis=0).astype(jnp.int32)
    gate_ref[...] = jnp.concatenate(gates + [zrow], axis=0)
    rank_ref[...] = jnp.concatenate(ranks + [zrow], axis=0).astype(jnp.int32)
    ones = jnp.ones((tm, cnt_ref.shape[1]), BF16)
    cnt_ref[...] = cnt_ref[...] + jnp.dot(sel_b, ones, preferred_element_type=F32)


def _proj_ln_router(h, w, wl, x, alpha, ln_g, ln_b, router_w_t, router_bias_col):
    T, K = h.shape
    D = x.shape[1]
    E = router_w_t.shape[0]
    tm = TM_PROJ
    kp = SUBLANE
    row_spec = pl.BlockSpec((kp, tm), lambda i: (0, i))
    return pl.pallas_call(
        functools.partial(_proj_ln_router_kernel, alpha),
        grid=(T // tm,),
        in_specs=[
            pl.BlockSpec((tm, K), lambda i: (i, 0)),
            pl.BlockSpec((None, K, D), lambda i: (wl, 0, 0)),
            pl.BlockSpec((tm, D), lambda i: (i, 0)),
            pl.BlockSpec((1, D), lambda i: (0, 0)),
            pl.BlockSpec((1, D), lambda i: (0, 0)),
            pl.BlockSpec((E, D), lambda i: (0, 0)),
            pl.BlockSpec((E, 1), lambda i: (0, 0)),
        ],
        out_specs=[pl.BlockSpec((tm, D), lambda i: (i, 0)), row_spec, row_spec, row_spec,
                   pl.BlockSpec((E, LANE), lambda i: (0, 0))],
        out_shape=[jax.ShapeDtypeStruct((T, D), F32),
                   jax.ShapeDtypeStruct((kp, T), jnp.int32),
                   jax.ShapeDtypeStruct((kp, T), F32),
                   jax.ShapeDtypeStruct((kp, T), jnp.int32),
                   jax.ShapeDtypeStruct((E, LANE), F32)],
        scratch_shapes=[pltpu.VMEM((K, D), BF16), pltpu.VMEM((tm, tm), BF16)],
        compiler_params=_params("arbitrary"),
        name="proj_ln_router",
    )(h, w, x, ln_g.reshape(1, D), ln_b.reshape(1, D), router_w_t, router_bias_col)


def _start_row_gather(idx_ref, base, n, src_hbm, dst, sem):
    for i in range(n):
        r = idx_ref[base + i]
        pltpu.make_async_copy(src_hbm.at[pl.ds(r, 1), :], dst.at[pl.ds(i, 1), :], sem).start(priority=i % 2)


def _wait_row_gather(n, src_hbm, dst, sem):
    pltpu.make_async_copy(src_hbm.at[pl.ds(0, n), :], dst, sem).wait()


def _scatter_kernel(n_tiles, pos_ref, zb_ref, nu_ref, x_ref, xs_hbm, xbuf, zbuf, sem, zsem):
    i = pl.program_id(0)
    tm = x_ref.shape[0]
    n_rows = TOP_K * tm
    bm = zbuf.shape[0]
    n_experts = zb_ref.shape[0]
    n_blocks = xs_hbm.shape[0] // bm
    slot = lax.rem(i, 2)

    def zero_block(blk):
        start = pl.multiple_of(blk * bm, bm)
        return pltpu.make_async_copy(zbuf, xs_hbm.at[pl.ds(start, bm), :], zsem)

    def wait_rows(s):
        pltpu.make_async_copy(xs_hbm.at[pl.ds(0, n_rows), :], xs_hbm.at[pl.ds(0, n_rows), :], sem.at[s]).wait()

    @pl.when(i == 0)
    def _():
        zbuf[...] = jnp.zeros_like(zbuf)
        for e in range(n_experts):
            @pl.when(zb_ref[e] >= 0)
            def _():
                zero_block(zb_ref[e]).start()

        def start_tail(blk, carry):
            zero_block(blk).start()
            return carry

        def wait_tail(blk, carry):
            zero_block(blk).wait()
            return carry

        lax.fori_loop(nu_ref[0], n_blocks, start_tail, 0)
        for e in range(n_experts):
            @pl.when(zb_ref[e] >= 0)
            def _():
                zero_block(zb_ref[e]).wait()
        lax.fori_loop(nu_ref[0], n_blocks, wait_tail, 0)

    @pl.when(i >= 2)
    def _():
        wait_rows(slot)

    xbuf[slot] = x_ref[...]
    for k in range(TOP_K):
        for t in range(tm):
            r = pos_ref[i * n_rows + k * tm + t]
            pltpu.make_async_copy(xbuf.at[slot, pl.ds(t, 1), :], xs_hbm.at[pl.ds(r, 1), :],
                                  sem.at[slot]).start(priority=t % 2)

    @pl.when(i == n_tiles - 1)
    def _():
        wait_rows(slot)
        if n_tiles >= 2:
            wait_rows(1 - slot)


def _scatter_rows(x, pos_tiles, zero_blk, n_used, n_slots):
    T, D = x.shape
    tm = TM_COMBINE
    n_tiles = T // tm
    grid_spec = pltpu.PrefetchScalarGridSpec(
        num_scalar_prefetch=3,
        grid=(n_tiles,),
        in_specs=[pl.BlockSpec((tm, D), lambda i, pos, zb, nu: (i, 0))],
        out_specs=pl.BlockSpec(memory_space=pl.ANY),
        scratch_shapes=[
            pltpu.VMEM((2, tm, D), F32),
            pltpu.VMEM((BM_EXPERT, D), F32),
            pltpu.SemaphoreType.DMA((2,)),
            pltpu.SemaphoreType.DMA,
        ],
    )
    return pl.pallas_call(
        functools.partial(_scatter_kernel, n_tiles),
        grid_spec=grid_spec,
        out_shape=jax.ShapeDtypeStruct((n_slots, D), F32),
        compiler_params=_params("arbitrary"),
        name="scatter_rows",
    )(pos_tiles, zero_blk, n_used, x)


def _expert_kernel(be_ref, nu_ref, x_ref, wg_ref, wu_ref, wd_ref, y_ref, wgb, wub, wdb):
    b = pl.program_id(0)
    n_used = nu_ref[0]

    @pl.when(b < n_used)
    def _():
        new_expert = jnp.logical_or(b == 0, be_ref[b] != be_ref[jnp.maximum(b - 1, 0)])

        @pl.when(new_expert)
        def _():
            wgb[...] = wg_ref[...].astype(BF16)
            wub[...] = wu_ref[...].astype(BF16)
            wdb[...] = wd_ref[...].astype(BF16)

        xb = x_ref[...].astype(BF16)
        hg = jnp.dot(xb, wgb[...], preferred_element_type=F32)
        hu = jnp.dot(xb, wub[...], preferred_element_type=F32)
        hh = (_silu(hg) * hu).astype(BF16)
        y_ref[...] = jnp.dot(hh, wdb[...], preferred_element_type=F32)

    @pl.when(b >= n_used)
    def _():
        y_ref[...] = jnp.zeros_like(y_ref)


def _experts(x_sorted, block_e, n_used, wg, wu, wd, layer):
    n_slots, D = x_sorted.shape
    DE = wg.shape[3]
    bm = BM_EXPERT
    nb = n_slots // bm
    w_map = lambda b, be, nu: (layer, be[b], 0, 0)
    grid_spec = pltpu.PrefetchScalarGridSpec(
        num_scalar_prefetch=2,
        grid=(nb,),
        in_specs=[
            pl.BlockSpec((bm, D), lambda b, be, nu: (jnp.minimum(b, nu[0] - 1), 0)),
            pl.BlockSpec((None, None, D, DE), w_map),
            pl.BlockSpec((None, None, D, DE), w_map),
            pl.BlockSpec((None, None, DE, D), w_map),
        ],
        out_specs=pl.BlockSpec((bm, D), lambda b, be, nu: (b, 0)),
        scratch_shapes=[
            pltpu.VMEM((D, DE), BF16),
            pltpu.VMEM((D, DE), BF16),
            pltpu.VMEM((DE, D), BF16),
        ],
    )
    return pl.pallas_call(
        _expert_kernel,
        grid_spec=grid_spec,
        out_shape=jax.ShapeDtypeStruct((n_slots, D), F32),
        compiler_params=_params("arbitrary"),
        name="experts",
    )(block_e, n_used, x_sorted, wg, wu, wd)


def _combine_kernel(alpha, pos_ref, gates_ref, x_ref, y_hbm, sg_ref, su_ref, sd_ref, g_ref, b_ref,
                    o_ref, ybuf, sgb, sub, sdb, sem):
    i = pl.program_id(0)
    tm = x_ref.shape[0]
    n_rows = ybuf.shape[1]
    slot = lax.rem(i, 2)

    @pl.when(i == 0)
    def _():
        _start_row_gather(pos_ref, 0, n_rows, y_hbm, ybuf.at[0], sem.at[0])
        sgb[...] = sg_ref[...].astype(BF16)
        sub[...] = su_ref[...].astype(BF16)
        sdb[...] = sd_ref[...].astype(BF16)

    @pl.when(i + 1 < pl.num_programs(0))
    def _():
        _start_row_gather(pos_ref, (i + 1) * n_rows, n_rows, y_hbm, ybuf.at[1 - slot], sem.at[1 - slot])

    x = x_ref[...]
    xb = x.astype(BF16)
    hg = jnp.dot(xb, sgb[...], preferred_element_type=F32)
    hu = jnp.dot(xb, sub[...], preferred_element_type=F32)
    acc = jnp.dot((_silu(hg) * hu).astype(BF16), sdb[...], preferred_element_type=F32)
    _wait_row_gather(n_rows, y_hbm, ybuf.at[slot], sem.at[slot])
    gates = gates_ref[...]
    for k in range(TOP_K):
        acc = acc + gates[:, k:k + 1] * ybuf[slot, k * tm:(k + 1) * tm, :]
    o_ref[...] = _layer_norm(alpha * x + acc, g_ref[...], b_ref[...])


def _combine(x, y_sorted, pos_tiles, gates, sg, su, sd, alpha, ln_g, ln_b, layer):
    T, D = x.shape
    DS = sg.shape[2]
    tm = TM_COMBINE
    grid_spec = pltpu.PrefetchScalarGridSpec(
        num_scalar_prefetch=1,
        grid=(T // tm,),
        in_specs=[
            pl.BlockSpec((tm, TOP_K), lambda i, pos: (i, 0)),
            pl.BlockSpec((tm, D), lambda i, pos: (i, 0)),
            pl.BlockSpec(memory_space=pl.ANY),
            pl.BlockSpec((None, D, DS), lambda i, pos: (layer, 0, 0)),
            pl.BlockSpec((None, D, DS), lambda i, pos: (layer, 0, 0)),
            pl.BlockSpec((None, DS, D), lambda i, pos: (layer, 0, 0)),
            pl.BlockSpec((None, 1, D), lambda i, pos: (layer, 0, 0)),
            pl.BlockSpec((None, 1, D), lambda i, pos: (layer, 0, 0)),
        ],
        out_specs=pl.BlockSpec((tm, D), lambda i, pos: (i, 0)),
        scratch_shapes=[
            pltpu.VMEM((2, TOP_K * tm, D), F32),
            pltpu.VMEM((D, DS), BF16),
            pltpu.VMEM((D, DS), BF16),
            pltpu.VMEM((DS, D), BF16),
            pltpu.SemaphoreType.DMA((2,)),
        ],
    )
    return pl.pallas_call(
        functools.partial(_combine_kernel, alpha),
        grid_spec=grid_spec,
        out_shape=jax.ShapeDtypeStruct((T, D), F32),
        compiler_params=_params("arbitrary"),
        name="combine",
    )(pos_tiles, gates, x, y_sorted, sg, su, sd, ln_g.reshape(-1, 1, D), ln_b.reshape(-1, 1, D))


def _dispatch(eid_t, rank_t, counts):
    T = eid_t.shape[1]
    E = counts.shape[0]
    A = T * TOP_K
    bm = BM_EXPERT
    nb = -(-A // bm) + E
    cnt = counts[:, 0].astype(jnp.int32)
    padded = (cnt + bm - 1) // bm * bm
    pends = jnp.cumsum(padded)
    pstart = pends - padded
    eid = eid_t[:TOP_K]
    onehot = eid[:, :, None] == jnp.arange(E, dtype=jnp.int32)[None, None, :]
    dest = rank_t[:TOP_K] + jnp.sum(jnp.where(onehot, pstart[None, None, :], 0), axis=-1)
    block_start = jnp.arange(nb, dtype=jnp.int32) * bm
    block_e = jnp.minimum(jnp.sum((pends[None, :] <= block_start[:, None]).astype(jnp.int32), axis=1), E - 1)
    n_used = (pends[-1:] // bm).astype(jnp.int32)
    zero_blk = jnp.where(padded > 0, pends // bm - 1, -1).astype(jnp.int32)
    tm = TM_COMBINE
    pos_tiles = dest.reshape(TOP_K, T // tm, tm).transpose(1, 0, 2).reshape(-1)
    return nb * bm, block_e, n_used, zero_blk, pos_tiles


def _moe(x1, eid_t, gate_t, rank_t, counts, alpha, layer, wg, wu, wd, sg, su, sd, ln_g, ln_b):
    n_slots, block_e, n_used, zero_blk, pos_tiles = _dispatch(eid_t, rank_t, counts)
    x_sorted = _scatter_rows(x1, pos_tiles, zero_blk, n_used, n_slots)
    y_sorted = _experts(x_sorted, block_e, n_used, wg, wu, wd, layer)
    gates = gate_t[:TOP_K].T
    return _combine(x1, y_sorted, pos_tiles, gates, sg, su, sd, alpha, ln_g, ln_b, layer)


def kernel(x_prompt, x_sample, state_lru_conv, state_lru_h, state_ssd_conv, state_ssd, lru_w_in, lru_conv_w, lru_conv_b, lru_w_rgate, lru_b_rgate, lru_w_igate, lru_b_igate, lru_lambda, lru_w_out, ssd_w_in, ssd_conv_w, ssd_conv_b, ssd_dt_bias, ssd_a_log, ssd_d, ssd_norm_w, ssd_w_out, ln_mix_g, ln_mix_b, ln_ffn_g, ln_ffn_b, router_w, router_bias, moe_w_gate, moe_w_up, moe_w_down, shared_w_gate, shared_w_up, shared_w_down):
    bp, lp, d_model = x_prompt.shape
    bs, ls, _ = x_sample.shape
    depth = ln_mix_g.shape[0]
    alpha = (2.0 * depth) ** 0.25
    tp, ts = bp * lp, bs * ls
    n_heads, p_dim, n_state = state_ssd.shape[2:]
    d_inner = n_heads * p_dim
    conv_dim = ssd_conv_w.shape[2]
    d_rnn = lru_conv_w.shape[2]

    x = jnp.concatenate([x_prompt.reshape(tp, d_model), x_sample.reshape(ts, d_model)], axis=0)
    tl_p = min(TL_LRU, lp)
    tl_s = min(TL_LRU, ls)
    q_p = min(SSD_CHUNK, lp)
    q_s = min(SSD_CHUNK, ls)

    p_lru_conv, p_lru_h, p_ssd_conv, p_ssd = [], [], [], []
    s_lru_conv, s_lru_h, s_ssd_conv, s_ssd = [], [], [], []
    for i in range(depth):
        j = i // 2
        if i % 2 == 0:
            gx = _in_proj(x, lru_w_in, j, 2 * d_rnn, TN_LRU_IN)
            wrg = _block_diag_gate(lru_w_rgate[j])
            wig = _block_diag_gate(lru_w_igate[j])
            common = (lru_conv_w[j], lru_conv_b[j], wrg, lru_b_rgate[j], wig, lru_b_igate[j], lru_lambda[j])
            hp, cp, lp_h = _lru_seq(gx, 0, bp, lp, tl_p, jnp.zeros((bp, CONV_WIDTH - 1, d_rnn), F32),
                                    jnp.zeros((bp, d_rnn), F32), *common)
            hs, cs_, ls_h = _lru_seq(gx, tp, bs, ls, tl_s, state_lru_conv[j], state_lru_h[j], *common)
            p_lru_conv.append(cp)
            p_lru_h.append(lp_h.reshape(bp, d_rnn))
            s_lru_conv.append(cs_)
            s_lru_h.append(ls_h.reshape(bs, d_rnn))
            mixed = jnp.concatenate([hp, hs], axis=0)
            w_out, wl = lru_w_out, j
        else:
            zx = _in_proj(x, ssd_w_in, j, d_inner + conv_dim, TN_SSD_IN)
            dt_raw = _small_proj(x, ssd_w_in[j][:, d_inner + conv_dim:])
            common = (ssd_conv_w[j], ssd_conv_b[j], ssd_dt_bias[j], ssd_a_log[j], ssd_d[j], ssd_norm_w[j])
            hp, cp, sp = _ssd_seq(zx, dt_raw, 0, bp, lp, q_p, jnp.zeros((bp, CONV_WIDTH - 1, conv_dim), F32),
                                  jnp.zeros((bp, n_heads, p_dim, n_state), F32), *common)
            hs, cs_, ss = _ssd_seq(zx, dt_raw, tp, bs, ls, q_s, state_ssd_conv[j], state_ssd[j], *common)
            p_ssd_conv.append(cp)
            p_ssd.append(sp)
            s_ssd_conv.append(cs_)
            s_ssd.append(ss)
            mixed = jnp.concatenate([hp, hs], axis=0)
            w_out, wl = ssd_w_out, j
        x1, eid_t, gate_t, rank_t, counts = _proj_ln_router(
            mixed, w_out, wl, x, alpha, ln_mix_g[i], ln_mix_b[i], router_w[i].T, router_bias[i].reshape(-1, 1))
        x = _moe(x1, eid_t, gate_t, rank_t, counts, alpha, i, moe_w_gate, moe_w_up, moe_w_down,
                 shared_w_gate, shared_w_up, shared_w_down, ln_ffn_g, ln_ffn_b)

    y_prompt = x[:tp].reshape(bp, lp, d_model)
    y_sample = x[tp:].reshape(bs, ls, d_model)
    return (y_prompt, y_sample,
            jnp.stack(p_lru_conv), jnp.stack(p_lru_h), jnp.stack(p_ssd_conv), jnp.stack(p_ssd),
            jnp.stack(s_lru_conv), jnp.stack(s_lru_h), jnp.stack(s_ssd_conv), jnp.stack(s_ssd))
```

```python
import functools
import math

import jax
import jax.numpy as jnp
from jax import lax
from jax.experimental import pallas as pl
from jax.experimental.pallas import tpu as pltpu

F32 = jnp.float32
BF16 = jnp.bfloat16

LN_EPS = 1e-5
RMS_EPS = 1e-5
CONV_WIDTH = 4
LRU_C = 8.0
SSD_CHUNK = 64
SSD_GROUPS = 8
TOP_K = 6
N_EXPERT_GROUPS = 8
TOPK_GROUPS = 4
ROUTE_SCALE = 1.0

LANE = 128
SUBLANE = 8
MXU_DIM = 256
VMEM_LIMIT = 56 * 1024 * 1024

TM_PROJ = 512
TM_COMBINE = 128
BM_EXPERT = 256
TL_LRU = 256
TN_LRU_IN = 1024
TN_SSD_IN = 1536

HIGHEST = lax.Precision.HIGHEST
NT_DIMS = (((1,), (1,)), ((), ()))
TN_DIMS = (((0,), (0,)), ((), ()))


def _sigmoid(x):
    return 1.0 / (1.0 + jnp.exp(-x))


def _silu(x):
    return x * _sigmoid(x)


def _softplus(x):
    return jnp.maximum(x, 0.0) + jnp.log1p(jnp.exp(-jnp.abs(x)))


def _gelu_tanh(x):
    c = math.sqrt(2.0 / math.pi)
    return 0.5 * x * (1.0 + jnp.tanh(c * (x + 0.044715 * (x * x * x))))


def _layer_norm(v, g, b):
    mu = jnp.mean(v, axis=-1, keepdims=True)
    d = v - mu
    var = jnp.mean(d * d, axis=-1, keepdims=True)
    return d * lax.rsqrt(var + LN_EPS) * g + b


def _params(*sem):
    return pltpu.CompilerParams(dimension_semantics=sem, vmem_limit_bytes=VMEM_LIMIT)


def _mm_kernel(x_ref, w_ref, o_ref):
    o_ref[...] = jnp.dot(x_ref[...].astype(BF16), w_ref[...].astype(BF16),
                         preferred_element_type=F32)


def _in_proj_kernel(x_ref, w_ref, o_ref, wb_ref):
    @pl.when(pl.program_id(1) == 0)
    def _():
        wb_ref[...] = w_ref[...].astype(BF16)

    o_ref[...] = jnp.dot(x_ref[...].astype(BF16), wb_ref[...], preferred_element_type=F32)


def _in_proj(x, w, layer, n_cols, tn):
    T, K = x.shape
    tm = TM_PROJ
    return pl.pallas_call(
        _in_proj_kernel,
        grid=(n_cols // tn, T // tm),
        in_specs=[pl.BlockSpec((tm, K), lambda j, i: (i, 0)),
                  pl.BlockSpec((None, K, tn), lambda j, i: (layer, 0, j))],
        out_specs=pl.BlockSpec((tm, tn), lambda j, i: (i, j)),
        out_shape=jax.ShapeDtypeStruct((T, n_cols), F32),
        scratch_shapes=[pltpu.VMEM((K, tn), BF16)],
        compiler_params=_params("arbitrary", "arbitrary"),
        name="in_proj",
    )(x, w)


def _small_proj(x, w):
    T, K = x.shape
    N = w.shape[1]
    tm = TM_PROJ
    return pl.pallas_call(
        _mm_kernel,
        grid=(T // tm,),
        in_specs=[pl.BlockSpec((tm, K), lambda i: (i, 0)),
                  pl.BlockSpec((K, N), lambda i: (0, 0))],
        out_specs=pl.BlockSpec((tm, N), lambda i: (i, 0)),
        out_shape=jax.ShapeDtypeStruct((T, N), F32),
        compiler_params=_params("parallel"),
        name="dt_proj",
    )(x, w)


CONV_PAD = SUBLANE
CONV_HIST = CONV_PAD - (CONV_WIDTH - 1)


def _conv_tile(xx_ref, n, cw, cb):
    acc = cb + cw[CONV_WIDTH - 1:CONV_WIDTH] * xx_ref[CONV_PAD:CONV_PAD + n, :]
    for k in range(CONV_WIDTH - 1):
        off = CONV_HIST + k
        acc = acc + cw[k:k + 1] * xx_ref[off:off + n, :]
    return acc


def _lru_seq_kernel(gate_ref, xr_ref, cbuf_ref, h0_ref, cw_ref, cb_ref, wrg_ref, brg_ref,
                    wig_ref, big_ref, lam_ref, out_ref, nconv_ref, hlast_ref,
                    xx_ref, h_ref, a_ref, u_ref):
    c = pl.program_id(1)
    n = xr_ref.shape[0]
    d = xr_ref.shape[1]

    @pl.when(c == 0)
    def _():
        xx_ref[CONV_HIST:CONV_PAD, :] = cbuf_ref[0]
        h_ref[...] = h0_ref[0]

    xx_ref[CONV_PAD:CONV_PAD + n, :] = xr_ref[...]
    xr = _conv_tile(xx_ref, n, cw_ref[...], cb_ref[...])
    tail = xx_ref[n + CONV_HIST:n + CONV_PAD, :]
    xx_ref[CONV_HIST:CONV_PAD, :] = tail

    xb = xr.astype(BF16)
    nblk = d // MXU_DIM
    r_parts, i_parts = [], []
    for g in range(nblk):
        seg = xb[:, g * MXU_DIM:(g + 1) * MXU_DIM]
        r_parts.append(jnp.dot(seg, wrg_ref[g], preferred_element_type=F32))
        i_parts.append(jnp.dot(seg, wig_ref[g], preferred_element_type=F32))
    r = _sigmoid(jnp.concatenate(r_parts, axis=-1) + brg_ref[...])
    ig = _sigmoid(jnp.concatenate(i_parts, axis=-1) + big_ref[...])
    log_sig_lam = -_softplus(-lam_ref[...])
    log_a = LRU_C * r * log_sig_lam
    a = jnp.exp(log_a)
    a_ref[...] = a
    u_ref[...] = jnp.sqrt(-jnp.tanh(log_a) * (a * a + 1.0)) * (ig * xr)

    def step(t, h):
        h = a_ref[pl.ds(t, 1), :] * h + u_ref[pl.ds(t, 1), :]
        u_ref[pl.ds(t, 1), :] = h
        return h

    h = lax.fori_loop(0, n, step, h_ref[...], unroll=8)
    h_ref[...] = h
    out_ref[...] = (u_ref[...] * _gelu_tanh(gate_ref[...])).astype(BF16)

    @pl.when(c == pl.num_programs(1) - 1)
    def _():
        nconv_ref[0] = tail
        hlast_ref[0] = h


def _lru_seq(gx, row0, bsz, seqlen, tl, conv_buf, h0, cw, cb, wrg, brg, wig, big, lam):
    d = cw.shape[1]
    nt = seqlen // tl
    blk0 = row0 // tl
    row_map = lambda b, c: (blk0 + b * nt + c, 0)
    vec = lambda a: a.reshape(1, d)
    full2 = lambda b, c: (0, 0)
    full3 = lambda b, c: (0, 0, 0)
    return pl.pallas_call(
        _lru_seq_kernel,
        grid=(bsz, nt),
        in_specs=[
            pl.BlockSpec((tl, d), row_map),
            pl.BlockSpec((tl, d), lambda b, c: (blk0 + b * nt + c, 1)),
            pl.BlockSpec((1, CONV_WIDTH - 1, d), lambda b, c: (b, 0, 0)),
            pl.BlockSpec((1, 1, d), lambda b, c: (b, 0, 0)),
            pl.BlockSpec((CONV_WIDTH, d), full2),
            pl.BlockSpec((1, d), full2),
            pl.BlockSpec(wrg.shape, full3),
            pl.BlockSpec((1, d), full2),
            pl.BlockSpec(wig.shape, full3),
            pl.BlockSpec((1, d), full2),
            pl.BlockSpec((1, d), full2),
        ],
        out_specs=[
            pl.BlockSpec((tl, d), lambda b, c: (b * nt + c, 0)),
            pl.BlockSpec((1, CONV_WIDTH - 1, d), lambda b, c: (b, 0, 0)),
            pl.BlockSpec((1, 1, d), lambda b, c: (b, 0, 0)),
        ],
        out_shape=[
            jax.ShapeDtypeStruct((bsz * seqlen, d), BF16),
            jax.ShapeDtypeStruct((bsz, CONV_WIDTH - 1, d), F32),
            jax.ShapeDtypeStruct((bsz, 1, d), F32),
        ],
        scratch_shapes=[
            pltpu.VMEM((tl + CONV_PAD, d), F32),
            pltpu.VMEM((1, d), F32),
            pltpu.VMEM((tl, d), F32),
            pltpu.VMEM((tl, d), F32),
        ],
        compiler_params=_params("arbitrary", "arbitrary"),
        name="lru_seq",
    )(gx, gx, conv_buf, h0.reshape(bsz, 1, d), cw, vec(cb), wrg, vec(brg), wig, vec(big), vec(lam))


def _block_diag_gate(w):
    nb, bw, _ = w.shape
    per = MXU_DIM // bw
    w4 = w.reshape(nb // per, per, bw, bw)
    eye = jnp.eye(per, dtype=w.dtype)
    out = w4[:, :, :, None, :] * eye[None, :, None, :, None]
    return out.reshape(nb // per, MXU_DIM, MXU_DIM).astype(BF16)


def _dot_sel(sel, x, sel_on_left=False, dims=None):
    p1 = x.astype(BF16)
    r1 = x - p1.astype(F32)
    p2 = r1.astype(BF16)
    p3 = (r1 - p2.astype(F32)).astype(BF16)
    out = None
    for p in (p1, p2, p3):
        a, b = (sel, p) if sel_on_left else (p, sel)
        if dims is None:
            term = jnp.dot(a, b, preferred_element_type=F32)
        else:
            term = lax.dot_general(a, b, dims, preferred_element_type=F32)
        out = term if out is None else out + term
    return out


def _ssd_seq_kernel(has_eq, z_ref, xs_ref, bc_ref, dt_ref, cbuf_ref, h0_ref, cw_ref, cb_ref, dtb_ref,
                    alog_ref, dskx_ref, nw_ref, ex_ref, *rest):
    eq_ref = rest[0] if has_eq else None
    out_ref, nconv_ref, st_ref, xx_ref = rest[-4:]
    c = pl.program_id(1)
    q = xs_ref.shape[0]
    d_inner = xs_ref.shape[1]
    n_heads = st_ref.shape[1]
    p_dim = st_ref.shape[2]
    n_state = st_ref.shape[3]
    hpg = n_heads // SSD_GROUPS

    @pl.when(c == 0)
    def _():
        xx_ref[CONV_HIST:CONV_PAD, :] = cbuf_ref[0]
        st_ref[...] = h0_ref[...]

    xx_ref[CONV_PAD:CONV_PAD + q, 0:d_inner] = xs_ref[...]
    xx_ref[CONV_PAD:CONV_PAD + q, d_inner:] = bc_ref[...]
    act = _silu(_conv_tile(xx_ref, q, cw_ref[...], cb_ref[...]))
    tail = xx_ref[q + CONV_HIST:q + CONV_PAD, :]
    xx_ref[CONV_HIST:CONV_PAD, :] = tail

    @pl.when(c == pl.num_programs(1) - 1)
    def _():
        nconv_ref[0] = tail

    dt = _softplus(dt_ref[...] + dtb_ref[...])
    adt = dt * (-jnp.exp(alog_ref[...]))
    row = lax.broadcasted_iota(jnp.int32, (q, q), 0)
    col = lax.broadcasted_iota(jnp.int32, (q, q), 1)
    tril = (col <= row).astype(BF16)
    cs = _dot_sel(tril, adt, sel_on_left=True)
    both_x = _dot_sel(ex_ref[...], jnp.concatenate([dt, cs], axis=0))
    dt_x = both_x[:q]
    cs_x = both_x[q:]
    if eq_ref is None:
        cs_q = cs_x
    else:
        cs_q = _dot_sel(eq_ref[...], cs)

    xs = act[:, :d_inner]
    xd = xs * dt_x
    xd_b = xd.astype(BF16)
    xdec_b = (xd * jnp.exp(cs_x[q - 1:q, :] - cs_x)).astype(BF16)
    ecs_x = jnp.exp(cs_x)

    head = lax.broadcasted_iota(jnp.int32, (q, n_heads), 1)
    even = (head & 1) == 0
    cs_par = jnp.concatenate([jnp.where(even, cs, 0.0), jnp.where(even, 0.0, cs)], axis=0)
    pair_sel = ((lax.broadcasted_iota(jnp.int32, (n_heads // 2, n_heads), 1) >> 1)
                == lax.broadcasted_iota(jnp.int32, (n_heads // 2, n_heads), 0)).astype(BF16)
    rtab = _dot_sel(pair_sel, cs_par, sel_on_left=True, dims=NT_DIMS)
    eye_h = (lax.broadcasted_iota(jnp.int32, (n_heads, n_heads), 0)
             == lax.broadcasted_iota(jnp.int32, (n_heads, n_heads), 1)).astype(BF16)
    cs_t = _dot_sel(eye_h, cs, sel_on_left=True, dims=NT_DIMS)
    st_decay = jnp.exp(jnp.broadcast_to(cs_t[:, q - 1:q], (n_heads, n_state)))

    row2 = lax.broadcasted_iota(jnp.int32, (q, 2 * q), 0)
    col2 = lax.broadcasted_iota(jnp.int32, (q, 2 * q), 1)
    causal2 = jnp.where(col2 >= q, col2 - q, col2) <= row2
    lane_p = lax.broadcasted_iota(jnp.int32, (q, 2 * p_dim), 1)
    lo_half = lane_p < p_dim
    zero_b = jnp.zeros((q, 2 * p_dim), BF16)
    b_off = d_inner
    c_off = d_inner + SSD_GROUPS * n_state
    gp = hpg * p_dim
    pairs_per_group = hpg // 2

    y_diag_parts, y_off_parts = [], []
    for g in range(SSD_GROUPS):
        bg = act[:, b_off + g * n_state:b_off + (g + 1) * n_state].astype(BF16)
        cg = act[:, c_off + g * n_state:c_off + (g + 1) * n_state].astype(BF16)
        cb2 = lax.dot_general(cg, jnp.concatenate([bg, bg], axis=0), NT_DIMS,
                              preferred_element_type=F32)
        prev = st_ref[0, g * hpg:(g + 1) * hpg].reshape(gp, n_state)
        y_off_parts.append(lax.dot_general(cg, prev.astype(BF16), NT_DIMS, preferred_element_type=F32))
        for jj in range(pairs_per_group):
            j = g * pairs_per_group + jj
            lmat = jnp.exp(jnp.where(causal2, cs_q[:, 2 * j * q:2 * (j + 1) * q] - rtab[j:j + 1, :], -jnp.inf))
            m = (cb2 * lmat).astype(BF16)
            pp = xd_b[:, 2 * j * p_dim:2 * (j + 1) * p_dim]
            blockdiag = jnp.concatenate([jnp.where(lo_half, pp, zero_b), jnp.where(lo_half, zero_b, pp)], axis=0)
            y_diag_parts.append(jnp.dot(m, blockdiag, preferred_element_type=F32))
        st_new = lax.dot_general(xdec_b[:, g * gp:(g + 1) * gp], bg, TN_DIMS,
                                 preferred_element_type=F32)
        for e in range(hpg):
            h = g * hpg + e
            st_ref[0, h] = (st_decay[h:h + 1, :] * prev[e * p_dim:(e + 1) * p_dim, :]
                            + st_new[e * p_dim:(e + 1) * p_dim, :])

    y = (jnp.concatenate(y_diag_parts, axis=-1) + jnp.concatenate(y_off_parts, axis=-1) * ecs_x
         + dskx_ref[...] * xs)
    gated = y * _silu(z_ref[...])
    gw = d_inner // SSD_GROUPS
    nw = nw_ref[...]
    for g in range(SSD_GROUPS):
        seg = gated[:, g * gw:(g + 1) * gw]
        ms = jnp.mean(seg * seg, axis=-1, keepdims=True)
        out_ref[:, g * gw:(g + 1) * gw] = (seg * lax.rsqrt(ms + RMS_EPS) * nw[:, g * gw:(g + 1) * gw]).astype(BF16)


def _ssd_seq(zx, dt_raw, row0, bsz, seqlen, q, conv_buf, h0, cw, cb, dtb, alog, dsk, nw):
    n_heads, p_dim, n_state = h0.shape[1:]
    d_inner = n_heads * p_dim
    conv_dim = cw.shape[1]
    nt = seqlen // q
    blk0 = row0 // q
    full2 = lambda b, c: (0, 0)
    col = lambda j: (lambda b, c: (blk0 + b * nt + c, j))
    head_ids = jnp.arange(n_heads, dtype=jnp.int32)[:, None]
    expand = lambda width: (jnp.arange(n_heads * width, dtype=jnp.int32)[None, :] // width == head_ids).astype(BF16)
    has_eq = q != p_dim
    extra_in = [expand(q)] if has_eq else []
    extra_specs = [pl.BlockSpec((n_heads, n_heads * q), full2)] if has_eq else []
    return pl.pallas_call(
        functools.partial(_ssd_seq_kernel, has_eq),
        grid=(bsz, nt),
        in_specs=[
            pl.BlockSpec((q, d_inner), col(0)),
            pl.BlockSpec((q, d_inner), col(1)),
            pl.BlockSpec((q, d_inner), col(2)),
            pl.BlockSpec((q, n_heads), col(0)),
            pl.BlockSpec((1, CONV_WIDTH - 1, conv_dim), lambda b, c: (b, 0, 0)),
            pl.BlockSpec((1, n_heads, p_dim, n_state), lambda b, c: (b, 0, 0, 0)),
            pl.BlockSpec((CONV_WIDTH, conv_dim), full2),
            pl.BlockSpec((1, conv_dim), full2),
            pl.BlockSpec((1, n_heads), full2),
            pl.BlockSpec((1, n_heads), full2),
            pl.BlockSpec((1, d_inner), full2),
            pl.BlockSpec((1, d_inner), full2),
            pl.BlockSpec((n_heads, d_inner), full2),
        ] + extra_specs,
        out_specs=[
            pl.BlockSpec((q, d_inner), lambda b, c: (b * nt + c, 0)),
            pl.BlockSpec((1, CONV_WIDTH - 1, conv_dim), lambda b, c: (b, 0, 0)),
            pl.BlockSpec((1, n_heads, p_dim, n_state), lambda b, c: (b, 0, 0, 0)),
        ],
        out_shape=[
            jax.ShapeDtypeStruct((bsz * seqlen, d_inner), BF16),
            jax.ShapeDtypeStruct((bsz, CONV_WIDTH - 1, conv_dim), F32),
            jax.ShapeDtypeStruct((bsz, n_heads, p_dim, n_state), F32),
        ],
        scratch_shapes=[
            pltpu.VMEM((q + CONV_PAD, conv_dim), F32),
        ],
        compiler_params=_params("arbitrary", "arbitrary"),
        name="ssd_seq",
    )(zx, zx, zx, dt_raw, conv_buf, h0, cw, cb.reshape(1, -1), dtb.reshape(1, -1),
      alog.reshape(1, -1), jnp.repeat(dsk, p_dim).reshape(1, -1), nw.reshape(1, -1), expand(p_dim), *extra_in)


def _first_argmax(v, rid, n):
    m = jnp.max(v, axis=0, keepdims=True)
    idx = jnp.min(jnp.where(v == m, rid, float(n)), axis=0, keepdims=True)
    return m, idx


def _route_tile(scores_t, bias_col):
    n_exp, tm = scores_t.shape
    per = n_exp // N_EXPERT_GROUPS
    neg = -jnp.inf
    biased = scores_t + bias_col
    rid_g = lax.broadcasted_iota(jnp.int32, (per, tm), 0).astype(F32)
    tiles, gs_rows = [], []
    for g in range(N_EXPERT_GROUPS):
        v = biased[g * per:(g + 1) * per, :]
        tiles.append(v)
        m1, i1 = _first_argmax(v, rid_g, per)
        m2 = jnp.max(jnp.where(rid_g == i1, neg, v), axis=0, keepdims=True)
        gs_rows.append(m1 + m2)
    gs = jnp.concatenate(gs_rows, axis=0)
    rid_grp = lax.broadcasted_iota(jnp.int32, (N_EXPERT_GROUPS, tm), 0).astype(F32)
    gsel = jnp.zeros((N_EXPERT_GROUPS, tm), F32)
    for _ in range(TOPK_GROUPS):
        _, gi = _first_argmax(gs, rid_grp, N_EXPERT_GROUPS)
        hit = rid_grp == gi
        gsel = jnp.where(hit, 1.0, gsel)
        gs = jnp.where(hit, neg, gs)
    masked = jnp.concatenate(
        [jnp.where(gsel[g:g + 1, :] > 0.0, tiles[g], neg) for g in range(N_EXPERT_GROUPS)], axis=0)
    rid = lax.broadcasted_iota(jnp.int32, (n_exp, tm), 0).astype(F32)
    sel = jnp.zeros((n_exp, tm), F32)
    ids, picked = [], []
    for _ in range(TOP_K):
        _, ei = _first_argmax(masked, rid, n_exp)
        hit = rid == ei
        ids.append(ei)
        picked.append(jnp.sum(jnp.where(hit, scores_t, 0.0), axis=0, keepdims=True))
        sel = jnp.where(hit, 1.0, sel)
        masked = jnp.where(hit, neg, masked)
    total = picked[0]
    for p in picked[1:]:
        total = total + p
    gates = [p / total * ROUTE_SCALE for p in picked]
    return sel, rid, ids, gates


def _proj_ln_router_kernel(alpha, h_ref, w_ref, x_ref, g_ref, b_ref, rwt_ref, rb_ref,
                           o_ref, eid_ref, gate_ref, rank_ref, cnt_ref, wb_ref, tri_ref):
    i = pl.program_id(0)
    tm = x_ref.shape[0]

    @pl.when(i == 0)
    def _():
        wb_ref[...] = w_ref[...].astype(BF16)
        r = lax.broadcasted_iota(jnp.int32, (tm, tm), 0)
        c = lax.broadcasted_iota(jnp.int32, (tm, tm), 1)
        tri_ref[...] = jnp.where(r < c, 1.0, 0.0).astype(BF16)
        cnt_ref[...] = jnp.zeros_like(cnt_ref)

    mix = jnp.dot(h_ref[...], wb_ref[...], preferred_element_type=F32)
    x1 = _layer_norm(alpha * x_ref[...] + mix, g_ref[...], b_ref[...])
    o_ref[...] = x1
    logits_t = lax.dot_general(rwt_ref[...].astype(BF16), x1.astype(BF16), NT_DIMS,
                               preferred_element_type=F32)
    sel, rid, ids, gates = _route_tile(_sigmoid(logits_t), rb_ref[...])

    sel_b = sel.astype(BF16)
    before = jnp.dot(sel_b, tri_ref[...], preferred_element_type=F32) + cnt_ref[:, 0:1]
    pad_rows = eid_ref.shape[0] - TOP_K
    zrow = jnp.zeros((pad_rows, tm), F32)
    ranks = [jnp.sum(jnp.where(rid == ei, before, 0.0), axis=0, keepdims=True) for ei in ids]
    eid_ref[...] = jnp.concatenate(ids + [zrow], axis=0).astype(jnp.int32)
    gate_ref[...] = jnp.concatenate(gates + [zrow], axis=0)
    rank_ref[...] = jnp.concatenate(ranks + [zrow], axis=0).astype(jnp.int32)
    ones = jnp.ones((tm, cnt_ref.shape[1]), BF16)
    cnt_ref[...] = cnt_ref[...] + jnp.dot(sel_b, ones, preferred_element_type=F32)


def _proj_ln_router(h, w, wl, x, alpha, ln_g, ln_b, router_w_t, router_bias_col):
    T, K = h.shape
    D = x.shape[1]
    E = router_w_t.shape[0]
    tm = TM_PROJ
    kp = SUBLANE
    row_spec = pl.BlockSpec((kp, tm), lambda i: (0, i))
    return pl.pallas_call(
        functools.partial(_proj_ln_router_kernel, alpha),
        grid=(T // tm,),
        in_specs=[
            pl.BlockSpec((tm, K), lambda i: (i, 0)),
            pl.BlockSpec((None, K, D), lambda i: (wl, 0, 0)),
            pl.BlockSpec((tm, D), lambda i: (i, 0)),
            pl.BlockSpec((1, D), lambda i: (0, 0)),
            pl.BlockSpec((1, D), lambda i: (0, 0)),
            pl.BlockSpec((E, D), lambda i: (0, 0)),
            pl.BlockSpec((E, 1), lambda i: (0, 0)),
        ],
        out_specs=[pl.BlockSpec((tm, D), lambda i: (i, 0)), row_spec, row_spec, row_spec,
                   pl.BlockSpec((E, LANE), lambda i: (0, 0))],
        out_shape=[jax.ShapeDtypeStruct((T, D), F32),
                   jax.ShapeDtypeStruct((kp, T), jnp.int32),
                   jax.ShapeDtypeStruct((kp, T), F32),
                   jax.ShapeDtypeStruct((kp, T), jnp.int32),
                   jax.ShapeDtypeStruct((E, LANE), F32)],
        scratch_shapes=[pltpu.VMEM((K, D), BF16), pltpu.VMEM((tm, tm), BF16)],
        compiler_params=_params("arbitrary"),
        name="proj_ln_router",
    )(h, w, x, ln_g.reshape(1, D), ln_b.reshape(1, D), router_w_t, router_bias_col)


def _start_row_gather(idx_ref, base, n, src_hbm, dst, sem):
    for i in range(n):
        r = idx_ref[base + i]
        pltpu.make_async_copy(src_hbm.at[pl.ds(r, 1), :], dst.at[pl.ds(i, 1), :], sem).start(priority=i % 2)


def _wait_row_gather(n, src_hbm, dst, sem):
    pltpu.make_async_copy(src_hbm.at[pl.ds(0, n), :], dst, sem).wait()


def _scatter_kernel(n_tiles, pos_ref, zb_ref, nu_ref, x_ref, xs_hbm, xbuf, zbuf, sem, zsem):
    i = pl.program_id(0)
    tm = x_ref.shape[0]
    n_rows = TOP_K * tm
    bm = zbuf.shape[0]
    n_experts = zb_ref.shape[0]
    n_blocks = xs_hbm.shape[0] // bm
    slot = lax.rem(i, 2)

    def zero_block(blk):
        start = pl.multiple_of(blk * bm, bm)
        return pltpu.make_async_copy(zbuf, xs_hbm.at[pl.ds(start, bm), :], zsem)

    def wait_rows(s):
        pltpu.make_async_copy(xs_hbm.at[pl.ds(0, n_rows), :], xs_hbm.at[pl.ds(0, n_rows), :], sem.at[s]).wait()

    @pl.when(i == 0)
    def _():
        zbuf[...] = jnp.zeros_like(zbuf)
        for e in range(n_experts):
            @pl.when(zb_ref[e] >= 0)
            def _():
                zero_block(zb_ref[e]).start()

        def start_tail(blk, carry):
            zero_block(blk).start()
            return carry

        def wait_tail(blk, carry):
            zero_block(blk).wait()
            return carry

        lax.fori_loop(nu_ref[0], n_blocks, start_tail, 0)
        for e in range(n_experts):
            @pl.when(zb_ref[e] >= 0)
            def _():
                zero_block(zb_ref[e]).wait()
        lax.fori_loop(nu_ref[0], n_blocks, wait_tail, 0)

    @pl.when(i >= 2)
    def _():
        wait_rows(slot)

    xbuf[slot] = x_ref[...]
    for k in range(TOP_K):
        for t in range(tm):
            r = pos_ref[i * n_rows + k * tm + t]
            pltpu.make_async_copy(xbuf.at[slot, pl.ds(t, 1), :], xs_hbm.at[pl.ds(r, 1), :],
                                  sem.at[slot]).start(priority=t % 2)

    @pl.when(i == n_tiles - 1)
    def _():
        wait_rows(slot)
        if n_tiles >= 2:
            wait_rows(1 - slot)


def _scatter_rows(x, pos_tiles, zero_blk, n_used, n_slots):
    T, D = x.shape
    tm = TM_COMBINE
    n_tiles = T // tm
    grid_spec = pltpu.PrefetchScalarGridSpec(
        num_scalar_prefetch=3,
        grid=(n_tiles,),
        in_specs=[pl.BlockSpec((tm, D), lambda i, pos, zb, nu: (i, 0))],
        out_specs=pl.BlockSpec(memory_space=pl.ANY),
        scratch_shapes=[
            pltpu.VMEM((2, tm, D), F32),
            pltpu.VMEM((BM_EXPERT, D), F32),
            pltpu.SemaphoreType.DMA((2,)),
            pltpu.SemaphoreType.DMA,
        ],
    )
    return pl.pallas_call(
        functools.partial(_scatter_kernel, n_tiles),
        grid_spec=grid_spec,
        out_shape=jax.ShapeDtypeStruct((n_slots, D), F32),
        compiler_params=_params("arbitrary"),
        name="scatter_rows",
    )(pos_tiles, zero_blk, n_used, x)


def _expert_kernel(be_ref, nu_ref, x_ref, wg_ref, wu_ref, wd_ref, y_ref, wgb, wub, wdb):
    b = pl.program_id(0)
    n_used = nu_ref[0]

    @pl.when(b < n_used)
    def _():
        new_expert = jnp.logical_or(b == 0, be_ref[b] != be_ref[jnp.maximum(b - 1, 0)])

        @pl.when(new_expert)
        def _():
            wgb[...] = wg_ref[...].astype(BF16)
            wub[...] = wu_ref[...].astype(BF16)
            wdb[...] = wd_ref[...].astype(BF16)

        xb = x_ref[...].astype(BF16)
        hg = jnp.dot(xb, wgb[...], preferred_element_type=F32)
        hu = jnp.dot(xb, wub[...], preferred_element_type=F32)
        hh = (_silu(hg) * hu).astype(BF16)
        y_ref[...] = jnp.dot(hh, wdb[...], preferred_element_type=F32)

    @pl.when(b >= n_used)
    def _():
        y_ref[...] = jnp.zeros_like(y_ref)


def _experts(x_sorted, block_e, n_used, wg, wu, wd, layer):
    n_slots, D = x_sorted.shape
    DE = wg.shape[3]
    bm = BM_EXPERT
    nb = n_slots // bm
    w_map = lambda b, be, nu: (layer, be[b], 0, 0)
    grid_spec = pltpu.PrefetchScalarGridSpec(
        num_scalar_prefetch=2,
        grid=(nb,),
        in_specs=[
            pl.BlockSpec((bm, D), lambda b, be, nu: (jnp.minimum(b, nu[0] - 1), 0)),
            pl.BlockSpec((None, None, D, DE), w_map),
            pl.BlockSpec((None, None, D, DE), w_map),
            pl.BlockSpec((None, None, DE, D), w_map),
        ],
        out_specs=pl.BlockSpec((bm, D), lambda b, be, nu: (b, 0)),
        scratch_shapes=[
            pltpu.VMEM((D, DE), BF16),
            pltpu.VMEM((D, DE), BF16),
            pltpu.VMEM((DE, D), BF16),
        ],
    )
    return pl.pallas_call(
        _expert_kernel,
        grid_spec=grid_spec,
        out_shape=jax.ShapeDtypeStruct((n_slots, D), F32),
        compiler_params=_params("arbitrary"),
        name="experts",
    )(block_e, n_used, x_sorted, wg, wu, wd)


def _combine_kernel(alpha, pos_ref, gates_ref, x_ref, y_hbm, sg_ref, su_ref, sd_ref, g_ref, b_ref,
                    o_ref, ybuf, sgb, sub, sdb, sem):
    i = pl.program_id(0)
    tm = x_ref.shape[0]
    n_rows = ybuf.shape[1]
    slot = lax.rem(i, 2)

    @pl.when(i == 0)
    def _():
        _start_row_gather(pos_ref, 0, n_rows, y_hbm, ybuf.at[0], sem.at[0])
        sgb[...] = sg_ref[...].astype(BF16)
        sub[...] = su_ref[...].astype(BF16)
        sdb[...] = sd_ref[...].astype(BF16)

    @pl.when(i + 1 < pl.num_programs(0))
    def _():
        _start_row_gather(pos_ref, (i + 1) * n_rows, n_rows, y_hbm, ybuf.at[1 - slot], sem.at[1 - slot])

    x = x_ref[...]
    xb = x.astype(BF16)
    hg = jnp.dot(xb, sgb[...], preferred_element_type=F32)
    hu = jnp.dot(xb, sub[...], preferred_element_type=F32)
    acc = jnp.dot((_silu(hg) * hu).astype(BF16), sdb[...], preferred_element_type=F32)
    _wait_row_gather(n_rows, y_hbm, ybuf.at[slot], sem.at[slot])
    gates = gates_ref[...]
    for k in range(TOP_K):
        acc = acc + gates[:, k:k + 1] * ybuf[slot, k * tm:(k + 1) * tm, :]
    o_ref[...] = _layer_norm(alpha * x + acc, g_ref[...], b_ref[...])


def _combine(x, y_sorted, pos_tiles, gates, sg, su, sd, alpha, ln_g, ln_b, layer):
    T, D = x.shape
    DS = sg.shape[2]
    tm = TM_COMBINE
    grid_spec = pltpu.PrefetchScalarGridSpec(
        num_scalar_prefetch=1,
        grid=(T // tm,),
        in_specs=[
            pl.BlockSpec((tm, TOP_K), lambda i, pos: (i, 0)),
            pl.BlockSpec((tm, D), lambda i, pos: (i, 0)),
            pl.BlockSpec(memory_space=pl.ANY),
            pl.BlockSpec((None, D, DS), lambda i, pos: (layer, 0, 0)),
            pl.BlockSpec((None, D, DS), lambda i, pos: (layer, 0, 0)),
            pl.BlockSpec((None, DS, D), lambda i, pos: (layer, 0, 0)),
            pl.BlockSpec((None, 1, D), lambda i, pos: (layer, 0, 0)),
            pl.BlockSpec((None, 1, D), lambda i, pos: (layer, 0, 0)),
        ],
        out_specs=pl.BlockSpec((tm, D), lambda i, pos: (i, 0)),
        scratch_shapes=[
            pltpu.VMEM((2, TOP_K * tm, D), F32),
            pltpu.VMEM((D, DS), BF16),
            pltpu.VMEM((D, DS), BF16),
            pltpu.VMEM((DS, D), BF16),
            pltpu.SemaphoreType.DMA((2,)),
        ],
    )
    return pl.pallas_call(
        functools.partial(_combine_kernel, alpha),
        grid_spec=grid_spec,
        out_shape=jax.ShapeDtypeStruct((T, D), F32),
        compiler_params=_params("arbitrary"),
        name="combine",
    )(pos_tiles, gates, x, y_sorted, sg, su, sd, ln_g.reshape(-1, 1, D), ln_b.reshape(-1, 1, D))


def _dispatch(eid_t, rank_t, counts):
    T = eid_t.shape[1]
    E = counts.shape[0]
    A = T * TOP_K
    bm = BM_EXPERT
    nb = -(-A // bm) + E
    cnt = counts[:, 0].astype(jnp.int32)
    padded = (cnt + bm - 1) // bm * bm
    pends = jnp.cumsum(padded)
    pstart = pends - padded
    eid = eid_t[:TOP_K]
    onehot = eid[:, :, None] == jnp.arange(E, dtype=jnp.int32)[None, None, :]
    dest = rank_t[:TOP_K] + jnp.sum(jnp.where(onehot, pstart[None, None, :], 0), axis=-1)
    block_start = jnp.arange(nb, dtype=jnp.int32) * bm
    block_e = jnp.minimum(jnp.sum((pends[None, :] <= block_start[:, None]).astype(jnp.int32), axis=1), E - 1)
    n_used = (pends[-1:] // bm).astype(jnp.int32)
    zero_blk = jnp.where(padded > 0, pends // bm - 1, -1).astype(jnp.int32)
    tm = TM_COMBINE
    pos_tiles = dest.reshape(TOP_K, T // tm, tm).transpose(1, 0, 2).reshape(-1)
    return nb * bm, block_e, n_used, zero_blk, pos_tiles


def _moe(x1, eid_t, gate_t, rank_t, counts, alpha, layer, wg, wu, wd, sg, su, sd, ln_g, ln_b):
    n_slots, block_e, n_used, zero_blk, pos_tiles = _dispatch(eid_t, rank_t, counts)
    x_sorted = _scatter_rows(x1, pos_tiles, zero_blk, n_used, n_slots)
    y_sorted = _experts(x_sorted, block_e, n_used, wg, wu, wd, layer)
    gates = gate_t[:TOP_K].T
    return _combine(x1, y_sorted, pos_tiles, gates, sg, su, sd, alpha, ln_g, ln_b, layer)


def kernel(x_prompt, x_sample, state_lru_conv, state_lru_h, state_ssd_conv, state_ssd, lru_w_in, lru_conv_w, lru_conv_b, lru_w_rgate, lru_b_rgate, lru_w_igate, lru_b_igate, lru_lambda, lru_w_out, ssd_w_in, ssd_conv_w, ssd_conv_b, ssd_dt_bias, ssd_a_log, ssd_d, ssd_norm_w, ssd_w_out, ln_mix_g, ln_mix_b, ln_ffn_g, ln_ffn_b, router_w, router_bias, moe_w_gate, moe_w_up, moe_w_down, shared_w_gate, shared_w_up, shared_w_down):
    bp, lp, d_model = x_prompt.shape
    bs, ls, _ = x_sample.shape
    depth = ln_mix_g.shape[0]
    alpha = (2.0 * depth) ** 0.25
    tp, ts = bp * lp, bs * ls
    n_heads, p_dim, n_state = state_ssd.shape[2:]
    d_inner = n_heads * p_dim
    conv_dim = ssd_conv_w.shape[2]
    d_rnn = lru_conv_w.shape[2]

    x = jnp.concatenate([x_prompt.reshape(tp, d_model), x_sample.reshape(ts, d_model)], axis=0)
    tl_p = min(TL_LRU, lp)
    tl_s = min(TL_LRU, ls)
    q_p = min(SSD_CHUNK, lp)
    q_s = min(SSD_CHUNK, ls)

    p_lru_conv, p_lru_h, p_ssd_conv, p_ssd = [], [], [], []
    s_lru_conv, s_lru_h, s_ssd_conv, s_ssd = [], [], [], []
    for i in range(depth):
        j = i // 2
        if i % 2 == 0:
            gx = _in_proj(x, lru_w_in, j, 2 * d_rnn, TN_LRU_IN)
            wrg = _block_diag_gate(lru_w_rgate[j])
            wig = _block_diag_gate(lru_w_igate[j])
            common = (lru_conv_w[j], lru_conv_b[j], wrg, lru_b_rgate[j], wig, lru_b_igate[j], lru_lambda[j])
            hp, cp, lp_h = _lru_seq(gx, 0, bp, lp, tl_p, jnp.zeros((bp, CONV_WIDTH - 1, d_rnn), F32),
                                    jnp.zeros((bp, d_rnn), F32), *common)
            hs, cs_, ls_h = _lru_seq(gx, tp, bs, ls, tl_s, state_lru_conv[j], state_lru_h[j], *common)
            p_lru_conv.append(cp)
            p_lru_h.append(lp_h.reshape(bp, d_rnn))
            s_lru_conv.append(cs_)
            s_lru_h.append(ls_h.reshape(bs, d_rnn))
            mixed = jnp.concatenate([hp, hs], axis=0)
            w_out, wl = lru_w_out, j
        else:
            zx = _in_proj(x, ssd_w_in, j, d_inner + conv_dim, TN_SSD_IN)
            dt_raw = _small_proj(x, ssd_w_in[j][:, d_inner + conv_dim:])
            common = (ssd_conv_w[j], ssd_conv_b[j], ssd_dt_bias[j], ssd_a_log[j], ssd_d[j], ssd_norm_w[j])
            hp, cp, sp = _ssd_seq(zx, dt_raw, 0, bp, lp, q_p, jnp.zeros((bp, CONV_WIDTH - 1, conv_dim), F32),
                                  jnp.zeros((bp, n_heads, p_dim, n_state), F32), *common)
            hs, cs_, ss = _ssd_seq(zx, dt_raw, tp, bs, ls, q_s, state_ssd_conv[j], state_ssd[j], *common)
            p_ssd_conv.append(cp)
            p_ssd.append(sp)
            s_ssd_conv.append(cs_)
            s_ssd.append(ss)
            mixed = jnp.concatenate([hp, hs], axis=0)
            w_out, wl = ssd_w_out, j
        x1, eid_t, gate_t, rank_t, counts = _proj_ln_router(
            mixed, w_out, wl, x, alpha, ln_mix_g[i], ln_mix_b[i], router_w[i].T, router_bias[i].reshape(-1, 1))
        x = _moe(x1, eid_t, gate_t, rank_t, counts, alpha, i, moe_w_gate, moe_w_up, moe_w_down,
                 shared_w_gate, shared_w_up, shared_w_down, ln_ffn_g, ln_ffn_b)

    y_prompt = x[:tp].reshape(bp, lp, d_model)
    y_sample = x[tp:].reshape(bs, ls, d_model)
    return (y_prompt, y_sample,
            jnp.stack(p_lru_conv), jnp.stack(p_lru_h), jnp.stack(p_ssd_conv), jnp.stack(p_ssd),
            jnp.stack(s_lru_conv), jnp.stack(s_lru_h), jnp.stack(s_ssd_conv), jnp.stack(s_ssd))
```

```python
import functools
import math

import jax
import jax.numpy as jnp
from jax import lax
from jax.experimental import pallas as pl
from jax.experimental.pallas import tpu as pltpu

F32 = jnp.float32
BF16 = jnp.bfloat16

LN_EPS = 1e-5
RMS_EPS = 1e-5
CONV_WIDTH = 4
LRU_C = 8.0
SSD_CHUNK = 64
SSD_GROUPS = 8
TOP_K = 6
N_EXPERT_GROUPS = 8
TOPK_GROUPS = 4
ROUTE_SCALE = 1.0

LANE = 128
SUBLANE = 8
MXU_DIM = 256
VMEM_LIMIT = 56 * 1024 * 1024

TM_PROJ = 512
TM_COMBINE = 128
BM_EXPERT = 256
TL_LRU = 256
TN_LRU_IN = 1024
TN_SSD_IN = 1536

HIGHEST = lax.Precision.HIGHEST
NT_DIMS = (((1,), (1,)), ((), ()))
TN_DIMS = (((0,), (0,)), ((), ()))


def _sigmoid(x):
    return 1.0 / (1.0 + jnp.exp(-x))


def _silu(x):
    return x * _sigmoid(x)


def _softplus(x):
    return jnp.maximum(x, 0.0) + jnp.log1p(jnp.exp(-jnp.abs(x)))


def _gelu_tanh(x):
    c = math.sqrt(2.0 / math.pi)
    return 0.5 * x * (1.0 + jnp.tanh(c * (x + 0.044715 * (x * x * x))))


def _layer_norm(v, g, b):
    mu = jnp.mean(v, axis=-1, keepdims=True)
    d = v - mu
    var = jnp.mean(d * d, axis=-1, keepdims=True)
    return d * lax.rsqrt(var + LN_EPS) * g + b


def _pack_pair(lo, hi):
    lo_bits = lax.bitcast_convert_type(lo.astype(BF16).astype(F32), jnp.uint32)
    hi_bits = lax.bitcast_convert_type(hi.astype(BF16).astype(F32), jnp.uint32)
    return hi_bits | (lo_bits >> 16)


def _unpack_pair(w):
    lo = lax.bitcast_convert_type(w << 16, F32)
    hi = lax.bitcast_convert_type(w & jnp.uint32(0xFFFF0000), F32)
    return lo, hi


def _params(*sem):
    return pltpu.CompilerParams(dimension_semantics=sem, vmem_limit_bytes=VMEM_LIMIT)


def _mm_kernel(x_ref, w_ref, o_ref):
    o_ref[...] = jnp.dot(x_ref[...].astype(BF16), w_ref[...].astype(BF16),
                         preferred_element_type=F32)


def _in_proj_kernel(x_ref, w_ref, o_ref, wb_ref):
    @pl.when(pl.program_id(1) == 0)
    def _():
        wb_ref[...] = w_ref[...].astype(BF16)

    o_ref[...] = jnp.dot(x_ref[...].astype(BF16), wb_ref[...], preferred_element_type=F32)


def _in_proj(x, w, layer, n_cols, tn):
    T, K = x.shape
    tm = TM_PROJ
    return pl.pallas_call(
        _in_proj_kernel,
        grid=(n_cols // tn, T // tm),
        in_specs=[pl.BlockSpec((tm, K), lambda j, i: (i, 0)),
                  pl.BlockSpec((None, K, tn), lambda j, i: (layer, 0, j))],
        out_specs=pl.BlockSpec((tm, tn), lambda j, i: (i, j)),
        out_shape=jax.ShapeDtypeStruct((T, n_cols), F32),
        scratch_shapes=[pltpu.VMEM((K, tn), BF16)],
        compiler_params=_params("arbitrary", "arbitrary"),
        name="in_proj",
    )(x, w)


def _small_proj(x, w):
    T, K = x.shape
    N = w.shape[1]
    tm = TM_PROJ
    return pl.pallas_call(
        _mm_kernel,
        grid=(T // tm,),
        in_specs=[pl.BlockSpec((tm, K), lambda i: (i, 0)),
                  pl.BlockSpec((K, N), lambda i: (0, 0))],
        out_specs=pl.BlockSpec((tm, N), lambda i: (i, 0)),
        out_shape=jax.ShapeDtypeStruct((T, N), F32),
        compiler_params=_params("parallel"),
        name="dt_proj",
    )(x, w)


CONV_PAD = SUBLANE
CONV_HIST = CONV_PAD - (CONV_WIDTH - 1)


def _conv_tile(xx_ref, n, cw, cb):
    acc = cb + cw[CONV_WIDTH - 1:CONV_WIDTH] * xx_ref[CONV_PAD:CONV_PAD + n, :]
    for k in range(CONV_WIDTH - 1):
        off = CONV_HIST + k
        acc = acc + cw[k:k + 1] * xx_ref[off:off + n, :]
    return acc


def _lru_seq_kernel(gate_ref, xr_ref, cbuf_ref, h0_ref, cw_ref, cb_ref, wrg_ref, brg_ref,
                    wig_ref, big_ref, lam_ref, out_ref, nconv_ref, hlast_ref,
                    xx_ref, h_ref, a_ref, u_ref):
    c = pl.program_id(1)
    n = xr_ref.shape[0]
    d = xr_ref.shape[1]

    @pl.when(c == 0)
    def _():
        xx_ref[CONV_HIST:CONV_PAD, :] = cbuf_ref[0]
        h_ref[...] = h0_ref[0]

    xx_ref[CONV_PAD:CONV_PAD + n, :] = xr_ref[...]
    xr = _conv_tile(xx_ref, n, cw_ref[...], cb_ref[...])
    tail = xx_ref[n + CONV_HIST:n + CONV_PAD, :]
    xx_ref[CONV_HIST:CONV_PAD, :] = tail

    xb = xr.astype(BF16)
    nblk = d // MXU_DIM
    r_parts, i_parts = [], []
    for g in range(nblk):
        seg = xb[:, g * MXU_DIM:(g + 1) * MXU_DIM]
        r_parts.append(jnp.dot(seg, wrg_ref[g], preferred_element_type=F32))
        i_parts.append(jnp.dot(seg, wig_ref[g], preferred_element_type=F32))
    r = _sigmoid(jnp.concatenate(r_parts, axis=-1) + brg_ref[...])
    ig = _sigmoid(jnp.concatenate(i_parts, axis=-1) + big_ref[...])
    log_sig_lam = -_softplus(-lam_ref[...])
    log_a = LRU_C * r * log_sig_lam
    a = jnp.exp(log_a)
    a_ref[...] = a
    u_ref[...] = jnp.sqrt(-jnp.tanh(log_a) * (a * a + 1.0)) * (ig * xr)

    def step(t, h):
        h = a_ref[pl.ds(t, 1), :] * h + u_ref[pl.ds(t, 1), :]
        u_ref[pl.ds(t, 1), :] = h
        return h

    h = lax.fori_loop(0, n, step, h_ref[...], unroll=8)
    h_ref[...] = h
    out_ref[...] = (u_ref[...] * _gelu_tanh(gate_ref[...])).astype(BF16)

    @pl.when(c == pl.num_programs(1) - 1)
    def _():
        nconv_ref[0] = tail
        hlast_ref[0] = h


def _lru_seq(gx, row0, bsz, seqlen, tl, conv_buf, h0, cw, cb, wrg, brg, wig, big, lam):
    d = cw.shape[1]
    nt = seqlen // tl
    blk0 = row0 // tl
    row_map = lambda b, c: (blk0 + b * nt + c, 0)
    vec = lambda a: a.reshape(1, d)
    full2 = lambda b, c: (0, 0)
    full3 = lambda b, c: (0, 0, 0)
    return pl.pallas_call(
        _lru_seq_kernel,
        grid=(bsz, nt),
        in_specs=[
            pl.BlockSpec((tl, d), row_map),
            pl.BlockSpec((tl, d), lambda b, c: (blk0 + b * nt + c, 1)),
            pl.BlockSpec((1, CONV_WIDTH - 1, d), lambda b, c: (b, 0, 0)),
            pl.BlockSpec((1, 1, d), lambda b, c: (b, 0, 0)),
            pl.BlockSpec((CONV_WIDTH, d), full2),
            pl.BlockSpec((1, d), full2),
            pl.BlockSpec(wrg.shape, full3),
            pl.BlockSpec((1, d), full2),
            pl.BlockSpec(wig.shape, full3),
            pl.BlockSpec((1, d), full2),
            pl.BlockSpec((1, d), full2),
        ],
        out_specs=[
            pl.BlockSpec((tl, d), lambda b, c: (b * nt + c, 0)),
            pl.BlockSpec((1, CONV_WIDTH - 1, d), lambda b, c: (b, 0, 0)),
            pl.BlockSpec((1, 1, d), lambda b, c: (b, 0, 0)),
        ],
        out_shape=[
            jax.ShapeDtypeStruct((bsz * seqlen, d), BF16),
            jax.ShapeDtypeStruct((bsz, CONV_WIDTH - 1, d), F32),
            jax.ShapeDtypeStruct((bsz, 1, d), F32),
        ],
        scratch_shapes=[
            pltpu.VMEM((tl + CONV_PAD, d), F32),
            pltpu.VMEM((1, d), F32),
            pltpu.VMEM((tl, d), F32),
            pltpu.VMEM((tl, d), F32),
        ],
        compiler_params=_params("arbitrary", "arbitrary"),
        name="lru_seq",
    )(gx, gx, conv_buf, h0.reshape(bsz, 1, d), cw, vec(cb), wrg, vec(brg), wig, vec(big), vec(lam))


def _block_diag_gate(w):
    nb, bw, _ = w.shape
    per = MXU_DIM // bw
    w4 = w.reshape(nb // per, per, bw, bw)
    eye = jnp.eye(per, dtype=w.dtype)
    out = w4[:, :, :, None, :] * eye[None, :, None, :, None]
    return out.reshape(nb // per, MXU_DIM, MXU_DIM).astype(BF16)


def _dot_sel(sel, x, sel_on_left=False, dims=None):
    p1 = x.astype(BF16)
    r1 = x - p1.astype(F32)
    p2 = r1.astype(BF16)
    p3 = (r1 - p2.astype(F32)).astype(BF16)
    out = None
    for p in (p1, p2, p3):
        a, b = (sel, p) if sel_on_left else (p, sel)
        if dims is None:
            term = jnp.dot(a, b, preferred_element_type=F32)
        else:
            term = lax.dot_general(a, b, dims, preferred_element_type=F32)
        out = term if out is None else out + term
    return out


def _ssd_seq_kernel(has_eq, z_ref, xs_ref, bc_ref, dt_ref, cbuf_ref, h0_ref, cw_ref, cb_ref, dtb_ref,
                    alog_ref, dskx_ref, nw_ref, ex_ref, *rest):
    eq_ref = rest[0] if has_eq else None
    out_ref, nconv_ref, st_ref, xx_ref = rest[-4:]
    c = pl.program_id(1)
    q = xs_ref.shape[0]
    d_inner = xs_ref.shape[1]
    n_heads = st_ref.shape[1]
    p_dim = st_ref.shape[2]
    n_state = st_ref.shape[3]
    hpg = n_heads // SSD_GROUPS

    @pl.when(c == 0)
    def _():
        xx_ref[CONV_HIST:CONV_PAD, :] = cbuf_ref[0]
        st_ref[...] = h0_ref[...]

    xx_ref[CONV_PAD:CONV_PAD + q, 0:d_inner] = xs_ref[...]
    xx_ref[CONV_PAD:CONV_PAD + q, d_inner:] = bc_ref[...]
    act = _silu(_conv_tile(xx_ref, q, cw_ref[...], cb_ref[...]))
    tail = xx_ref[q + CONV_HIST:q + CONV_PAD, :]
    xx_ref[CONV_HIST:CONV_PAD, :] = tail

    @pl.when(c == pl.num_programs(1) - 1)
    def _():
        nconv_ref[0] = tail

    dt = _softplus(dt_ref[...] + dtb_ref[...])
    adt = dt * (-jnp.exp(alog_ref[...]))
    row = lax.broadcasted_iota(jnp.int32, (q, q), 0)
    col = lax.broadcasted_iota(jnp.int32, (q, q), 1)
    tril = (col <= row).astype(BF16)
    cs = _dot_sel(tril, adt, sel_on_left=True)
    both_x = _dot_sel(ex_ref[...], jnp.concatenate([dt, cs], axis=0))
    dt_x = both_x[:q]
    cs_x = both_x[q:]
    if eq_ref is None:
        cs_q = cs_x
    else:
        cs_q = _dot_sel(eq_ref[...], cs)

    xs = act[:, :d_inner]
    xd = xs * dt_x
    xd_b = xd.astype(BF16)
    xdec_b = (xd * jnp.exp(cs_x[q - 1:q, :] - cs_x)).astype(BF16)
    ecs_x = jnp.exp(cs_x)

    head = lax.broadcasted_iota(jnp.int32, (q, n_heads), 1)
    even = (head & 1) == 0
    cs_par = jnp.concatenate([jnp.where(even, cs, 0.0), jnp.where(even, 0.0, cs)], axis=0)
    pair_sel = ((lax.broadcasted_iota(jnp.int32, (n_heads // 2, n_heads), 1) >> 1)
                == lax.broadcasted_iota(jnp.int32, (n_heads // 2, n_heads), 0)).astype(BF16)
    rtab = _dot_sel(pair_sel, cs_par, sel_on_left=True, dims=NT_DIMS)
    eye_h = (lax.broadcasted_iota(jnp.int32, (n_heads, n_heads), 0)
             == lax.broadcasted_iota(jnp.int32, (n_heads, n_heads), 1)).astype(BF16)
    cs_t = _dot_sel(eye_h, cs, sel_on_left=True, dims=NT_DIMS)
    st_decay = jnp.exp(jnp.broadcast_to(cs_t[:, q - 1:q], (n_heads, n_state)))

    row2 = lax.broadcasted_iota(jnp.int32, (q, 2 * q), 0)
    col2 = lax.broadcasted_iota(jnp.int32, (q, 2 * q), 1)
    causal2 = jnp.where(col2 >= q, col2 - q, col2) <= row2
    lane_p = lax.broadcasted_iota(jnp.int32, (q, 2 * p_dim), 1)
    lo_half = lane_p < p_dim
    zero_b = jnp.zeros((q, 2 * p_dim), BF16)
    b_off = d_inner
    c_off = d_inner + SSD_GROUPS * n_state
    gp = hpg * p_dim
    pairs_per_group = hpg // 2

    y_diag_parts, y_off_parts = [], []
    for g in range(SSD_GROUPS):
        bg = act[:, b_off + g * n_state:b_off + (g + 1) * n_state].astype(BF16)
        cg = act[:, c_off + g * n_state:c_off + (g + 1) * n_state].astype(BF16)
        cb2 = lax.dot_general(cg, jnp.concatenate([bg, bg], axis=0), NT_DIMS,
                              preferred_element_type=F32)
        prev = st_ref[0, g * hpg:(g + 1) * hpg].reshape(gp, n_state)
        y_off_parts.append(lax.dot_general(cg, prev.astype(BF16), NT_DIMS, preferred_element_type=F32))
        for jj in range(pairs_per_group):
            j = g * pairs_per_group + jj
            lmat = jnp.exp(jnp.where(causal2, cs_q[:, 2 * j * q:2 * (j + 1) * q] - rtab[j:j + 1, :], -jnp.inf))
            m = (cb2 * lmat).astype(BF16)
            pp = xd_b[:, 2 * j * p_dim:2 * (j + 1) * p_dim]
            blockdiag = jnp.concatenate([jnp.where(lo_half, pp, zero_b), jnp.where(lo_half, zero_b, pp)], axis=0)
            y_diag_parts.append(jnp.dot(m, blockdiag, preferred_element_type=F32))
        st_new = lax.dot_general(xdec_b[:, g * gp:(g + 1) * gp], bg, TN_DIMS,
                                 preferred_element_type=F32)
        for e in range(hpg):
            h = g * hpg + e
            st_ref[0, h] = (st_decay[h:h + 1, :] * prev[e * p_dim:(e + 1) * p_dim, :]
                            + st_new[e * p_dim:(e + 1) * p_dim, :])

    y = (jnp.concatenate(y_diag_parts, axis=-1) + jnp.concatenate(y_off_parts, axis=-1) * ecs_x
         + dskx_ref[...] * xs)
    gated = y * _silu(z_ref[...])
    gw = d_inner // SSD_GROUPS
    nw = nw_ref[...]
    for g in range(SSD_GROUPS):
        seg = gated[:, g * gw:(g + 1) * gw]
        ms = jnp.mean(seg * seg, axis=-1, keepdims=True)
        out_ref[:, g * gw:(g + 1) * gw] = (seg * lax.rsqrt(ms + RMS_EPS) * nw[:, g * gw:(g + 1) * gw]).astype(BF16)


def _ssd_seq(zx, dt_raw, row0, bsz, seqlen, q, conv_buf, h0, cw, cb, dtb, alog, dsk, nw):
    n_heads, p_dim, n_state = h0.shape[1:]
    d_inner = n_heads * p_dim
    conv_dim = cw.shape[1]
    nt = seqlen // q
    blk0 = row0 // q
    full2 = lambda b, c: (0, 0)
    col = lambda j: (lambda b, c: (blk0 + b * nt + c, j))
    head_ids = jnp.arange(n_heads, dtype=jnp.int32)[:, None]
    expand = lambda width: (jnp.arange(n_heads * width, dtype=jnp.int32)[None, :] // width == head_ids).astype(BF16)
    has_eq = q != p_dim
    extra_in = [expand(q)] if has_eq else []
    extra_specs = [pl.BlockSpec((n_heads, n_heads * q), full2)] if has_eq else []
    return pl.pallas_call(
        functools.partial(_ssd_seq_kernel, has_eq),
        grid=(bsz, nt),
        in_specs=[
            pl.BlockSpec((q, d_inner), col(0)),
            pl.BlockSpec((q, d_inner), col(1)),
            pl.BlockSpec((q, d_inner), col(2)),
            pl.BlockSpec((q, n_heads), col(0)),
            pl.BlockSpec((1, CONV_WIDTH - 1, conv_dim), lambda b, c: (b, 0, 0)),
            pl.BlockSpec((1, n_heads, p_dim, n_state), lambda b, c: (b, 0, 0, 0)),
            pl.BlockSpec((CONV_WIDTH, conv_dim), full2),
            pl.BlockSpec((1, conv_dim), full2),
            pl.BlockSpec((1, n_heads), full2),
            pl.BlockSpec((1, n_heads), full2),
            pl.BlockSpec((1, d_inner), full2),
            pl.BlockSpec((1, d_inner), full2),
            pl.BlockSpec((n_heads, d_inner), full2),
        ] + extra_specs,
        out_specs=[
            pl.BlockSpec((q, d_inner), lambda b, c: (b * nt + c, 0)),
            pl.BlockSpec((1, CONV_WIDTH - 1, conv_dim), lambda b, c: (b, 0, 0)),
            pl.BlockSpec((1, n_heads, p_dim, n_state), lambda b, c: (b, 0, 0, 0)),
        ],
        out_shape=[
            jax.ShapeDtypeStruct((bsz * seqlen, d_inner), BF16),
            jax.ShapeDtypeStruct((bsz, CONV_WIDTH - 1, conv_dim), F32),
            jax.ShapeDtypeStruct((bsz, n_heads, p_dim, n_state), F32),
        ],
        scratch_shapes=[
            pltpu.VMEM((q + CONV_PAD, conv_dim), F32),
        ],
        compiler_params=_params("arbitrary", "arbitrary"),
        name="ssd_seq",
    )(zx, zx, zx, dt_raw, conv_buf, h0, cw, cb.reshape(1, -1), dtb.reshape(1, -1),
      alog.reshape(1, -1), jnp.repeat(dsk, p_dim).reshape(1, -1), nw.reshape(1, -1), expand(p_dim), *extra_in)


def _first_argmax(v, rid, n):
    m = jnp.max(v, axis=0, keepdims=True)
    idx = jnp.min(jnp.where(v == m, rid, float(n)), axis=0, keepdims=True)
    return m, idx


def _route_tile(scores_t, bias_col):
    n_exp, tm = scores_t.shape
    per = n_exp // N_EXPERT_GROUPS
    neg = -jnp.inf
    biased = scores_t + bias_col
    rid_g = lax.broadcasted_iota(jnp.int32, (per, tm), 0).astype(F32)
    tiles, gs_rows = [], []
    for g in range(N_EXPERT_GROUPS):
        v = biased[g * per:(g + 1) * per, :]
        tiles.append(v)
        m1, i1 = _first_argmax(v, rid_g, per)
        m2 = jnp.max(jnp.where(rid_g == i1, neg, v), axis=0, keepdims=True)
        gs_rows.append(m1 + m2)
    gs = jnp.concatenate(gs_rows, axis=0)
    rid_grp = lax.broadcasted_iota(jnp.int32, (N_EXPERT_GROUPS, tm), 0).astype(F32)
    gsel = jnp.zeros((N_EXPERT_GROUPS, tm), F32)
    for _ in range(TOPK_GROUPS):
        _, gi = _first_argmax(gs, rid_grp, N_EXPERT_GROUPS)
        hit = rid_grp == gi
        gsel = jnp.where(hit, 1.0, gsel)
        gs = jnp.where(hit, neg, gs)
    masked = jnp.concatenate(
        [jnp.where(gsel[g:g + 1, :] > 0.0, tiles[g], neg) for g in range(N_EXPERT_GROUPS)], axis=0)
    rid = lax.broadcasted_iota(jnp.int32, (n_exp, tm), 0).astype(F32)
    sel = jnp.zeros((n_exp, tm), F32)
    ids, picked = [], []
    for _ in range(TOP_K):
        _, ei = _first_argmax(masked, rid, n_exp)
        hit = rid == ei
        ids.append(ei)
        picked.append(jnp.sum(jnp.where(hit, scores_t, 0.0), axis=0, keepdims=True))
        sel = jnp.where(hit, 1.0, sel)
        masked = jnp.where(hit, neg, masked)
    total = picked[0]
    for p in picked[1:]:
        total = total + p
    gates = [p / total * ROUTE_SCALE for p in picked]
    return sel, rid, ids, gates


def _proj_ln_router_kernel(alpha, h_ref, w_ref, x_ref, g_ref, b_ref, rwt_ref, rb_ref,
                           o_ref, op_ref, eid_ref, gate_ref, rank_ref, cnt_ref, wb_ref, tri_ref):
    i = pl.program_id(0)
    tm = x_ref.shape[0]

    @pl.when(i == 0)
    def _():
        wb_ref[...] = w_ref[...].astype(BF16)
        r = lax.broadcasted_iota(jnp.int32, (tm, tm), 0)
        c = lax.broadcasted_iota(jnp.int32, (tm, tm), 1)
        tri_ref[...] = jnp.where(r < c, 1.0, 0.0).astype(BF16)
        cnt_ref[...] = jnp.zeros_like(cnt_ref)

    mix = jnp.dot(h_ref[...], wb_ref[...], preferred_element_type=F32)
    x1 = _layer_norm(alpha * x_ref[...] + mix, g_ref[...], b_ref[...])
    o_ref[...] = x1
    half = x1.shape[1] // 2
    op_ref[...] = _pack_pair(x1[:, :half], x1[:, half:])
    logits_t = lax.dot_general(rwt_ref[...].astype(BF16), x1.astype(BF16), NT_DIMS,
                               preferred_element_type=F32)
    sel, rid, ids, gates = _route_tile(_sigmoid(logits_t), rb_ref[...])

    sel_b = sel.astype(BF16)
    before = jnp.dot(sel_b, tri_ref[...], preferred_element_type=F32) + cnt_ref[:, 0:1]
    pad_rows = eid_ref.shape[0] - TOP_K
    zrow = jnp.zeros((pad_rows, tm), F32)
    ranks = [jnp.sum(jnp.where(rid == ei, before, 0.0), axis=0, keepdims=True) for ei in ids]
    eid_ref[...] = jnp.concatenate(ids + [zrow], axis=0).astype(jnp.int32)
    gate_ref[...] = jnp.concatenate(gates + [zrow], axis=0)
    rank_ref[...] = jnp.concatenate(ranks + [zrow], axis=0).astype(jnp.int32)
    ones = jnp.ones((tm, cnt_ref.shape[1]), BF16)
    cnt_ref[...] = cnt_ref[...] + jnp.dot(sel_b, ones, preferred_element_type=F32)


def _proj_ln_router(h, w, wl, x, alpha, ln_g, ln_b, router_w_t, router_bias_col):
    T, K = h.shape
    D = x.shape[1]
    E = router_w_t.shape[0]
    tm = TM_PROJ
    kp = SUBLANE
    row_spec = pl.BlockSpec((kp, tm), lambda i: (0, i))
    return pl.pallas_call(
        functools.partial(_proj_ln_router_kernel, alpha),
        grid=(T // tm,),
        in_specs=[
            pl.BlockSpec((tm, K), lambda i: (i, 0)),
            pl.BlockSpec((None, K, D), lambda i: (wl, 0, 0)),
            pl.BlockSpec((tm, D), lambda i: (i, 0)),
            pl.BlockSpec((1, D), lambda i: (0, 0)),
            pl.BlockSpec((1, D), lambda i: (0, 0)),
            pl.BlockSpec((E, D), lambda i: (0, 0)),
            pl.BlockSpec((E, 1), lambda i: (0, 0)),
        ],
        out_specs=[pl.BlockSpec((tm, D), lambda i: (i, 0)), pl.BlockSpec((tm, D // 2), lambda i: (i, 0)),
                   row_spec, row_spec, row_spec, pl.BlockSpec((E, LANE), lambda i: (0, 0))],
        out_shape=[jax.ShapeDtypeStruct((T, D), F32),
                   jax.ShapeDtypeStruct((T, D // 2), jnp.uint32),
                   jax.ShapeDtypeStruct((kp, T), jnp.int32),
                   jax.ShapeDtypeStruct((kp, T), F32),
                   jax.ShapeDtypeStruct((kp, T), jnp.int32),
                   jax.ShapeDtypeStruct((E, LANE), F32)],
        scratch_shapes=[pltpu.VMEM((K, D), BF16), pltpu.VMEM((tm, tm), BF16)],
        compiler_params=_params("arbitrary"),
        name="proj_ln_router",
    )(h, w, x, ln_g.reshape(1, D), ln_b.reshape(1, D), router_w_t, router_bias_col)


def _start_row_gather(idx_ref, base, n, src_hbm, dst, sem):
    for i in range(n):
        r = idx_ref[base + i]
        pltpu.make_async_copy(src_hbm.at[pl.ds(r, 1), :], dst.at[pl.ds(i, 1), :], sem).start(priority=i % 2)


def _wait_row_gather(n, src_hbm, dst, sem):
    pltpu.make_async_copy(src_hbm.at[pl.ds(0, n), :], dst, sem).wait()


def _scatter_kernel(n_tiles, pos_ref, zb_ref, nu_ref, x_ref, xs_hbm, xbuf, zbuf, sem, zsem):
    i = pl.program_id(0)
    tm = x_ref.shape[0]
    n_rows = TOP_K * tm
    bm = zbuf.shape[0]
    n_experts = zb_ref.shape[0]
    n_blocks = xs_hbm.shape[0] // bm
    slot = lax.rem(i, 2)

    def zero_block(blk):
        start = pl.multiple_of(blk * bm, bm)
        return pltpu.make_async_copy(zbuf, xs_hbm.at[pl.ds(start, bm), :], zsem)

    def wait_rows(s):
        pltpu.make_async_copy(xs_hbm.at[pl.ds(0, n_rows), :], xs_hbm.at[pl.ds(0, n_rows), :], sem.at[s]).wait()

    @pl.when(i == 0)
    def _():
        zbuf[...] = jnp.zeros_like(zbuf)
        for e in range(n_experts):
            @pl.when(zb_ref[e] >= 0)
            def _():
                zero_block(zb_ref[e]).start()

        def start_tail(blk, carry):
            zero_block(blk).start()
            return carry

        def wait_tail(blk, carry):
            zero_block(blk).wait()
            return carry

        lax.fori_loop(nu_ref[0], n_blocks, start_tail, 0)
        for e in range(n_experts):
            @pl.when(zb_ref[e] >= 0)
            def _():
                zero_block(zb_ref[e]).wait()
        lax.fori_loop(nu_ref[0], n_blocks, wait_tail, 0)

    @pl.when(i >= 2)
    def _():
        wait_rows(slot)

    xbuf[slot] = x_ref[...]
    for k in range(TOP_K):
        for t in range(tm):
            r = pos_ref[i * n_rows + k * tm + t]
            pltpu.make_async_copy(xbuf.at[slot, pl.ds(t, 1), :], xs_hbm.at[pl.ds(r, 1), :],
                                  sem.at[slot]).start(priority=t % 2)

    @pl.when(i == n_tiles - 1)
    def _():
        wait_rows(slot)
        if n_tiles >= 2:
            wait_rows(1 - slot)


def _scatter_rows(x, pos_tiles, zero_blk, n_used, n_slots):
    T, D = x.shape
    tm = TM_COMBINE
    n_tiles = T // tm
    grid_spec = pltpu.PrefetchScalarGridSpec(
        num_scalar_prefetch=3,
        grid=(n_tiles,),
        in_specs=[pl.BlockSpec((tm, D), lambda i, pos, zb, nu: (i, 0))],
        out_specs=pl.BlockSpec(memory_space=pl.ANY),
        scratch_shapes=[
            pltpu.VMEM((2, tm, D), x.dtype),
            pltpu.VMEM((BM_EXPERT, D), x.dtype),
            pltpu.SemaphoreType.DMA((2,)),
            pltpu.SemaphoreType.DMA,
        ],
    )
    return pl.pallas_call(
        functools.partial(_scatter_kernel, n_tiles),
        grid_spec=grid_spec,
        out_shape=jax.ShapeDtypeStruct((n_slots, D), x.dtype),
        compiler_params=_params("arbitrary"),
        name="scatter_rows",
    )(pos_tiles, zero_blk, n_used, x)


def _expert_kernel(be_ref, nu_ref, x_ref, wg_ref, wu_ref, wd_ref, y_ref, wgb, wub, wdb):
    b = pl.program_id(0)
    n_used = nu_ref[0]

    @pl.when(b < n_used)
    def _():
        new_expert = jnp.logical_or(b == 0, be_ref[b] != be_ref[jnp.maximum(b - 1, 0)])

        @pl.when(new_expert)
        def _():
            wgb[...] = wg_ref[...].astype(BF16)
            wub[...] = wu_ref[...].astype(BF16)
            wdb[...] = wd_ref[...].astype(BF16)

        half = x_ref.shape[1]
        x_lo, x_hi = _unpack_pair(x_ref[...])
        x_lo = x_lo.astype(BF16)
        x_hi = x_hi.astype(BF16)
        hg = (jnp.dot(x_lo, wgb[0:half, :], preferred_element_type=F32)
              + jnp.dot(x_hi, wgb[half:, :], preferred_element_type=F32))
        hu = (jnp.dot(x_lo, wub[0:half, :], preferred_element_type=F32)
              + jnp.dot(x_hi, wub[half:, :], preferred_element_type=F32))
        hh = (_silu(hg) * hu).astype(BF16)
        y = jnp.dot(hh, wdb[...], preferred_element_type=F32)
        y_ref[...] = _pack_pair(y[:, :half], y[:, half:])

    @pl.when(b >= n_used)
    def _():
        y_ref[...] = jnp.zeros_like(y_ref)


def _experts(x_sorted, block_e, n_used, wg, wu, wd, layer):
    n_slots, dh = x_sorted.shape
    D, DE = wg.shape[2:]
    bm = BM_EXPERT
    nb = n_slots // bm
    w_map = lambda b, be, nu: (layer, be[b], 0, 0)
    grid_spec = pltpu.PrefetchScalarGridSpec(
        num_scalar_prefetch=2,
        grid=(nb,),
        in_specs=[
            pl.BlockSpec((bm, dh), lambda b, be, nu: (jnp.minimum(b, nu[0] - 1), 0)),
            pl.BlockSpec((None, None, D, DE), w_map),
            pl.BlockSpec((None, None, D, DE), w_map),
            pl.BlockSpec((None, None, DE, D), w_map),
        ],
        out_specs=pl.BlockSpec((bm, dh), lambda b, be, nu: (b, 0)),
        scratch_shapes=[
            pltpu.VMEM((D, DE), BF16),
            pltpu.VMEM((D, DE), BF16),
            pltpu.VMEM((DE, D), BF16),
        ],
    )
    return pl.pallas_call(
        _expert_kernel,
        grid_spec=grid_spec,
        out_shape=jax.ShapeDtypeStruct((n_slots, dh), jnp.uint32),
        compiler_params=_params("arbitrary"),
        name="experts",
    )(block_e, n_used, x_sorted, wg, wu, wd)


def _combine_kernel(alpha, pos_ref, gates_ref, x_ref, y_hbm, sg_ref, su_ref, sd_ref, g_ref, b_ref,
                    o_ref, ybuf, sgb, sub, sdb, sem):
    i = pl.program_id(0)
    tm = x_ref.shape[0]
    n_rows = ybuf.shape[1]
    slot = lax.rem(i, 2)

    @pl.when(i == 0)
    def _():
        _start_row_gather(pos_ref, 0, n_rows, y_hbm, ybuf.at[0], sem.at[0])
        sgb[...] = sg_ref[...].astype(BF16)
        sub[...] = su_ref[...].astype(BF16)
        sdb[...] = sd_ref[...].astype(BF16)

    @pl.when(i + 1 < pl.num_programs(0))
    def _():
        _start_row_gather(pos_ref, (i + 1) * n_rows, n_rows, y_hbm, ybuf.at[1 - slot], sem.at[1 - slot])

    x = x_ref[...]
    xb = x.astype(BF16)
    hg = jnp.dot(xb, sgb[...], preferred_element_type=F32)
    hu = jnp.dot(xb, sub[...], preferred_element_type=F32)
    acc = jnp.dot((_silu(hg) * hu).astype(BF16), sdb[...], preferred_element_type=F32)
    _wait_row_gather(n_rows, y_hbm, ybuf.at[slot], sem.at[slot])
    gates = gates_ref[...]
    routed_lo = routed_hi = None
    for k in range(TOP_K):
        y_lo, y_hi = _unpack_pair(ybuf[slot, k * tm:(k + 1) * tm, :])
        gk = gates[:, k:k + 1]
        routed_lo = gk * y_lo if routed_lo is None else routed_lo + gk * y_lo
        routed_hi = gk * y_hi if routed_hi is None else routed_hi + gk * y_hi
    acc = acc + jnp.concatenate([routed_lo, routed_hi], axis=-1)
    o_ref[...] = _layer_norm(alpha * x + acc, g_ref[...], b_ref[...])


def _combine(x, y_sorted, pos_tiles, gates, sg, su, sd, alpha, ln_g, ln_b, layer):
    T, D = x.shape
    DS = sg.shape[2]
    tm = TM_COMBINE
    grid_spec = pltpu.PrefetchScalarGridSpec(
        num_scalar_prefetch=1,
        grid=(T // tm,),
        in_specs=[
            pl.BlockSpec((tm, TOP_K), lambda i, pos: (i, 0)),
            pl.BlockSpec((tm, D), lambda i, pos: (i, 0)),
            pl.BlockSpec(memory_space=pl.ANY),
            pl.BlockSpec((None, D, DS), lambda i, pos: (layer, 0, 0)),
            pl.BlockSpec((None, D, DS), lambda i, pos: (layer, 0, 0)),
            pl.BlockSpec((None, DS, D), lambda i, pos: (layer, 0, 0)),
            pl.BlockSpec((None, 1, D), lambda i, pos: (layer, 0, 0)),
            pl.BlockSpec((None, 1, D), lambda i, pos: (layer, 0, 0)),
        ],
        out_specs=pl.BlockSpec((tm, D), lambda i, pos: (i, 0)),
        scratch_shapes=[
            pltpu.VMEM((2, TOP_K * tm, y_sorted.shape[1]), y_sorted.dtype),
            pltpu.VMEM((D, DS), BF16),
            pltpu.VMEM((D, DS), BF16),
            pltpu.VMEM((DS, D), BF16),
            pltpu.SemaphoreType.DMA((2,)),
        ],
    )
    return pl.pallas_call(
        functools.partial(_combine_kernel, alpha),
        grid_spec=grid_spec,
        out_shape=jax.ShapeDtypeStruct((T, D), F32),
        compiler_params=_params("arbitrary"),
        name="combine",
    )(pos_tiles, gates, x, y_sorted, sg, su, sd, ln_g.reshape(-1, 1, D), ln_b.reshape(-1, 1, D))


def _dispatch(eid_t, rank_t, counts):
    T = eid_t.shape[1]
    E = counts.shape[0]
    A = T * TOP_K
    bm = BM_EXPERT
    nb = -(-A // bm) + E
    cnt = counts[:, 0].astype(jnp.int32)
    padded = (cnt + bm - 1) // bm * bm
    pends = jnp.cumsum(padded)
    pstart = pends - padded
    eid = eid_t[:TOP_K]
    onehot = eid[:, :, None] == jnp.arange(E, dtype=jnp.int32)[None, None, :]
    dest = rank_t[:TOP_K] + jnp.sum(jnp.where(onehot, pstart[None, None, :], 0), axis=-1)
    block_start = jnp.arange(nb, dtype=jnp.int32) * bm
    block_e = jnp.minimum(jnp.sum((pends[None, :] <= block_start[:, None]).astype(jnp.int32), axis=1), E - 1)
    n_used = (pends[-1:] // bm).astype(jnp.int32)
    zero_blk = jnp.where(padded > 0, pends // bm - 1, -1).astype(jnp.int32)
    tm = TM_COMBINE
    pos_tiles = dest.reshape(TOP_K, T // tm, tm).transpose(1, 0, 2).reshape(-1)
    return nb * bm, block_e, n_used, zero_blk, pos_tiles


def _moe(x1, x1_packed, eid_t, gate_t, rank_t, counts, alpha, layer, wg, wu, wd, sg, su, sd, ln_g, ln_b):
    n_slots, block_e, n_used, zero_blk, pos_tiles = _dispatch(eid_t, rank_t, counts)
    x_sorted = _scatter_rows(x1_packed, pos_tiles, zero_blk, n_used, n_slots)
    y_sorted = _experts(x_sorted, block_e, n_used, wg, wu, wd, layer)
    gates = gate_t[:TOP_K].T
    return _combine(x1, y_sorted, pos_tiles, gates, sg, su, sd, alpha, ln_g, ln_b, layer)


def kernel(x_prompt, x_sample, state_lru_conv, state_lru_h, state_ssd_conv, state_ssd, lru_w_in, lru_conv_w, lru_conv_b, lru_w_rgate, lru_b_rgate, lru_w_igate, lru_b_igate, lru_lambda, lru_w_out, ssd_w_in, ssd_conv_w, ssd_conv_b, ssd_dt_bias, ssd_a_log, ssd_d, ssd_norm_w, ssd_w_out, ln_mix_g, ln_mix_b, ln_ffn_g, ln_ffn_b, router_w, router_bias, moe_w_gate, moe_w_up, moe_w_down, shared_w_gate, shared_w_up, shared_w_down):
    bp, lp, d_model = x_prompt.shape
    bs, ls, _ = x_sample.shape
    depth = ln_mix_g.shape[0]
    alpha = (2.0 * depth) ** 0.25
    tp, ts = bp * lp, bs * ls
    n_heads, p_dim, n_state = state_ssd.shape[2:]
    d_inner = n_heads * p_dim
    conv_dim = ssd_conv_w.shape[2]
    d_rnn = lru_conv_w.shape[2]

    x = jnp.concatenate([x_prompt.reshape(tp, d_model), x_sample.reshape(ts, d_model)], axis=0)
    tl_p = min(TL_LRU, lp)
    tl_s = min(TL_LRU, ls)
    q_p = min(SSD_CHUNK, lp)
    q_s = min(SSD_CHUNK, ls)

    p_lru_conv, p_lru_h, p_ssd_conv, p_ssd = [], [], [], []
    s_lru_conv, s_lru_h, s_ssd_conv, s_ssd = [], [], [], []
    for i in range(depth):
        j = i // 2
        if i % 2 == 0:
            gx = _in_proj(x, lru_w_in, j, 2 * d_rnn, TN_LRU_IN)
            wrg = _block_diag_gate(lru_w_rgate[j])
            wig = _block_diag_gate(lru_w_igate[j])
            common = (lru_conv_w[j], lru_conv_b[j], wrg, lru_b_rgate[j], wig, lru_b_igate[j], lru_lambda[j])
            hp, cp, lp_h = _lru_seq(gx, 0, bp, lp, tl_p, jnp.zeros((bp, CONV_WIDTH - 1, d_rnn), F32),
                                    jnp.zeros((bp, d_rnn), F32), *common)
            hs, cs_, ls_h = _lru_seq(gx, tp, bs, ls, tl_s, state_lru_conv[j], state_lru_h[j], *common)
            p_lru_conv.append(cp)
            p_lru_h.append(lp_h.reshape(bp, d_rnn))
            s_lru_conv.append(cs_)
            s_lru_h.append(ls_h.reshape(bs, d_rnn))
            mixed = jnp.concatenate([hp, hs], axis=0)
            w_out, wl = lru_w_out, j
        else:
            zx = _in_proj(x, ssd_w_in, j, d_inner + conv_dim, TN_SSD_IN)
            dt_raw = _small_proj(x, ssd_w_in[j][:, d_inner + conv_dim:])
            common = (ssd_conv_w[j], ssd_conv_b[j], ssd_dt_bias[j], ssd_a_log[j], ssd_d[j], ssd_norm_w[j])
            hp, cp, sp = _ssd_seq(zx, dt_raw, 0, bp, lp, q_p, jnp.zeros((bp, CONV_WIDTH - 1, conv_dim), F32),
                                  jnp.zeros((bp, n_heads, p_dim, n_state), F32), *common)
            hs, cs_, ss = _ssd_seq(zx, dt_raw, tp, bs, ls, q_s, state_ssd_conv[j], state_ssd[j], *common)
            p_ssd_conv.append(cp)
            p_ssd.append(sp)
            s_ssd_conv.append(cs_)
            s_ssd.append(ss)
            mixed = jnp.concatenate([hp, hs], axis=0)
            w_out, wl = ssd_w_out, j
        x1, x1_packed, eid_t, gate_t, rank_t, counts = _proj_ln_router(
            mixed, w_out, wl, x, alpha, ln_mix_g[i], ln_mix_b[i], router_w[i].T, router_bias[i].reshape(-1, 1))
        x = _moe(x1, x1_packed, eid_t, gate_t, rank_t, counts, alpha, i, moe_w_gate, moe_w_up, moe_w_down,
                 shared_w_gate, shared_w_up, shared_w_down, ln_ffn_g, ln_ffn_b)

    y_prompt = x[:tp].reshape(bp, lp, d_model)
    y_sample = x[tp:].reshape(bs, ls, d_model)
    return (y_prompt, y_sample,
            jnp.stack(p_lru_conv), jnp.stack(p_lru_h), jnp.stack(p_ssd_conv), jnp.stack(p_ssd),
            jnp.stack(s_lru_conv), jnp.stack(s_lru_h), jnp.stack(s_ssd_conv), jnp.stack(s_ssd))
```

```python
import functools
import math

import jax
import jax.numpy as jnp
from jax import lax
from jax.experimental import pallas as pl
from jax.experimental.pallas import tpu as pltpu

F32 = jnp.float32
BF16 = jnp.bfloat16

LN_EPS = 1e-5
RMS_EPS = 1e-5
CONV_WIDTH = 4
LRU_C = 8.0
SSD_CHUNK = 64
SSD_GROUPS = 8
TOP_K = 6
N_EXPERT_GROUPS = 8
TOPK_GROUPS = 4
ROUTE_SCALE = 1.0

LANE = 128
SUBLANE = 8
MXU_DIM = 256
VMEM_LIMIT = 56 * 1024 * 1024

TM_PROJ = 512
TM_COMBINE = 128
BM_EXPERT = 512
TL_LRU = 256
TN_LRU_IN = 1024
TN_SSD_IN = 1536

HIGHEST = lax.Precision.HIGHEST
NT_DIMS = (((1,), (1,)), ((), ()))
TN_DIMS = (((0,), (0,)), ((), ()))


def _sigmoid(x):
    return 1.0 / (1.0 + jnp.exp(-x))


def _silu(x):
    return x * _sigmoid(x)


def _softplus(x):
    return jnp.maximum(x, 0.0) + jnp.log1p(jnp.exp(-jnp.abs(x)))


def _gelu_tanh(x):
    c = math.sqrt(2.0 / math.pi)
    return 0.5 * x * (1.0 + jnp.tanh(c * (x + 0.044715 * (x * x * x))))


def _layer_norm(v, g, b):
    mu = jnp.mean(v, axis=-1, keepdims=True)
    d = v - mu
    var = jnp.mean(d * d, axis=-1, keepdims=True)
    return d * lax.rsqrt(var + LN_EPS) * g + b


def _pack_pair(lo, hi):
    lo_bits = lax.bitcast_convert_type(lo.astype(BF16).astype(F32), jnp.uint32)
    hi_bits = lax.bitcast_convert_type(hi.astype(BF16).astype(F32), jnp.uint32)
    return hi_bits | (lo_bits >> 16)


def _unpack_pair(w):
    lo = lax.bitcast_convert_type(w << 16, F32)
    hi = lax.bitcast_convert_type(w & jnp.uint32(0xFFFF0000), F32)
    return lo, hi


def _params(*sem):
    return pltpu.CompilerParams(dimension_semantics=sem, vmem_limit_bytes=VMEM_LIMIT)


def _mm_kernel(x_ref, w_ref, o_ref):
    o_ref[...] = jnp.dot(x_ref[...].astype(BF16), w_ref[...].astype(BF16),
                         preferred_element_type=F32)


def _in_proj_kernel(x_ref, w_ref, o_ref, wb_ref):
    @pl.when(pl.program_id(1) == 0)
    def _():
        wb_ref[...] = w_ref[...].astype(BF16)

    o_ref[...] = jnp.dot(x_ref[...].astype(BF16), wb_ref[...], preferred_element_type=F32)


def _in_proj(x, w, layer, n_cols, tn):
    T, K = x.shape
    tm = TM_PROJ
    return pl.pallas_call(
        _in_proj_kernel,
        grid=(n_cols // tn, T // tm),
        in_specs=[pl.BlockSpec((tm, K), lambda j, i: (i, 0)),
                  pl.BlockSpec((None, K, tn), lambda j, i: (layer, 0, j))],
        out_specs=pl.BlockSpec((tm, tn), lambda j, i: (i, j)),
        out_shape=jax.ShapeDtypeStruct((T, n_cols), F32),
        scratch_shapes=[pltpu.VMEM((K, tn), BF16)],
        compiler_params=_params("arbitrary", "arbitrary"),
        name="in_proj",
    )(x, w)


def _small_proj(x, w):
    T, K = x.shape
    N = w.shape[1]
    tm = TM_PROJ
    return pl.pallas_call(
        _mm_kernel,
        grid=(T // tm,),
        in_specs=[pl.BlockSpec((tm, K), lambda i: (i, 0)),
                  pl.BlockSpec((K, N), lambda i: (0, 0))],
        out_specs=pl.BlockSpec((tm, N), lambda i: (i, 0)),
        out_shape=jax.ShapeDtypeStruct((T, N), F32),
        compiler_params=_params("parallel"),
        name="dt_proj",
    )(x, w)


CONV_PAD = SUBLANE
CONV_HIST = CONV_PAD - (CONV_WIDTH - 1)


def _conv_tile(xx_ref, n, cw, cb):
    acc = cb + cw[CONV_WIDTH - 1:CONV_WIDTH] * xx_ref[CONV_PAD:CONV_PAD + n, :]
    for k in range(CONV_WIDTH - 1):
        off = CONV_HIST + k
        acc = acc + cw[k:k + 1] * xx_ref[off:off + n, :]
    return acc


def _lru_seq_kernel(gate_ref, xr_ref, cbuf_ref, h0_ref, cw_ref, cb_ref, wrg_ref, brg_ref,
                    wig_ref, big_ref, lam_ref, out_ref, nconv_ref, hlast_ref,
                    xx_ref, h_ref, a_ref, u_ref):
    c = pl.program_id(1)
    n = xr_ref.shape[0]
    d = xr_ref.shape[1]

    @pl.when(c == 0)
    def _():
        xx_ref[CONV_HIST:CONV_PAD, :] = cbuf_ref[0]
        h_ref[...] = h0_ref[0]

    xx_ref[CONV_PAD:CONV_PAD + n, :] = xr_ref[...]
    xr = _conv_tile(xx_ref, n, cw_ref[...], cb_ref[...])
    tail = xx_ref[n + CONV_HIST:n + CONV_PAD, :]
    xx_ref[CONV_HIST:CONV_PAD, :] = tail

    xb = xr.astype(BF16)
    nblk = d // MXU_DIM
    r_parts, i_parts = [], []
    for g in range(nblk):
        seg = xb[:, g * MXU_DIM:(g + 1) * MXU_DIM]
        r_parts.append(jnp.dot(seg, wrg_ref[g], preferred_element_type=F32))
        i_parts.append(jnp.dot(seg, wig_ref[g], preferred_element_type=F32))
    r = _sigmoid(jnp.concatenate(r_parts, axis=-1) + brg_ref[...])
    ig = _sigmoid(jnp.concatenate(i_parts, axis=-1) + big_ref[...])
    log_sig_lam = -_softplus(-lam_ref[...])
    log_a = LRU_C * r * log_sig_lam
    a = jnp.exp(log_a)
    a_ref[...] = a
    u_ref[...] = jnp.sqrt(-jnp.tanh(log_a) * (a * a + 1.0)) * (ig * xr)

    def step(t, h):
        h = a_ref[pl.ds(t, 1), :] * h + u_ref[pl.ds(t, 1), :]
        u_ref[pl.ds(t, 1), :] = h
        return h

    h = lax.fori_loop(0, n, step, h_ref[...], unroll=8)
    h_ref[...] = h
    out_ref[...] = (u_ref[...] * _gelu_tanh(gate_ref[...])).astype(BF16)

    @pl.when(c == pl.num_programs(1) - 1)
    def _():
        nconv_ref[0] = tail
        hlast_ref[0] = h


def _lru_seq(gx, row0, bsz, seqlen, tl, conv_buf, h0, cw, cb, wrg, brg, wig, big, lam):
    d = cw.shape[1]
    nt = seqlen // tl
    blk0 = row0 // tl
    row_map = lambda b, c: (blk0 + b * nt + c, 0)
    vec = lambda a: a.reshape(1, d)
    full2 = lambda b, c: (0, 0)
    full3 = lambda b, c: (0, 0, 0)
    return pl.pallas_call(
        _lru_seq_kernel,
        grid=(bsz, nt),
        in_specs=[
            pl.BlockSpec((tl, d), row_map),
            pl.BlockSpec((tl, d), lambda b, c: (blk0 + b * nt + c, 1)),
            pl.BlockSpec((1, CONV_WIDTH - 1, d), lambda b, c: (b, 0, 0)),
            pl.BlockSpec((1, 1, d), lambda b, c: (b, 0, 0)),
            pl.BlockSpec((CONV_WIDTH, d), full2),
            pl.BlockSpec((1, d), full2),
            pl.BlockSpec(wrg.shape, full3),
            pl.BlockSpec((1, d), full2),
            pl.BlockSpec(wig.shape, full3),
            pl.BlockSpec((1, d), full2),
            pl.BlockSpec((1, d), full2),
        ],
        out_specs=[
            pl.BlockSpec((tl, d), lambda b, c: (b * nt + c, 0)),
            pl.BlockSpec((1, CONV_WIDTH - 1, d), lambda b, c: (b, 0, 0)),
            pl.BlockSpec((1, 1, d), lambda b, c: (b, 0, 0)),
        ],
        out_shape=[
            jax.ShapeDtypeStruct((bsz * seqlen, d), BF16),
            jax.ShapeDtypeStruct((bsz, CONV_WIDTH - 1, d), F32),
            jax.ShapeDtypeStruct((bsz, 1, d), F32),
        ],
        scratch_shapes=[
            pltpu.VMEM((tl + CONV_PAD, d), F32),
            pltpu.VMEM((1, d), F32),
            pltpu.VMEM((tl, d), F32),
            pltpu.VMEM((tl, d), F32),
        ],
        compiler_params=_params("arbitrary", "arbitrary"),
        name="lru_seq",
    )(gx, gx, conv_buf, h0.reshape(bsz, 1, d), cw, vec(cb), wrg, vec(brg), wig, vec(big), vec(lam))


def _block_diag_gate(w):
    nb, bw, _ = w.shape
    per = MXU_DIM // bw
    w4 = w.reshape(nb // per, per, bw, bw)
    eye = jnp.eye(per, dtype=w.dtype)
    out = w4[:, :, :, None, :] * eye[None, :, None, :, None]
    return out.reshape(nb // per, MXU_DIM, MXU_DIM).astype(BF16)


def _dot_sel(sel, x, sel_on_left=False, dims=None):
    p1 = x.astype(BF16)
    r1 = x - p1.astype(F32)
    p2 = r1.astype(BF16)
    p3 = (r1 - p2.astype(F32)).astype(BF16)
    out = None
    for p in (p1, p2, p3):
        a, b = (sel, p) if sel_on_left else (p, sel)
        if dims is None:
            term = jnp.dot(a, b, preferred_element_type=F32)
        else:
            term = lax.dot_general(a, b, dims, preferred_element_type=F32)
        out = term if out is None else out + term
    return out


def _ssd_seq_kernel(has_eq, z_ref, xs_ref, bc_ref, dt_ref, cbuf_ref, h0_ref, cw_ref, cb_ref, dtb_ref,
                    alog_ref, dskx_ref, nw_ref, ex_ref, *rest):
    eq_ref = rest[0] if has_eq else None
    out_ref, nconv_ref, st_ref, xx_ref = rest[-4:]
    c = pl.program_id(1)
    q = xs_ref.shape[0]
    d_inner = xs_ref.shape[1]
    n_heads = st_ref.shape[1]
    p_dim = st_ref.shape[2]
    n_state = st_ref.shape[3]
    hpg = n_heads // SSD_GROUPS

    @pl.when(c == 0)
    def _():
        xx_ref[CONV_HIST:CONV_PAD, :] = cbuf_ref[0]
        st_ref[...] = h0_ref[...]

    xx_ref[CONV_PAD:CONV_PAD + q, 0:d_inner] = xs_ref[...]
    xx_ref[CONV_PAD:CONV_PAD + q, d_inner:] = bc_ref[...]
    act = _silu(_conv_tile(xx_ref, q, cw_ref[...], cb_ref[...]))
    tail = xx_ref[q + CONV_HIST:q + CONV_PAD, :]
    xx_ref[CONV_HIST:CONV_PAD, :] = tail

    @pl.when(c == pl.num_programs(1) - 1)
    def _():
        nconv_ref[0] = tail

    dt = _softplus(dt_ref[...] + dtb_ref[...])
    adt = dt * (-jnp.exp(alog_ref[...]))
    row = lax.broadcasted_iota(jnp.int32, (q, q), 0)
    col = lax.broadcasted_iota(jnp.int32, (q, q), 1)
    tril = (col <= row).astype(BF16)
    cs = _dot_sel(tril, adt, sel_on_left=True)
    both_x = _dot_sel(ex_ref[...], jnp.concatenate([dt, cs], axis=0))
    dt_x = both_x[:q]
    cs_x = both_x[q:]
    if eq_ref is None:
        cs_q = cs_x
    else:
        cs_q = _dot_sel(eq_ref[...], cs)

    xs = act[:, :d_inner]
    xd = xs * dt_x
    xd_b = xd.astype(BF16)
    xdec_b = (xd * jnp.exp(cs_x[q - 1:q, :] - cs_x)).astype(BF16)
    ecs_x = jnp.exp(cs_x)

    head = lax.broadcasted_iota(jnp.int32, (q, n_heads), 1)
    even = (head & 1) == 0
    cs_par = jnp.concatenate([jnp.where(even, cs, 0.0), jnp.where(even, 0.0, cs)], axis=0)
    pair_sel = ((lax.broadcasted_iota(jnp.int32, (n_heads // 2, n_heads), 1) >> 1)
                == lax.broadcasted_iota(jnp.int32, (n_heads // 2, n_heads), 0)).astype(BF16)
    rtab = _dot_sel(pair_sel, cs_par, sel_on_left=True, dims=NT_DIMS)
    eye_h = (lax.broadcasted_iota(jnp.int32, (n_heads, n_heads), 0)
             == lax.broadcasted_iota(jnp.int32, (n_heads, n_heads), 1)).astype(BF16)
    cs_t = _dot_sel(eye_h, cs, sel_on_left=True, dims=NT_DIMS)
    st_decay = jnp.exp(jnp.broadcast_to(cs_t[:, q - 1:q], (n_heads, n_state)))

    row2 = lax.broadcasted_iota(jnp.int32, (q, 2 * q), 0)
    col2 = lax.broadcasted_iota(jnp.int32, (q, 2 * q), 1)
    causal2 = jnp.where(col2 >= q, col2 - q, col2) <= row2
    lane_p = lax.broadcasted_iota(jnp.int32, (q, 2 * p_dim), 1)
    lo_half = lane_p < p_dim
    zero_b = jnp.zeros((q, 2 * p_dim), BF16)
    b_off = d_inner
    c_off = d_inner + SSD_GROUPS * n_state
    gp = hpg * p_dim
    pairs_per_group = hpg // 2

    y_diag_parts, y_off_parts = [], []
    for g in range(SSD_GROUPS):
        bg = act[:, b_off + g * n_state:b_off + (g + 1) * n_state].astype(BF16)
        cg = act[:, c_off + g * n_state:c_off + (g + 1) * n_state].astype(BF16)
        cb2 = lax.dot_general(cg, jnp.concatenate([bg, bg], axis=0), NT_DIMS,
                              preferred_element_type=F32)
        prev = st_ref[0, g * hpg:(g + 1) * hpg].reshape(gp, n_state)
        y_off_parts.append(lax.dot_general(cg, prev.astype(BF16), NT_DIMS, preferred_element_type=F32))
        for jj in range(pairs_per_group):
            j = g * pairs_per_group + jj
            lmat = jnp.exp(jnp.where(causal2, cs_q[:, 2 * j * q:2 * (j + 1) * q] - rtab[j:j + 1, :], -jnp.inf))
            m = (cb2 * lmat).astype(BF16)
            pp = xd_b[:, 2 * j * p_dim:2 * (j + 1) * p_dim]
            blockdiag = jnp.concatenate([jnp.where(lo_half, pp, zero_b), jnp.where(lo_half, zero_b, pp)], axis=0)
            y_diag_parts.append(jnp.dot(m, blockdiag, preferred_element_type=F32))
        st_new = lax.dot_general(xdec_b[:, g * gp:(g + 1) * gp], bg, TN_DIMS,
                                 preferred_element_type=F32)
        for e in range(hpg):
            h = g * hpg + e
            st_ref[0, h] = (st_decay[h:h + 1, :] * prev[e * p_dim:(e + 1) * p_dim, :]
                            + st_new[e * p_dim:(e + 1) * p_dim, :])

    y = (jnp.concatenate(y_diag_parts, axis=-1) + jnp.concatenate(y_off_parts, axis=-1) * ecs_x
         + dskx_ref[...] * xs)
    gated = y * _silu(z_ref[...])
    gw = d_inner // SSD_GROUPS
    nw = nw_ref[...]
    for g in range(SSD_GROUPS):
        seg = gated[:, g * gw:(g + 1) * gw]
        ms = jnp.mean(seg * seg, axis=-1, keepdims=True)
        out_ref[:, g * gw:(g + 1) * gw] = (seg * lax.rsqrt(ms + RMS_EPS) * nw[:, g * gw:(g + 1) * gw]).astype(BF16)


def _ssd_seq(zx, dt_raw, row0, bsz, seqlen, q, conv_buf, h0, cw, cb, dtb, alog, dsk, nw):
    n_heads, p_dim, n_state = h0.shape[1:]
    d_inner = n_heads * p_dim
    conv_dim = cw.shape[1]
    nt = seqlen // q
    blk0 = row0 // q
    full2 = lambda b, c: (0, 0)
    col = lambda j: (lambda b, c: (blk0 + b * nt + c, j))
    head_ids = jnp.arange(n_heads, dtype=jnp.int32)[:, None]
    expand = lambda width: (jnp.arange(n_heads * width, dtype=jnp.int32)[None, :] // width == head_ids).astype(BF16)
    has_eq = q != p_dim
    extra_in = [expand(q)] if has_eq else []
    extra_specs = [pl.BlockSpec((n_heads, n_heads * q), full2)] if has_eq else []
    return pl.pallas_call(
        functools.partial(_ssd_seq_kernel, has_eq),
        grid=(bsz, nt),
        in_specs=[
            pl.BlockSpec((q, d_inner), col(0)),
            pl.BlockSpec((q, d_inner), col(1)),
            pl.BlockSpec((q, d_inner), col(2)),
            pl.BlockSpec((q, n_heads), col(0)),
            pl.BlockSpec((1, CONV_WIDTH - 1, conv_dim), lambda b, c: (b, 0, 0)),
            pl.BlockSpec((1, n_heads, p_dim, n_state), lambda b, c: (b, 0, 0, 0)),
            pl.BlockSpec((CONV_WIDTH, conv_dim), full2),
            pl.BlockSpec((1, conv_dim), full2),
            pl.BlockSpec((1, n_heads), full2),
            pl.BlockSpec((1, n_heads), full2),
            pl.BlockSpec((1, d_inner), full2),
            pl.BlockSpec((1, d_inner), full2),
            pl.BlockSpec((n_heads, d_inner), full2),
        ] + extra_specs,
        out_specs=[
            pl.BlockSpec((q, d_inner), lambda b, c: (b * nt + c, 0)),
            pl.BlockSpec((1, CONV_WIDTH - 1, conv_dim), lambda b, c: (b, 0, 0)),
            pl.BlockSpec((1, n_heads, p_dim, n_state), lambda b, c: (b, 0, 0, 0)),
        ],
        out_shape=[
            jax.ShapeDtypeStruct((bsz * seqlen, d_inner), BF16),
            jax.ShapeDtypeStruct((bsz, CONV_WIDTH - 1, conv_dim), F32),
            jax.ShapeDtypeStruct((bsz, n_heads, p_dim, n_state), F32),
        ],
        scratch_shapes=[
            pltpu.VMEM((q + CONV_PAD, conv_dim), F32),
        ],
        compiler_params=_params("arbitrary", "arbitrary"),
        name="ssd_seq",
    )(zx, zx, zx, dt_raw, conv_buf, h0, cw, cb.reshape(1, -1), dtb.reshape(1, -1),
      alog.reshape(1, -1), jnp.repeat(dsk, p_dim).reshape(1, -1), nw.reshape(1, -1), expand(p_dim), *extra_in)


def _first_argmax(v, rid, n):
    m = jnp.max(v, axis=0, keepdims=True)
    idx = jnp.min(jnp.where(v == m, rid, float(n)), axis=0, keepdims=True)
    return m, idx


def _route_tile(scores_t, bias_col):
    n_exp, tm = scores_t.shape
    per = n_exp // N_EXPERT_GROUPS
    neg = -jnp.inf
    biased = scores_t + bias_col
    rid_g = lax.broadcasted_iota(jnp.int32, (per, tm), 0).astype(F32)
    tiles, gs_rows = [], []
    for g in range(N_EXPERT_GROUPS):
        v = biased[g * per:(g + 1) * per, :]
        tiles.append(v)
        m1, i1 = _first_argmax(v, rid_g, per)
        m2 = jnp.max(jnp.where(rid_g == i1, neg, v), axis=0, keepdims=True)
        gs_rows.append(m1 + m2)
    gs = jnp.concatenate(gs_rows, axis=0)
    rid_grp = lax.broadcasted_iota(jnp.int32, (N_EXPERT_GROUPS, tm), 0).astype(F32)
    gsel = jnp.zeros((N_EXPERT_GROUPS, tm), F32)
    for _ in range(TOPK_GROUPS):
        _, gi = _first_argmax(gs, rid_grp, N_EXPERT_GROUPS)
        hit = rid_grp == gi
        gsel = jnp.where(hit, 1.0, gsel)
        gs = jnp.where(hit, neg, gs)
    masked = jnp.concatenate(
        [jnp.where(gsel[g:g + 1, :] > 0.0, tiles[g], neg) for g in range(N_EXPERT_GROUPS)], axis=0)
    rid = lax.broadcasted_iota(jnp.int32, (n_exp, tm), 0).astype(F32)
    sel = jnp.zeros((n_exp, tm), F32)
    ids, picked = [], []
    for _ in range(TOP_K):
        _, ei = _first_argmax(masked, rid, n_exp)
        hit = rid == ei
        ids.append(ei)
        picked.append(jnp.sum(jnp.where(hit, scores_t, 0.0), axis=0, keepdims=True))
        sel = jnp.where(hit, 1.0, sel)
        masked = jnp.where(hit, neg, masked)
    total = picked[0]
    for p in picked[1:]:
        total = total + p
    gates = [p / total * ROUTE_SCALE for p in picked]
    return sel, rid, ids, gates


def _proj_ln_router_kernel(alpha, h_ref, w_ref, x_ref, g_ref, b_ref, rwt_ref, rb_ref,
                           o_ref, op_ref, eid_ref, gate_ref, rank_ref, cnt_ref, wb_ref, tri_ref):
    i = pl.program_id(0)
    tm = x_ref.shape[0]

    @pl.when(i == 0)
    def _():
        wb_ref[...] = w_ref[...].astype(BF16)
        r = lax.broadcasted_iota(jnp.int32, (tm, tm), 0)
        c = lax.broadcasted_iota(jnp.int32, (tm, tm), 1)
        tri_ref[...] = jnp.where(r < c, 1.0, 0.0).astype(BF16)
        cnt_ref[...] = jnp.zeros_like(cnt_ref)

    mix = jnp.dot(h_ref[...], wb_ref[...], preferred_element_type=F32)
    x1 = _layer_norm(alpha * x_ref[...] + mix, g_ref[...], b_ref[...])
    o_ref[...] = x1
    half = x1.shape[1] // 2
    op_ref[...] = _pack_pair(x1[:, :half], x1[:, half:])
    logits_t = lax.dot_general(rwt_ref[...].astype(BF16), x1.astype(BF16), NT_DIMS,
                               preferred_element_type=F32)
    sel, rid, ids, gates = _route_tile(_sigmoid(logits_t), rb_ref[...])

    sel_b = sel.astype(BF16)
    before = jnp.dot(sel_b, tri_ref[...], preferred_element_type=F32) + cnt_ref[:, 0:1]
    pad_rows = eid_ref.shape[0] - TOP_K
    zrow = jnp.zeros((pad_rows, tm), F32)
    ranks = [jnp.sum(jnp.where(rid == ei, before, 0.0), axis=0, keepdims=True) for ei in ids]
    eid_ref[...] = jnp.concatenate(ids + [zrow], axis=0).astype(jnp.int32)
    gate_ref[...] = jnp.concatenate(gates + [zrow], axis=0)
    rank_ref[...] = jnp.concatenate(ranks + [zrow], axis=0).astype(jnp.int32)
    ones = jnp.ones((tm, cnt_ref.shape[1]), BF16)
    cnt_ref[...] = cnt_ref[...] + jnp.dot(sel_b, ones, preferred_element_type=F32)


def _proj_ln_router(h, w, wl, x, alpha, ln_g, ln_b, router_w_t, router_bias_col):
    T, K = h.shape
    D = x.shape[1]
    E = router_w_t.shape[0]
    tm = TM_PROJ
    kp = SUBLANE
    row_spec = pl.BlockSpec((kp, tm), lambda i: (0, i))
    return pl.pallas_call(
        functools.partial(_proj_ln_router_kernel, alpha),
        grid=(T // tm,),
        in_specs=[
            pl.BlockSpec((tm, K), lambda i: (i, 0)),
            pl.BlockSpec((None, K, D), lambda i: (wl, 0, 0)),
            pl.BlockSpec((tm, D), lambda i: (i, 0)),
            pl.BlockSpec((1, D), lambda i: (0, 0)),
            pl.BlockSpec((1, D), lambda i: (0, 0)),
            pl.BlockSpec((E, D), lambda i: (0, 0)),
            pl.BlockSpec((E, 1), lambda i: (0, 0)),
        ],
        out_specs=[pl.BlockSpec((tm, D), lambda i: (i, 0)), pl.BlockSpec((tm, D // 2), lambda i: (i, 0)),
                   row_spec, row_spec, row_spec, pl.BlockSpec((E, LANE), lambda i: (0, 0))],
        out_shape=[jax.ShapeDtypeStruct((T, D), F32),
                   jax.ShapeDtypeStruct((T, D // 2), jnp.uint32),
                   jax.ShapeDtypeStruct((kp, T), jnp.int32),
                   jax.ShapeDtypeStruct((kp, T), F32),
                   jax.ShapeDtypeStruct((kp, T), jnp.int32),
                   jax.ShapeDtypeStruct((E, LANE), F32)],
        scratch_shapes=[pltpu.VMEM((K, D), BF16), pltpu.VMEM((tm, tm), BF16)],
        compiler_params=_params("arbitrary"),
        name="proj_ln_router",
    )(h, w, x, ln_g.reshape(1, D), ln_b.reshape(1, D), router_w_t, router_bias_col)


def _start_row_gather(idx_ref, base, n, src_hbm, dst, sem):
    for i in range(n):
        r = idx_ref[base + i]
        pltpu.make_async_copy(src_hbm.at[pl.ds(r, 1), :], dst.at[pl.ds(i, 1), :], sem).start(priority=i % 2)


def _wait_row_gather(n, src_hbm, dst, sem):
    pltpu.make_async_copy(src_hbm.at[pl.ds(0, n), :], dst, sem).wait()


def _scatter_kernel(n_tiles, pos_ref, zb_ref, nu_ref, x_ref, xs_hbm, xbuf, zbuf, sem, zsem):
    i = pl.program_id(0)
    tm = x_ref.shape[0]
    n_rows = TOP_K * tm
    bm = zbuf.shape[0]
    n_experts = zb_ref.shape[0]
    n_blocks = xs_hbm.shape[0] // bm
    slot = lax.rem(i, 2)

    def zero_block(blk):
        start = pl.multiple_of(blk * bm, bm)
        return pltpu.make_async_copy(zbuf, xs_hbm.at[pl.ds(start, bm), :], zsem)

    def wait_rows(s):
        pltpu.make_async_copy(xs_hbm.at[pl.ds(0, n_rows), :], xs_hbm.at[pl.ds(0, n_rows), :], sem.at[s]).wait()

    @pl.when(i == 0)
    def _():
        zbuf[...] = jnp.zeros_like(zbuf)
        for e in range(n_experts):
            @pl.when(zb_ref[e] >= 0)
            def _():
                zero_block(zb_ref[e]).start()

        def start_tail(blk, carry):
            zero_block(blk).start()
            return carry

        def wait_tail(blk, carry):
            zero_block(blk).wait()
            return carry

        lax.fori_loop(nu_ref[0], n_blocks, start_tail, 0)
        for e in range(n_experts):
            @pl.when(zb_ref[e] >= 0)
            def _():
                zero_block(zb_ref[e]).wait()
        lax.fori_loop(nu_ref[0], n_blocks, wait_tail, 0)

    @pl.when(i >= 2)
    def _():
        wait_rows(slot)

    xbuf[slot] = x_ref[...]
    for k in range(TOP_K):
        for t in range(tm):
            r = pos_ref[i * n_rows + k * tm + t]
            pltpu.make_async_copy(xbuf.at[slot, pl.ds(t, 1), :], xs_hbm.at[pl.ds(r, 1), :],
                                  sem.at[slot]).start(priority=t % 2)

    @pl.when(i == n_tiles - 1)
    def _():
        wait_rows(slot)
        if n_tiles >= 2:
            wait_rows(1 - slot)


def _scatter_rows(x, pos_tiles, zero_blk, n_used, n_slots):
    T, D = x.shape
    tm = TM_COMBINE
    n_tiles = T // tm
    grid_spec = pltpu.PrefetchScalarGridSpec(
        num_scalar_prefetch=3,
        grid=(n_tiles,),
        in_specs=[pl.BlockSpec((tm, D), lambda i, pos, zb, nu: (i, 0))],
        out_specs=pl.BlockSpec(memory_space=pl.ANY),
        scratch_shapes=[
            pltpu.VMEM((2, tm, D), x.dtype),
            pltpu.VMEM((BM_EXPERT, D), x.dtype),
            pltpu.SemaphoreType.DMA((2,)),
            pltpu.SemaphoreType.DMA,
        ],
    )
    return pl.pallas_call(
        functools.partial(_scatter_kernel, n_tiles),
        grid_spec=grid_spec,
        out_shape=jax.ShapeDtypeStruct((n_slots, D), x.dtype),
        compiler_params=_params("arbitrary"),
        name="scatter_rows",
    )(pos_tiles, zero_blk, n_used, x)


def _expert_kernel(be_ref, nu_ref, x_ref, wg_ref, wu_ref, wd_ref, y_ref, wgb, wub, wdb):
    b = pl.program_id(0)
    n_used = nu_ref[0]

    @pl.when(b < n_used)
    def _():
        new_expert = jnp.logical_or(b == 0, be_ref[b] != be_ref[jnp.maximum(b - 1, 0)])

        @pl.when(new_expert)
        def _():
            wgb[...] = wg_ref[...].astype(BF16)
            wub[...] = wu_ref[...].astype(BF16)
            wdb[...] = wd_ref[...].astype(BF16)

        half = x_ref.shape[1]
        x_lo, x_hi = _unpack_pair(x_ref[...])
        x_lo = x_lo.astype(BF16)
        x_hi = x_hi.astype(BF16)
        hg = (jnp.dot(x_lo, wgb[0:half, :], preferred_element_type=F32)
              + jnp.dot(x_hi, wgb[half:, :], preferred_element_type=F32))
        hu = (jnp.dot(x_lo, wub[0:half, :], preferred_element_type=F32)
              + jnp.dot(x_hi, wub[half:, :], preferred_element_type=F32))
        hh = (_silu(hg) * hu).astype(BF16)
        y = jnp.dot(hh, wdb[...], preferred_element_type=F32)
        y_ref[...] = _pack_pair(y[:, :half], y[:, half:])

    @pl.when(b >= n_used)
    def _():
        y_ref[...] = jnp.zeros_like(y_ref)


def _experts(x_sorted, block_e, n_used, wg, wu, wd, layer):
    n_slots, dh = x_sorted.shape
    D, DE = wg.shape[2:]
    bm = BM_EXPERT
    nb = n_slots // bm
    w_map = lambda b, be, nu: (layer, be[b], 0, 0)
    grid_spec = pltpu.PrefetchScalarGridSpec(
        num_scalar_prefetch=2,
        grid=(nb,),
        in_specs=[
            pl.BlockSpec((bm, dh), lambda b, be, nu: (jnp.minimum(b, nu[0] - 1), 0)),
            pl.BlockSpec((None, None, D, DE), w_map),
            pl.BlockSpec((None, None, D, DE), w_map),
            pl.BlockSpec((None, None, DE, D), w_map),
        ],
        out_specs=pl.BlockSpec((bm, dh), lambda b, be, nu: (b, 0)),
        scratch_shapes=[
            pltpu.VMEM((D, DE), BF16),
            pltpu.VMEM((D, DE), BF16),
            pltpu.VMEM((DE, D), BF16),
        ],
    )
    return pl.pallas_call(
        _expert_kernel,
        grid_spec=grid_spec,
        out_shape=jax.ShapeDtypeStruct((n_slots, dh), jnp.uint32),
        compiler_params=_params("arbitrary"),
        name="experts",
    )(block_e, n_used, x_sorted, wg, wu, wd)


def _combine_kernel(alpha, n_tiles, pos_ref, gates_ref, x_ref, y_hbm, sg_ref, su_ref, sd_ref, g_ref, b_ref,
                    o_ref, ybuf_a, ybuf_b, sgb, sub, sdb, sem):
    i = pl.program_id(0)
    tm = x_ref.shape[0] // 2
    n_rows = ybuf_a.shape[0]
    first = 2 * i

    @pl.when(i == 0)
    def _():
        _start_row_gather(pos_ref, 0, n_rows, y_hbm, ybuf_a, sem.at[0])
        sgb[...] = sg_ref[...].astype(BF16)
        sub[...] = su_ref[...].astype(BF16)
        sdb[...] = sd_ref[...].astype(BF16)

    def tile(r0, ybuf):
        x = x_ref[r0:r0 + tm, :]
        xb = x.astype(BF16)
        hg = jnp.dot(xb, sgb[...], preferred_element_type=F32)
        hu = jnp.dot(xb, sub[...], preferred_element_type=F32)
        acc = jnp.dot((_silu(hg) * hu).astype(BF16), sdb[...], preferred_element_type=F32)
        gates = gates_ref[r0:r0 + tm, :]
        routed_lo = routed_hi = None
        for k in range(TOP_K):
            y_lo, y_hi = _unpack_pair(ybuf[k * tm:(k + 1) * tm, :])
            gk = gates[:, k:k + 1]
            routed_lo = gk * y_lo if routed_lo is None else routed_lo + gk * y_lo
            routed_hi = gk * y_hi if routed_hi is None else routed_hi + gk * y_hi
        acc = acc + jnp.concatenate([routed_lo, routed_hi], axis=-1)
        o_ref[r0:r0 + tm, :] = _layer_norm(alpha * x + acc, g_ref[...], b_ref[...])

    _wait_row_gather(n_rows, y_hbm, ybuf_a, sem.at[0])
    _start_row_gather(pos_ref, (first + 1) * n_rows, n_rows, y_hbm, ybuf_b, sem.at[1])
    tile(0, ybuf_a)
    _wait_row_gather(n_rows, y_hbm, ybuf_b, sem.at[1])
    nxt = jnp.minimum(first + 2, n_tiles - 2)
    _start_row_gather(pos_ref, nxt * n_rows, n_rows, y_hbm, ybuf_a, sem.at[0])
    tile(tm, ybuf_b)

    @pl.when(i == pl.num_programs(0) - 1)
    def _():
        _wait_row_gather(n_rows, y_hbm, ybuf_a, sem.at[0])


def _combine(x, y_sorted, pos_tiles, gates, sg, su, sd, alpha, ln_g, ln_b, layer):
    T, D = x.shape
    DS = sg.shape[2]
    tm = TM_COMBINE
    n_tiles = T // tm
    assert n_tiles % 2 == 0, "combine walks token tiles in pairs"
    grid_spec = pltpu.PrefetchScalarGridSpec(
        num_scalar_prefetch=1,
        grid=(n_tiles // 2,),
        in_specs=[
            pl.BlockSpec((2 * tm, TOP_K), lambda i, pos: (i, 0)),
            pl.BlockSpec((2 * tm, D), lambda i, pos: (i, 0)),
            pl.BlockSpec(memory_space=pl.ANY),
            pl.BlockSpec((None, D, DS), lambda i, pos: (layer, 0, 0)),
            pl.BlockSpec((None, D, DS), lambda i, pos: (layer, 0, 0)),
            pl.BlockSpec((None, DS, D), lambda i, pos: (layer, 0, 0)),
            pl.BlockSpec((None, 1, D), lambda i, pos: (layer, 0, 0)),
            pl.BlockSpec((None, 1, D), lambda i, pos: (layer, 0, 0)),
        ],
        out_specs=pl.BlockSpec((2 * tm, D), lambda i, pos: (i, 0)),
        scratch_shapes=[
            pltpu.VMEM((TOP_K * tm, y_sorted.shape[1]), y_sorted.dtype),
            pltpu.VMEM((TOP_K * tm, y_sorted.shape[1]), y_sorted.dtype),
            pltpu.VMEM((D, DS), BF16),
            pltpu.VMEM((D, DS), BF16),
            pltpu.VMEM((DS, D), BF16),
            pltpu.SemaphoreType.DMA((2,)),
        ],
    )
    return pl.pallas_call(
        functools.partial(_combine_kernel, alpha, n_tiles),
        grid_spec=grid_spec,
        out_shape=jax.ShapeDtypeStruct((T, D), F32),
        compiler_params=_params("arbitrary"),
        name="combine",
    )(pos_tiles, gates, x, y_sorted, sg, su, sd, ln_g.reshape(-1, 1, D), ln_b.reshape(-1, 1, D))


def _dispatch(eid_t, rank_t, counts):
    T = eid_t.shape[1]
    E = counts.shape[0]
    A = T * TOP_K
    bm = BM_EXPERT
    nb = -(-A // bm) + E
    cnt = counts[:, 0].astype(jnp.int32)
    padded = (cnt + bm - 1) // bm * bm
    pends = jnp.cumsum(padded)
    pstart = pends - padded
    eid = eid_t[:TOP_K]
    onehot = eid[:, :, None] == jnp.arange(E, dtype=jnp.int32)[None, None, :]
    dest = rank_t[:TOP_K] + jnp.sum(jnp.where(onehot, pstart[None, None, :], 0), axis=-1)
    block_start = jnp.arange(nb, dtype=jnp.int32) * bm
    block_e = jnp.minimum(jnp.sum((pends[None, :] <= block_start[:, None]).astype(jnp.int32), axis=1), E - 1)
    n_used = (pends[-1:] // bm).astype(jnp.int32)
    zero_blk = jnp.where(padded > 0, pends // bm - 1, -1).astype(jnp.int32)
    tm = TM_COMBINE
    pos_tiles = dest.reshape(TOP_K, T // tm, tm).transpose(1, 0, 2).reshape(-1)
    return nb * bm, block_e, n_used, zero_blk, pos_tiles


def _moe(x1, x1_packed, eid_t, gate_t, rank_t, counts, alpha, layer, wg, wu, wd, sg, su, sd, ln_g, ln_b):
    n_slots, block_e, n_used, zero_blk, pos_tiles = _dispatch(eid_t, rank_t, counts)
    x_sorted = _scatter_rows(x1_packed, pos_tiles, zero_blk, n_used, n_slots)
    y_sorted = _experts(x_sorted, block_e, n_used, wg, wu, wd, layer)
    gates = gate_t[:TOP_K].T
    return _combine(x1, y_sorted, pos_tiles, gates, sg, su, sd, alpha, ln_g, ln_b, layer)


def kernel(x_prompt, x_sample, state_lru_conv, state_lru_h, state_ssd_conv, state_ssd, lru_w_in, lru_conv_w, lru_conv_b, lru_w_rgate, lru_b_rgate, lru_w_igate, lru_b_igate, lru_lambda, lru_w_out, ssd_w_in, ssd_conv_w, ssd_conv_b, ssd_dt_bias, ssd_a_log, ssd_d, ssd_norm_w, ssd_w_out, ln_mix_g, ln_mix_b, ln_ffn_g, ln_ffn_b, router_w, router_bias, moe_w_gate, moe_w_up, moe_w_down, shared_w_gate, shared_w_up, shared_w_down):
    bp, lp, d_model = x_prompt.shape
    bs, ls, _ = x_sample.shape
    depth = ln_mix_g.shape[0]
    alpha = (2.0 * depth) ** 0.25
    tp, ts = bp * lp, bs * ls
    n_heads, p_dim, n_state = state_ssd.shape[2:]
    d_inner = n_heads * p_dim
    conv_dim = ssd_conv_w.shape[2]
    d_rnn = lru_conv_w.shape[2]

    x = jnp.concatenate([x_prompt.reshape(tp, d_model), x_sample.reshape(ts, d_model)], axis=0)
    tl_p = min(TL_LRU, lp)
    tl_s = min(TL_LRU, ls)
    q_p = min(SSD_CHUNK, lp)
    q_s = min(SSD_CHUNK, ls)

    p_lru_conv, p_lru_h, p_ssd_conv, p_ssd = [], [], [], []
    s_lru_conv, s_lru_h, s_ssd_conv, s_ssd = [], [], [], []
    for i in range(depth):
        j = i // 2
        if i % 2 == 0:
            gx = _in_proj(x, lru_w_in, j, 2 * d_rnn, TN_LRU_IN)
            wrg = _block_diag_gate(lru_w_rgate[j])
            wig = _block_diag_gate(lru_w_igate[j])
            common = (lru_conv_w[j], lru_conv_b[j], wrg, lru_b_rgate[j], wig, lru_b_igate[j], lru_lambda[j])
            hp, cp, lp_h = _lru_seq(gx, 0, bp, lp, tl_p, jnp.zeros((bp, CONV_WIDTH - 1, d_rnn), F32),
                                    jnp.zeros((bp, d_rnn), F32), *common)
            hs, cs_, ls_h = _lru_seq(gx, tp, bs, ls, tl_s, state_lru_conv[j], state_lru_h[j], *common)
            p_lru_conv.append(cp)
            p_lru_h.append(lp_h.reshape(bp, d_rnn))
            s_lru_conv.append(cs_)
            s_lru_h.append(ls_h.reshape(bs, d_rnn))
            mixed = jnp.concatenate([hp, hs], axis=0)
            w_out, wl = lru_w_out, j
        else:
            zx = _in_proj(x, ssd_w_in, j, d_inner + conv_dim, TN_SSD_IN)
            dt_raw = _small_proj(x, ssd_w_in[j][:, d_inner + conv_dim:])
            common = (ssd_conv_w[j], ssd_conv_b[j], ssd_dt_bias[j], ssd_a_log[j], ssd_d[j], ssd_norm_w[j])
            hp, cp, sp = _ssd_seq(zx, dt_raw, 0, bp, lp, q_p, jnp.zeros((bp, CONV_WIDTH - 1, conv_dim), F32),
                                  jnp.zeros((bp, n_heads, p_dim, n_state), F32), *common)
            hs, cs_, ss = _ssd_seq(zx, dt_raw, tp, bs, ls, q_s, state_ssd_conv[j], state_ssd[j], *common)
            p_ssd_conv.append(cp)
            p_ssd.append(sp)
            s_ssd_conv.append(cs_)
            s_ssd.append(ss)
            mixed = jnp.concatenate([hp, hs], axis=0)
            w_out, wl = ssd_w_out, j
        x1, x1_packed, eid_t, gate_t, rank_t, counts = _proj_ln_router(
            mixed, w_out, wl, x, alpha, ln_mix_g[i], ln_mix_b[i], router_w[i].T, router_bias[i].reshape(-1, 1))
        x = _moe(x1, x1_packed, eid_t, gate_t, rank_t, counts, alpha, i, moe_w_gate, moe_w_up, moe_w_down,
                 shared_w_gate, shared_w_up, shared_w_down, ln_ffn_g, ln_ffn_b)

    y_prompt = x[:tp].reshape(bp, lp, d_model)
    y_sample = x[tp:].reshape(bs, ls, d_model)
    return (y_prompt, y_sample,
            jnp.stack(p_lru_conv), jnp.stack(p_lru_h), jnp.stack(p_ssd_conv), jnp.stack(p_ssd),
            jnp.stack(s_lru_conv), jnp.stack(s_lru_h), jnp.stack(s_ssd_conv), jnp.stack(s_ssd))
```

```python
import functools
import math

import jax
import jax.numpy as jnp
from jax import lax
from jax.experimental import pallas as pl
from jax.experimental.pallas import tpu as pltpu

F32 = jnp.float32
BF16 = jnp.bfloat16

LN_EPS = 1e-5
RMS_EPS = 1e-5
CONV_WIDTH = 4
LRU_C = 8.0
SSD_CHUNK = 64
SSD_GROUPS = 8
TOP_K = 6
N_EXPERT_GROUPS = 8
TOPK_GROUPS = 4
ROUTE_SCALE = 1.0

LANE = 128
SUBLANE = 8
MXU_DIM = 256
VMEM_LIMIT = 56 * 1024 * 1024

TM_PROJ = 512
TM_COMBINE = 128
BM_EXPERT = 512
TL_LRU = 256
TN_LRU_IN = 1024
TN_SSD_IN = 1536

HIGHEST = lax.Precision.HIGHEST
NT_DIMS = (((1,), (1,)), ((), ()))
TN_DIMS = (((0,), (0,)), ((), ()))


def _sigmoid(x):
    return 1.0 / (1.0 + jnp.exp(-x))


def _silu(x):
    return x * _sigmoid(x)


def _softplus(x):
    return jnp.maximum(x, 0.0) + jnp.log1p(jnp.exp(-jnp.abs(x)))


def _gelu_tanh(x):
    c = math.sqrt(2.0 / math.pi)
    return 0.5 * x * (1.0 + jnp.tanh(c * (x + 0.044715 * (x * x * x))))


def _layer_norm(v, g, b):
    mu = jnp.mean(v, axis=-1, keepdims=True)
    d = v - mu
    var = jnp.mean(d * d, axis=-1, keepdims=True)
    return d * lax.rsqrt(var + LN_EPS) * g + b


def _pack_pair(lo, hi):
    lo_bits = lax.bitcast_convert_type(lo.astype(BF16).astype(F32), jnp.uint32)
    hi_bits = lax.bitcast_convert_type(hi.astype(BF16).astype(F32), jnp.uint32)
    return hi_bits | (lo_bits >> 16)


def _unpack_pair(w):
    lo = lax.bitcast_convert_type(w << 16, F32)
    hi = lax.bitcast_convert_type(w & jnp.uint32(0xFFFF0000), F32)
    return lo, hi


def _params(*sem):
    return pltpu.CompilerParams(dimension_semantics=sem, vmem_limit_bytes=VMEM_LIMIT)


def _mm_kernel(x_ref, w_ref, o_ref):
    o_ref[...] = jnp.dot(x_ref[...].astype(BF16), w_ref[...].astype(BF16),
                         preferred_element_type=F32)


def _in_proj_kernel(x_ref, w_ref, o_ref, wb_ref):
    @pl.when(pl.program_id(1) == 0)
    def _():
        wb_ref[...] = w_ref[...].astype(BF16)

    o_ref[...] = jnp.dot(x_ref[...].astype(BF16), wb_ref[...], preferred_element_type=F32)


def _in_proj(x, w, layer, n_cols, tn):
    T, K = x.shape
    tm = TM_PROJ
    return pl.pallas_call(
        _in_proj_kernel,
        grid=(n_cols // tn, T // tm),
        in_specs=[pl.BlockSpec((tm, K), lambda j, i: (i, 0)),
                  pl.BlockSpec((None, K, tn), lambda j, i: (layer, 0, j))],
        out_specs=pl.BlockSpec((tm, tn), lambda j, i: (i, j)),
        out_shape=jax.ShapeDtypeStruct((T, n_cols), F32),
        scratch_shapes=[pltpu.VMEM((K, tn), BF16)],
        compiler_params=_params("arbitrary", "arbitrary"),
        name="in_proj",
    )(x, w)


def _small_proj(x, w):
    T, K = x.shape
    N = w.shape[1]
    tm = TM_PROJ
    return pl.pallas_call(
        _mm_kernel,
        grid=(T // tm,),
        in_specs=[pl.BlockSpec((tm, K), lambda i: (i, 0)),
                  pl.BlockSpec((K, N), lambda i: (0, 0))],
        out_specs=pl.BlockSpec((tm, N), lambda i: (i, 0)),
        out_shape=jax.ShapeDtypeStruct((T, N), F32),
        compiler_params=_params("parallel"),
        name="dt_proj",
    )(x, w)


CONV_PAD = SUBLANE
CONV_HIST = CONV_PAD - (CONV_WIDTH - 1)


def _conv_tile(xx_ref, n, cw, cb):
    acc = cb + cw[CONV_WIDTH - 1:CONV_WIDTH] * xx_ref[CONV_PAD:CONV_PAD + n, :]
    for k in range(CONV_WIDTH - 1):
        off = CONV_HIST + k
        acc = acc + cw[k:k + 1] * xx_ref[off:off + n, :]
    return acc


def _lru_seq_kernel(gate_ref, xr_ref, cbuf_ref, h0_ref, cw_ref, cb_ref, wrg_ref, brg_ref,
                    wig_ref, big_ref, lam_ref, out_ref, nconv_ref, hlast_ref,
                    xx_ref, h_ref, a_ref, u_ref):
    c = pl.program_id(1)
    n = xr_ref.shape[0]
    d = xr_ref.shape[1]

    @pl.when(c == 0)
    def _():
        xx_ref[CONV_HIST:CONV_PAD, :] = cbuf_ref[0]
        h_ref[...] = h0_ref[0]

    xx_ref[CONV_PAD:CONV_PAD + n, :] = xr_ref[...]
    xr = _conv_tile(xx_ref, n, cw_ref[...], cb_ref[...])
    tail = xx_ref[n + CONV_HIST:n + CONV_PAD, :]
    xx_ref[CONV_HIST:CONV_PAD, :] = tail

    xb = xr.astype(BF16)
    nblk = d // MXU_DIM
    r_parts, i_parts = [], []
    for g in range(nblk):
        seg = xb[:, g * MXU_DIM:(g + 1) * MXU_DIM]
        r_parts.append(jnp.dot(seg, wrg_ref[g], preferred_element_type=F32))
        i_parts.append(jnp.dot(seg, wig_ref[g], preferred_element_type=F32))
    r = _sigmoid(jnp.concatenate(r_parts, axis=-1) + brg_ref[...])
    ig = _sigmoid(jnp.concatenate(i_parts, axis=-1) + big_ref[...])
    log_sig_lam = -_softplus(-lam_ref[...])
    log_a = LRU_C * r * log_sig_lam
    a = jnp.exp(log_a)
    a_ref[...] = a
    u_ref[...] = jnp.sqrt(-jnp.tanh(log_a) * (a * a + 1.0)) * (ig * xr)

    def step(t, h):
        h = a_ref[pl.ds(t, 1), :] * h + u_ref[pl.ds(t, 1), :]
        u_ref[pl.ds(t, 1), :] = h
        return h

    h = lax.fori_loop(0, n, step, h_ref[...], unroll=8)
    h_ref[...] = h
    out_ref[...] = (u_ref[...] * _gelu_tanh(gate_ref[...])).astype(BF16)

    @pl.when(c == pl.num_programs(1) - 1)
    def _():
        nconv_ref[0] = tail
        hlast_ref[0] = h


def _lru_seq(gx, row0, bsz, seqlen, tl, conv_buf, h0, cw, cb, wrg, brg, wig, big, lam):
    d = cw.shape[1]
    nt = seqlen // tl
    blk0 = row0 // tl
    row_map = lambda b, c: (blk0 + b * nt + c, 0)
    vec = lambda a: a.reshape(1, d)
    full2 = lambda b, c: (0, 0)
    full3 = lambda b, c: (0, 0, 0)
    return pl.pallas_call(
        _lru_seq_kernel,
        grid=(bsz, nt),
        in_specs=[
            pl.BlockSpec((tl, d), row_map),
            pl.BlockSpec((tl, d), lambda b, c: (blk0 + b * nt + c, 1)),
            pl.BlockSpec((1, CONV_WIDTH - 1, d), lambda b, c: (b, 0, 0)),
            pl.BlockSpec((1, 1, d), lambda b, c: (b, 0, 0)),
            pl.BlockSpec((CONV_WIDTH, d), full2),
            pl.BlockSpec((1, d), full2),
            pl.BlockSpec(wrg.shape, full3),
            pl.BlockSpec((1, d), full2),
            pl.BlockSpec(wig.shape, full3),
            pl.BlockSpec((1, d), full2),
            pl.BlockSpec((1, d), full2),
        ],
        out_specs=[
            pl.BlockSpec((tl, d), lambda b, c: (b * nt + c, 0)),
            pl.BlockSpec((1, CONV_WIDTH - 1, d), lambda b, c: (b, 0, 0)),
            pl.BlockSpec((1, 1, d), lambda b, c: (b, 0, 0)),
        ],
        out_shape=[
            jax.ShapeDtypeStruct((bsz * seqlen, d), BF16),
            jax.ShapeDtypeStruct((bsz, CONV_WIDTH - 1, d), F32),
            jax.ShapeDtypeStruct((bsz, 1, d), F32),
        ],
        scratch_shapes=[
            pltpu.VMEM((tl + CONV_PAD, d), F32),
            pltpu.VMEM((1, d), F32),
            pltpu.VMEM((tl, d), F32),
            pltpu.VMEM((tl, d), F32),
        ],
        compiler_params=_params("arbitrary", "arbitrary"),
        name="lru_seq",
    )(gx, gx, conv_buf, h0.reshape(bsz, 1, d), cw, vec(cb), wrg, vec(brg), wig, vec(big), vec(lam))


def _block_diag_gate(w):
    nb, bw, _ = w.shape
    per = MXU_DIM // bw
    w4 = w.reshape(nb // per, per, bw, bw)
    eye = jnp.eye(per, dtype=w.dtype)
    out = w4[:, :, :, None, :] * eye[None, :, None, :, None]
    return out.reshape(nb // per, MXU_DIM, MXU_DIM).astype(BF16)


def _dot_sel(sel, x, sel_on_left=False, dims=None):
    p1 = x.astype(BF16)
    r1 = x - p1.astype(F32)
    p2 = r1.astype(BF16)
    p3 = (r1 - p2.astype(F32)).astype(BF16)
    out = None
    for p in (p1, p2, p3):
        a, b = (sel, p) if sel_on_left else (p, sel)
        if dims is None:
            term = jnp.dot(a, b, preferred_element_type=F32)
        else:
            term = lax.dot_general(a, b, dims, preferred_element_type=F32)
        out = term if out is None else out + term
    return out


def _ssd_seq_kernel(has_eq, z_ref, xs_ref, bc_ref, dt_ref, cbuf_ref, h0_ref, cw_ref, cb_ref, dtb_ref,
                    alog_ref, dskx_ref, nw_ref, ex_ref, *rest):
    eq_ref = rest[0] if has_eq else None
    out_ref, nconv_ref, st_ref, xx_ref = rest[-4:]
    c = pl.program_id(1)
    q = xs_ref.shape[0]
    d_inner = xs_ref.shape[1]
    n_heads = st_ref.shape[1]
    p_dim = st_ref.shape[2]
    n_state = st_ref.shape[3]
    hpg = n_heads // SSD_GROUPS

    @pl.when(c == 0)
    def _():
        xx_ref[CONV_HIST:CONV_PAD, :] = cbuf_ref[0]
        st_ref[...] = h0_ref[...]

    xx_ref[CONV_PAD:CONV_PAD + q, 0:d_inner] = xs_ref[...]
    xx_ref[CONV_PAD:CONV_PAD + q, d_inner:] = bc_ref[...]
    act = _silu(_conv_tile(xx_ref, q, cw_ref[...], cb_ref[...]))
    tail = xx_ref[q + CONV_HIST:q + CONV_PAD, :]
    xx_ref[CONV_HIST:CONV_PAD, :] = tail

    @pl.when(c == pl.num_programs(1) - 1)
    def _():
        nconv_ref[0] = tail

    dt = _softplus(dt_ref[...] + dtb_ref[...])
    adt = dt * (-jnp.exp(alog_ref[...]))
    row = lax.broadcasted_iota(jnp.int32, (q, q), 0)
    col = lax.broadcasted_iota(jnp.int32, (q, q), 1)
    tril = (col <= row).astype(BF16)
    cs = _dot_sel(tril, adt, sel_on_left=True)
    both_x = _dot_sel(ex_ref[...], jnp.concatenate([dt, cs], axis=0))
    dt_x = both_x[:q]
    cs_x = both_x[q:]
    if eq_ref is None:
        cs_q = cs_x
    else:
        cs_q = _dot_sel(eq_ref[...], cs)

    xs = act[:, :d_inner]
    xd = xs * dt_x
    xd_b = xd.astype(BF16)
    xdec_b = (xd * jnp.exp(cs_x[q - 1:q, :] - cs_x)).astype(BF16)
    ecs_x = jnp.exp(cs_x)

    head = lax.broadcasted_iota(jnp.int32, (q, n_heads), 1)
    even = (head & 1) == 0
    cs_par = jnp.concatenate([jnp.where(even, cs, 0.0), jnp.where(even, 0.0, cs)], axis=0)
    pair_sel = ((lax.broadcasted_iota(jnp.int32, (n_heads // 2, n_heads), 1) >> 1)
                == lax.broadcasted_iota(jnp.int32, (n_heads // 2, n_heads), 0)).astype(BF16)
    rtab = _dot_sel(pair_sel, cs_par, sel_on_left=True, dims=NT_DIMS)
    eye_h = (lax.broadcasted_iota(jnp.int32, (n_heads, n_heads), 0)
             == lax.broadcasted_iota(jnp.int32, (n_heads, n_heads), 1)).astype(BF16)
    cs_t = _dot_sel(eye_h, cs, sel_on_left=True, dims=NT_DIMS)
    st_decay = jnp.exp(jnp.broadcast_to(cs_t[:, q - 1:q], (n_heads, n_state)))

    row2 = lax.broadcasted_iota(jnp.int32, (q, 2 * q), 0)
    col2 = lax.broadcasted_iota(jnp.int32, (q, 2 * q), 1)
    causal2 = jnp.where(col2 >= q, col2 - q, col2) <= row2
    lane_p = lax.broadcasted_iota(jnp.int32, (q, 2 * p_dim), 1)
    lo_half = lane_p < p_dim
    zero_b = jnp.zeros((q, 2 * p_dim), BF16)
    b_off = d_inner
    c_off = d_inner + SSD_GROUPS * n_state
    gp = hpg * p_dim
    pairs_per_group = hpg // 2

    y_diag_parts, y_off_parts = [], []
    for g in range(SSD_GROUPS):
        bg = act[:, b_off + g * n_state:b_off + (g + 1) * n_state].astype(BF16)
        cg = act[:, c_off + g * n_state:c_off + (g + 1) * n_state].astype(BF16)
        cb2 = lax.dot_general(cg, jnp.concatenate([bg, bg], axis=0), NT_DIMS,
                              preferred_element_type=F32)
        prev = st_ref[0, g * hpg:(g + 1) * hpg].reshape(gp, n_state)
        y_off_parts.append(lax.dot_general(cg, prev.astype(BF16), NT_DIMS, preferred_element_type=F32))
        for jj in range(pairs_per_group):
            j = g * pairs_per_group + jj
            lmat = jnp.exp(jnp.where(causal2, cs_q[:, 2 * j * q:2 * (j + 1) * q] - rtab[j:j + 1, :], -jnp.inf))
            m = (cb2 * lmat).astype(BF16)
            pp = xd_b[:, 2 * j * p_dim:2 * (j + 1) * p_dim]
            blockdiag = jnp.concatenate([jnp.where(lo_half, pp, zero_b), jnp.where(lo_half, zero_b, pp)], axis=0)
            y_diag_parts.append(jnp.dot(m, blockdiag, preferred_element_type=F32))
        st_new = lax.dot_general(xdec_b[:, g * gp:(g + 1) * gp], bg, TN_DIMS,
                                 preferred_element_type=F32)
        for e in range(hpg):
            h = g * hpg + e
            st_ref[0, h] = (st_decay[h:h + 1, :] * prev[e * p_dim:(e + 1) * p_dim, :]
                            + st_new[e * p_dim:(e + 1) * p_dim, :])

    y = (jnp.concatenate(y_diag_parts, axis=-1) + jnp.concatenate(y_off_parts, axis=-1) * ecs_x
         + dskx_ref[...] * xs)
    gated = y * _silu(z_ref[...])
    gw = d_inner // SSD_GROUPS
    nw = nw_ref[...]
    for g in range(SSD_GROUPS):
        seg = gated[:, g * gw:(g + 1) * gw]
        ms = jnp.mean(seg * seg, axis=-1, keepdims=True)
        out_ref[:, g * gw:(g + 1) * gw] = (seg * lax.rsqrt(ms + RMS_EPS) * nw[:, g * gw:(g + 1) * gw]).astype(BF16)


def _ssd_seq(zx, dt_raw, row0, bsz, seqlen, q, conv_buf, h0, cw, cb, dtb, alog, dsk, nw):
    n_heads, p_dim, n_state = h0.shape[1:]
    d_inner = n_heads * p_dim
    conv_dim = cw.shape[1]
    nt = seqlen // q
    blk0 = row0 // q
    full2 = lambda b, c: (0, 0)
    col = lambda j: (lambda b, c: (blk0 + b * nt + c, j))
    head_ids = jnp.arange(n_heads, dtype=jnp.int32)[:, None]
    expand = lambda width: (jnp.arange(n_heads * width, dtype=jnp.int32)[None, :] // width == head_ids).astype(BF16)
    has_eq = q != p_dim
    extra_in = [expand(q)] if has_eq else []
    extra_specs = [pl.BlockSpec((n_heads, n_heads * q), full2)] if has_eq else []
    return pl.pallas_call(
        functools.partial(_ssd_seq_kernel, has_eq),
        grid=(bsz, nt),
        in_specs=[
            pl.BlockSpec((q, d_inner), col(0)),
            pl.BlockSpec((q, d_inner), col(1)),
            pl.BlockSpec((q, d_inner), col(2)),
            pl.BlockSpec((q, n_heads), col(0)),
            pl.BlockSpec((1, CONV_WIDTH - 1, conv_dim), lambda b, c: (b, 0, 0)),
            pl.BlockSpec((1, n_heads, p_dim, n_state), lambda b, c: (b, 0, 0, 0)),
            pl.BlockSpec((CONV_WIDTH, conv_dim), full2),
            pl.BlockSpec((1, conv_dim), full2),
            pl.BlockSpec((1, n_heads), full2),
            pl.BlockSpec((1, n_heads), full2),
            pl.BlockSpec((1, d_inner), full2),
            pl.BlockSpec((1, d_inner), full2),
            pl.BlockSpec((n_heads, d_inner), full2),
        ] + extra_specs,
        out_specs=[
            pl.BlockSpec((q, d_inner), lambda b, c: (b * nt + c, 0)),
            pl.BlockSpec((1, CONV_WIDTH - 1, conv_dim), lambda b, c: (b, 0, 0)),
            pl.BlockSpec((1, n_heads, p_dim, n_state), lambda b, c: (b, 0, 0, 0)),
        ],
        out_shape=[
            jax.ShapeDtypeStruct((bsz * seqlen, d_inner), BF16),
            jax.ShapeDtypeStruct((bsz, CONV_WIDTH - 1, conv_dim), F32),
            jax.ShapeDtypeStruct((bsz, n_heads, p_dim, n_state), F32),
        ],
        scratch_shapes=[
            pltpu.VMEM((q + CONV_PAD, conv_dim), F32),
        ],
        compiler_params=_params("arbitrary", "arbitrary"),
        name="ssd_seq",
    )(zx, zx, zx, dt_raw, conv_buf, h0, cw, cb.reshape(1, -1), dtb.reshape(1, -1),
      alog.reshape(1, -1), jnp.repeat(dsk, p_dim).reshape(1, -1), nw.reshape(1, -1), expand(p_dim), *extra_in)


def _first_argmax(v, rid, n):
    m = jnp.max(v, axis=0, keepdims=True)
    idx = jnp.min(jnp.where(v == m, rid, float(n)), axis=0, keepdims=True)
    return m, idx


def _route_tile(scores_t, bias_col):
    n_exp, tm = scores_t.shape
    per = n_exp // N_EXPERT_GROUPS
    neg = -jnp.inf
    biased = scores_t + bias_col
    rid_g = lax.broadcasted_iota(jnp.int32, (per, tm), 0).astype(F32)
    tiles, gs_rows = [], []
    for g in range(N_EXPERT_GROUPS):
        v = biased[g * per:(g + 1) * per, :]
        tiles.append(v)
        m1, i1 = _first_argmax(v, rid_g, per)
        m2 = jnp.max(jnp.where(rid_g == i1, neg, v), axis=0, keepdims=True)
        gs_rows.append(m1 + m2)
    gs = jnp.concatenate(gs_rows, axis=0)
    rid_grp = lax.broadcasted_iota(jnp.int32, (N_EXPERT_GROUPS, tm), 0).astype(F32)
    gsel = jnp.zeros((N_EXPERT_GROUPS, tm), F32)
    for _ in range(TOPK_GROUPS):
        _, gi = _first_argmax(gs, rid_grp, N_EXPERT_GROUPS)
        hit = rid_grp == gi
        gsel = jnp.where(hit, 1.0, gsel)
        gs = jnp.where(hit, neg, gs)
    masked = jnp.concatenate(
        [jnp.where(gsel[g:g + 1, :] > 0.0, tiles[g], neg) for g in range(N_EXPERT_GROUPS)], axis=0)
    rid = lax.broadcasted_iota(jnp.int32, (n_exp, tm), 0).astype(F32)
    sel = jnp.zeros((n_exp, tm), F32)
    ids, picked = [], []
    for _ in range(TOP_K):
        _, ei = _first_argmax(masked, rid, n_exp)
        hit = rid == ei
        ids.append(ei)
        picked.append(jnp.sum(jnp.where(hit, scores_t, 0.0), axis=0, keepdims=True))
        sel = jnp.where(hit, 1.0, sel)
        masked = jnp.where(hit, neg, masked)
    total = picked[0]
    for p in picked[1:]:
        total = total + p
    gates = [p / total * ROUTE_SCALE for p in picked]
    return sel, rid, ids, gates


def _proj_ln_router_kernel(alpha, n_a, ha_ref, hb_ref, w_ref, x_ref, g_ref, b_ref, rwt_ref, rb_ref,
                           o_ref, op_ref, eid_ref, gate_ref, rank_ref, cnt_ref, wb_ref, tri_ref, mix_ref):
    i = pl.program_id(0)
    tm = x_ref.shape[0]

    @pl.when(i == 0)
    def _():
        wb_ref[...] = w_ref[...].astype(BF16)
        r = lax.broadcasted_iota(jnp.int32, (tm, tm), 0)
        c = lax.broadcasted_iota(jnp.int32, (tm, tm), 1)
        tri_ref[...] = jnp.where(r < c, 1.0, 0.0).astype(BF16)
        cnt_ref[...] = jnp.zeros_like(cnt_ref)

    @pl.when(i < n_a)
    def _():
        mix_ref[...] = jnp.dot(ha_ref[...], wb_ref[...], preferred_element_type=F32)

    @pl.when(i >= n_a)
    def _():
        mix_ref[...] = jnp.dot(hb_ref[...], wb_ref[...], preferred_element_type=F32)

    x1 = _layer_norm(alpha * x_ref[...] + mix_ref[...], g_ref[...], b_ref[...])
    o_ref[...] = x1
    half = x1.shape[1] // 2
    op_ref[...] = _pack_pair(x1[:, :half], x1[:, half:])
    logits_t = lax.dot_general(rwt_ref[...].astype(BF16), x1.astype(BF16), NT_DIMS,
                               preferred_element_type=F32)
    sel, rid, ids, gates = _route_tile(_sigmoid(logits_t), rb_ref[...])

    sel_b = sel.astype(BF16)
    before = jnp.dot(sel_b, tri_ref[...], preferred_element_type=F32) + cnt_ref[:, 0:1]
    pad_rows = eid_ref.shape[0] - TOP_K
    zrow = jnp.zeros((pad_rows, tm), F32)
    ranks = [jnp.sum(jnp.where(rid == ei, before, 0.0), axis=0, keepdims=True) for ei in ids]
    eid_ref[...] = jnp.concatenate(ids + [zrow], axis=0).astype(jnp.int32)
    gate_ref[...] = jnp.concatenate(gates + [zrow], axis=0)
    rank_ref[...] = jnp.concatenate(ranks + [zrow], axis=0).astype(jnp.int32)
    ones = jnp.ones((tm, cnt_ref.shape[1]), BF16)
    cnt_ref[...] = cnt_ref[...] + jnp.dot(sel_b, ones, preferred_element_type=F32)


def _proj_ln_router(ha, hb, w, wl, x, alpha, ln_g, ln_b, router_w_t, router_bias_col):
    K = ha.shape[1]
    T, D = x.shape
    E = router_w_t.shape[0]
    tm = TM_PROJ
    kp = SUBLANE
    n_a = ha.shape[0] // tm
    n_b = hb.shape[0] // tm
    assert ha.shape[0] == n_a * tm and hb.shape[0] == n_b * tm and (n_a + n_b) * tm == T
    row_spec = pl.BlockSpec((kp, tm), lambda i: (0, i))
    return pl.pallas_call(
        functools.partial(_proj_ln_router_kernel, alpha, n_a),
        grid=(T // tm,),
        in_specs=[
            pl.BlockSpec((tm, K), lambda i: (jnp.minimum(i, n_a - 1), 0)),
            pl.BlockSpec((tm, K), lambda i: (jnp.maximum(i - n_a, 0), 0)),
            pl.BlockSpec((None, K, D), lambda i: (wl, 0, 0)),
            pl.BlockSpec((tm, D), lambda i: (i, 0)),
            pl.BlockSpec((1, D), lambda i: (0, 0)),
            pl.BlockSpec((1, D), lambda i: (0, 0)),
            pl.BlockSpec((E, D), lambda i: (0, 0)),
            pl.BlockSpec((E, 1), lambda i: (0, 0)),
        ],
        out_specs=[pl.BlockSpec((tm, D), lambda i: (i, 0)), pl.BlockSpec((tm, D // 2), lambda i: (i, 0)),
                   row_spec, row_spec, row_spec, pl.BlockSpec((E, LANE), lambda i: (0, 0))],
        out_shape=[jax.ShapeDtypeStruct((T, D), F32),
                   jax.ShapeDtypeStruct((T, D // 2), jnp.uint32),
                   jax.ShapeDtypeStruct((kp, T), jnp.int32),
                   jax.ShapeDtypeStruct((kp, T), F32),
                   jax.ShapeDtypeStruct((kp, T), jnp.int32),
                   jax.ShapeDtypeStruct((E, LANE), F32)],
        scratch_shapes=[pltpu.VMEM((K, D), BF16), pltpu.VMEM((tm, tm), BF16), pltpu.VMEM((tm, D), F32)],
        compiler_params=_params("arbitrary"),
        name="proj_ln_router",
    )(ha, hb, w, x, ln_g.reshape(1, D), ln_b.reshape(1, D), router_w_t, router_bias_col)


def _start_row_gather(idx_ref, base, n, src_hbm, dst, sem):
    for i in range(n):
        r = idx_ref[base + i]
        pltpu.make_async_copy(src_hbm.at[pl.ds(r, 1), :], dst.at[pl.ds(i, 1), :], sem).start(priority=i % 2)


def _wait_row_gather(n, src_hbm, dst, sem):
    pltpu.make_async_copy(src_hbm.at[pl.ds(0, n), :], dst, sem).wait()


def _scatter_kernel(n_tiles, pos_ref, zb_ref, nu_ref, x_ref, xs_hbm, xbuf, zbuf, sem, zsem):
    i = pl.program_id(0)
    tm = x_ref.shape[0]
    n_rows = TOP_K * tm
    bm = zbuf.shape[0]
    n_experts = zb_ref.shape[0]
    n_blocks = xs_hbm.shape[0] // bm
    slot = lax.rem(i, 2)

    def zero_block(blk):
        start = pl.multiple_of(blk * bm, bm)
        return pltpu.make_async_copy(zbuf, xs_hbm.at[pl.ds(start, bm), :], zsem)

    def wait_rows(s):
        pltpu.make_async_copy(xs_hbm.at[pl.ds(0, n_rows), :], xs_hbm.at[pl.ds(0, n_rows), :], sem.at[s]).wait()

    @pl.when(i == 0)
    def _():
        zbuf[...] = jnp.zeros_like(zbuf)
        for e in range(n_experts):
            @pl.when(zb_ref[e] >= 0)
            def _():
                zero_block(zb_ref[e]).start()

        def start_tail(blk, carry):
            zero_block(blk).start()
            return carry

        def wait_tail(blk, carry):
            zero_block(blk).wait()
            return carry

        lax.fori_loop(nu_ref[0], n_blocks, start_tail, 0)
        for e in range(n_experts):
            @pl.when(zb_ref[e] >= 0)
            def _():
                zero_block(zb_ref[e]).wait()
        lax.fori_loop(nu_ref[0], n_blocks, wait_tail, 0)

    @pl.when(i >= 2)
    def _():
        wait_rows(slot)

    xbuf[slot] = x_ref[...]
    for k in range(TOP_K):
        for t in range(tm):
            r = pos_ref[i * n_rows + k * tm + t]
            pltpu.make_async_copy(xbuf.at[slot, pl.ds(t, 1), :], xs_hbm.at[pl.ds(r, 1), :],
                                  sem.at[slot]).start(priority=t % 2)

    @pl.when(i == n_tiles - 1)
    def _():
        wait_rows(slot)
        if n_tiles >= 2:
            wait_rows(1 - slot)


def _scatter_rows(x, pos_tiles, zero_blk, n_used, n_slots):
    T, D = x.shape
    tm = TM_COMBINE
    n_tiles = T // tm
    grid_spec = pltpu.PrefetchScalarGridSpec(
        num_scalar_prefetch=3,
        grid=(n_tiles,),
        in_specs=[pl.BlockSpec((tm, D), lambda i, pos, zb, nu: (i, 0))],
        out_specs=pl.BlockSpec(memory_space=pl.ANY),
        scratch_shapes=[
            pltpu.VMEM((2, tm, D), x.dtype),
            pltpu.VMEM((BM_EXPERT, D), x.dtype),
            pltpu.SemaphoreType.DMA((2,)),
            pltpu.SemaphoreType.DMA,
        ],
    )
    return pl.pallas_call(
        functools.partial(_scatter_kernel, n_tiles),
        grid_spec=grid_spec,
        out_shape=jax.ShapeDtypeStruct((n_slots, D), x.dtype),
        compiler_params=_params("arbitrary"),
        name="scatter_rows",
    )(pos_tiles, zero_blk, n_used, x)


def _expert_kernel(be_ref, nu_ref, x_ref, wg_ref, wu_ref, wd_ref, y_ref, wgb, wub, wdb):
    b = pl.program_id(0)
    n_used = nu_ref[0]

    @pl.when(b < n_used)
    def _():
        new_expert = jnp.logical_or(b == 0, be_ref[b] != be_ref[jnp.maximum(b - 1, 0)])

        @pl.when(new_expert)
        def _():
            wgb[...] = wg_ref[...].astype(BF16)
            wub[...] = wu_ref[...].astype(BF16)
            wdb[...] = wd_ref[...].astype(BF16)

        half = x_ref.shape[1]
        x_lo, x_hi = _unpack_pair(x_ref[...])
        x_lo = x_lo.astype(BF16)
        x_hi = x_hi.astype(BF16)
        hg = (jnp.dot(x_lo, wgb[0:half, :], preferred_element_type=F32)
              + jnp.dot(x_hi, wgb[half:, :], preferred_element_type=F32))
        hu = (jnp.dot(x_lo, wub[0:half, :], preferred_element_type=F32)
              + jnp.dot(x_hi, wub[half:, :], preferred_element_type=F32))
        hh = (_silu(hg) * hu).astype(BF16)
        y = jnp.dot(hh, wdb[...], preferred_element_type=F32)
        y_ref[...] = _pack_pair(y[:, :half], y[:, half:])

    @pl.when(b >= n_used)
    def _():
        y_ref[...] = jnp.zeros_like(y_ref)


def _experts(x_sorted, block_e, n_used, wg, wu, wd, layer):
    n_slots, dh = x_sorted.shape
    D, DE = wg.shape[2:]
    bm = BM_EXPERT
    nb = n_slots // bm
    w_map = lambda b, be, nu: (layer, be[b], 0, 0)
    grid_spec = pltpu.PrefetchScalarGridSpec(
        num_scalar_prefetch=2,
        grid=(nb,),
        in_specs=[
            pl.BlockSpec((bm, dh), lambda b, be, nu: (jnp.minimum(b, nu[0] - 1), 0)),
            pl.BlockSpec((None, None, D, DE), w_map),
            pl.BlockSpec((None, None, D, DE), w_map),
            pl.BlockSpec((None, None, DE, D), w_map),
        ],
        out_specs=pl.BlockSpec((bm, dh), lambda b, be, nu: (b, 0)),
        scratch_shapes=[
            pltpu.VMEM((D, DE), BF16),
            pltpu.VMEM((D, DE), BF16),
            pltpu.VMEM((DE, D), BF16),
        ],
    )
    return pl.pallas_call(
        _expert_kernel,
        grid_spec=grid_spec,
        out_shape=jax.ShapeDtypeStruct((n_slots, dh), jnp.uint32),
        compiler_params=_params("arbitrary"),
        name="experts",
    )(block_e, n_used, x_sorted, wg, wu, wd)


def _combine_kernel(alpha, n_tiles, pos_ref, gates_ref, x_ref, y_hbm, sg_ref, su_ref, sd_ref, g_ref, b_ref,
                    o_ref, ob_ref, ybuf_a, ybuf_b, sgb, sub, sdb, sem):
    i = pl.program_id(0)
    tm = x_ref.shape[0] // 2
    n_rows = ybuf_a.shape[0]
    first = 2 * i

    @pl.when(i == 0)
    def _():
        _start_row_gather(pos_ref, 0, n_rows, y_hbm, ybuf_a, sem.at[0])
        sgb[...] = sg_ref[...].astype(BF16)
        sub[...] = su_ref[...].astype(BF16)
        sdb[...] = sd_ref[...].astype(BF16)

    def tile(r0, ybuf):
        x = x_ref[r0:r0 + tm, :]
        xb = x.astype(BF16)
        hg = jnp.dot(xb, sgb[...], preferred_element_type=F32)
        hu = jnp.dot(xb, sub[...], preferred_element_type=F32)
        acc = jnp.dot((_silu(hg) * hu).astype(BF16), sdb[...], preferred_element_type=F32)
        gates = gates_ref[r0:r0 + tm, :]
        routed_lo = routed_hi = None
        for k in range(TOP_K):
            y_lo, y_hi = _unpack_pair(ybuf[k * tm:(k + 1) * tm, :])
            gk = gates[:, k:k + 1]
            routed_lo = gk * y_lo if routed_lo is None else routed_lo + gk * y_lo
            routed_hi = gk * y_hi if routed_hi is None else routed_hi + gk * y_hi
        acc = acc + jnp.concatenate([routed_lo, routed_hi], axis=-1)
        out = _layer_norm(alpha * x + acc, g_ref[...], b_ref[...])
        o_ref[r0:r0 + tm, :] = out
        ob_ref[r0:r0 + tm, :] = out.astype(BF16)

    _wait_row_gather(n_rows, y_hbm, ybuf_a, sem.at[0])
    _start_row_gather(pos_ref, (first + 1) * n_rows, n_rows, y_hbm, ybuf_b, sem.at[1])
    tile(0, ybuf_a)
    _wait_row_gather(n_rows, y_hbm, ybuf_b, sem.at[1])
    nxt = jnp.minimum(first + 2, n_tiles - 2)
    _start_row_gather(pos_ref, nxt * n_rows, n_rows, y_hbm, ybuf_a, sem.at[0])
    tile(tm, ybuf_b)

    @pl.when(i == pl.num_programs(0) - 1)
    def _():
        _wait_row_gather(n_rows, y_hbm, ybuf_a, sem.at[0])


def _combine(x, y_sorted, pos_tiles, gates, sg, su, sd, alpha, ln_g, ln_b, layer):
    T, D = x.shape
    DS = sg.shape[2]
    tm = TM_COMBINE
    n_tiles = T // tm
    assert n_tiles % 2 == 0, "combine walks token tiles in pairs"
    grid_spec = pltpu.PrefetchScalarGridSpec(
        num_scalar_prefetch=1,
        grid=(n_tiles // 2,),
        in_specs=[
            pl.BlockSpec((2 * tm, TOP_K), lambda i, pos: (i, 0)),
            pl.BlockSpec((2 * tm, D), lambda i, pos: (i, 0)),
            pl.BlockSpec(memory_space=pl.ANY),
            pl.BlockSpec((None, D, DS), lambda i, pos: (layer, 0, 0)),
            pl.BlockSpec((None, D, DS), lambda i, pos: (layer, 0, 0)),
            pl.BlockSpec((None, DS, D), lambda i, pos: (layer, 0, 0)),
            pl.BlockSpec((None, 1, D), lambda i, pos: (layer, 0, 0)),
            pl.BlockSpec((None, 1, D), lambda i, pos: (layer, 0, 0)),
        ],
        out_specs=[pl.BlockSpec((2 * tm, D), lambda i, pos: (i, 0)), pl.BlockSpec((2 * tm, D), lambda i, pos: (i, 0))],
        scratch_shapes=[
            pltpu.VMEM((TOP_K * tm, y_sorted.shape[1]), y_sorted.dtype),
            pltpu.VMEM((TOP_K * tm, y_sorted.shape[1]), y_sorted.dtype),
            pltpu.VMEM((D, DS), BF16),
            pltpu.VMEM((D, DS), BF16),
            pltpu.VMEM((DS, D), BF16),
            pltpu.SemaphoreType.DMA((2,)),
        ],
    )
    return pl.pallas_call(
        functools.partial(_combine_kernel, alpha, n_tiles),
        grid_spec=grid_spec,
        out_shape=[jax.ShapeDtypeStruct((T, D), F32), jax.ShapeDtypeStruct((T, D), BF16)],
        compiler_params=_params("arbitrary"),
        name="combine",
    )(pos_tiles, gates, x, y_sorted, sg, su, sd, ln_g.reshape(-1, 1, D), ln_b.reshape(-1, 1, D))


def _dispatch(eid_t, rank_t, counts):
    T = eid_t.shape[1]
    E = counts.shape[0]
    A = T * TOP_K
    bm = BM_EXPERT
    nb = -(-A // bm) + E
    cnt = counts[:, 0].astype(jnp.int32)
    padded = (cnt + bm - 1) // bm * bm
    pends = jnp.cumsum(padded)
    pstart = pends - padded
    eid = eid_t[:TOP_K]
    onehot = eid[:, :, None] == jnp.arange(E, dtype=jnp.int32)[None, None, :]
    dest = rank_t[:TOP_K] + jnp.sum(jnp.where(onehot, pstart[None, None, :], 0), axis=-1)
    block_start = jnp.arange(nb, dtype=jnp.int32) * bm
    block_e = jnp.minimum(jnp.sum((pends[None, :] <= block_start[:, None]).astype(jnp.int32), axis=1), E - 1)
    n_used = (pends[-1:] // bm).astype(jnp.int32)
    zero_blk = jnp.where(padded > 0, pends // bm - 1, -1).astype(jnp.int32)
    tm = TM_COMBINE
    pos_tiles = dest.reshape(TOP_K, T // tm, tm).transpose(1, 0, 2).reshape(-1)
    return nb * bm, block_e, n_used, zero_blk, pos_tiles


def _moe(x1, x1_packed, eid_t, gate_t, rank_t, counts, alpha, layer, wg, wu, wd, sg, su, sd, ln_g, ln_b):
    n_slots, block_e, n_used, zero_blk, pos_tiles = _dispatch(eid_t, rank_t, counts)
    x_sorted = _scatter_rows(x1_packed, pos_tiles, zero_blk, n_used, n_slots)
    y_sorted = _experts(x_sorted, block_e, n_used, wg, wu, wd, layer)
    gates = gate_t[:TOP_K].T
    return _combine(x1, y_sorted, pos_tiles, gates, sg, su, sd, alpha, ln_g, ln_b, layer)


def kernel(x_prompt, x_sample, state_lru_conv, state_lru_h, state_ssd_conv, state_ssd, lru_w_in, lru_conv_w, lru_conv_b, lru_w_rgate, lru_b_rgate, lru_w_igate, lru_b_igate, lru_lambda, lru_w_out, ssd_w_in, ssd_conv_w, ssd_conv_b, ssd_dt_bias, ssd_a_log, ssd_d, ssd_norm_w, ssd_w_out, ln_mix_g, ln_mix_b, ln_ffn_g, ln_ffn_b, router_w, router_bias, moe_w_gate, moe_w_up, moe_w_down, shared_w_gate, shared_w_up, shared_w_down):
    bp, lp, d_model = x_prompt.shape
    bs, ls, _ = x_sample.shape
    depth = ln_mix_g.shape[0]
    alpha = (2.0 * depth) ** 0.25
    tp, ts = bp * lp, bs * ls
    n_heads, p_dim, n_state = state_ssd.shape[2:]
    d_inner = n_heads * p_dim
    conv_dim = ssd_conv_w.shape[2]
    d_rnn = lru_conv_w.shape[2]

    x = jnp.concatenate([x_prompt.reshape(tp, d_model), x_sample.reshape(ts, d_model)], axis=0)
    tl_p = min(TL_LRU, lp)
    tl_s = min(TL_LRU, ls)
    q_p = min(SSD_CHUNK, lp)
    q_s = min(SSD_CHUNK, ls)

    x_mxu = x
    p_lru_conv, p_lru_h, p_ssd_conv, p_ssd = [], [], [], []
    s_lru_conv, s_lru_h, s_ssd_conv, s_ssd = [], [], [], []
    for i in range(depth):
        j = i // 2
        if i % 2 == 0:
            gx = _in_proj(x_mxu, lru_w_in, j, 2 * d_rnn, TN_LRU_IN)
            wrg = _block_diag_gate(lru_w_rgate[j])
            wig = _block_diag_gate(lru_w_igate[j])
            common = (lru_conv_w[j], lru_conv_b[j], wrg, lru_b_rgate[j], wig, lru_b_igate[j], lru_lambda[j])
            hp, cp, lp_h = _lru_seq(gx, 0, bp, lp, tl_p, jnp.zeros((bp, CONV_WIDTH - 1, d_rnn), F32),
                                    jnp.zeros((bp, d_rnn), F32), *common)
            hs, cs_, ls_h = _lru_seq(gx, tp, bs, ls, tl_s, state_lru_conv[j], state_lru_h[j], *common)
            p_lru_conv.append(cp)
            p_lru_h.append(lp_h.reshape(bp, d_rnn))
            s_lru_conv.append(cs_)
            s_lru_h.append(ls_h.reshape(bs, d_rnn))
            w_out, wl = lru_w_out, j
        else:
            zx = _in_proj(x_mxu, ssd_w_in, j, d_inner + conv_dim, TN_SSD_IN)
            dt_raw = _small_proj(x_mxu, ssd_w_in[j][:, d_inner + conv_dim:])
            common = (ssd_conv_w[j], ssd_conv_b[j], ssd_dt_bias[j], ssd_a_log[j], ssd_d[j], ssd_norm_w[j])
            hp, cp, sp = _ssd_seq(zx, dt_raw, 0, bp, lp, q_p, jnp.zeros((bp, CONV_WIDTH - 1, conv_dim), F32),
                                  jnp.zeros((bp, n_heads, p_dim, n_state), F32), *common)
            hs, cs_, ss = _ssd_seq(zx, dt_raw, tp, bs, ls, q_s, state_ssd_conv[j], state_ssd[j], *common)
            p_ssd_conv.append(cp)
            p_ssd.append(sp)
            s_ssd_conv.append(cs_)
            s_ssd.append(ss)
            w_out, wl = ssd_w_out, j
        x1, x1_packed, eid_t, gate_t, rank_t, counts = _proj_ln_router(
            hp, hs, w_out, wl, x, alpha, ln_mix_g[i], ln_mix_b[i], router_w[i].T, router_bias[i].reshape(-1, 1))
        x, x_mxu = _moe(x1, x1_packed, eid_t, gate_t, rank_t, counts, alpha, i, moe_w_gate, moe_w_up, moe_w_down,
                 shared_w_gate, shared_w_up, shared_w_down, ln_ffn_g, ln_ffn_b)

    y_prompt = x[:tp].reshape(bp, lp, d_model)
    y_sample = x[tp:].reshape(bs, ls, d_model)
    return (y_prompt, y_sample,
            jnp.stack(p_lru_conv), jnp.stack(p_lru_h), jnp.stack(p_ssd_conv), jnp.stack(p_ssd),
            jnp.stack(s_lru_conv), jnp.stack(s_lru_h), jnp.stack(s_ssd_conv), jnp.stack(s_ssd))
```

```python
import functools
import math

import jax
import jax.numpy as jnp
from jax import lax
from jax.experimental import pallas as pl
from jax.experimental.pallas import tpu as pltpu

F32 = jnp.float32
BF16 = jnp.bfloat16

LN_EPS = 1e-5
RMS_EPS = 1e-5
CONV_WIDTH = 4
LRU_C = 8.0
SSD_CHUNK = 64
SSD_GROUPS = 8
TOP_K = 6
N_EXPERT_GROUPS = 8
TOPK_GROUPS = 4
ROUTE_SCALE = 1.0

LANE = 128
SUBLANE = 8
MXU_DIM = 256
VMEM_LIMIT = 56 * 1024 * 1024

TM_PROJ = 512
TM_COMBINE = 128
BM_EXPERT = 512
TL_LRU = 256
TN_LRU_IN = 1024
TN_SSD_IN = 1536

HIGHEST = lax.Precision.HIGHEST
NT_DIMS = (((1,), (1,)), ((), ()))
TN_DIMS = (((0,), (0,)), ((), ()))


def _sigmoid(x):
    return 1.0 / (1.0 + jnp.exp(-x))


def _silu(x):
    return x * _sigmoid(x)


def _softplus(x):
    return jnp.maximum(x, 0.0) + jnp.log1p(jnp.exp(-jnp.abs(x)))


def _gelu_tanh(x):
    c = math.sqrt(2.0 / math.pi)
    return 0.5 * x * (1.0 + jnp.tanh(c * (x + 0.044715 * (x * x * x))))


def _layer_norm(v, g, b):
    mu = jnp.mean(v, axis=-1, keepdims=True)
    d = v - mu
    var = jnp.mean(d * d, axis=-1, keepdims=True)
    return d * lax.rsqrt(var + LN_EPS) * g + b


def _pack_pair(lo, hi):
    lo_bits = lax.bitcast_convert_type(lo.astype(BF16).astype(F32), jnp.uint32)
    hi_bits = lax.bitcast_convert_type(hi.astype(BF16).astype(F32), jnp.uint32)
    return hi_bits | (lo_bits >> 16)


def _unpack_pair(w):
    lo = lax.bitcast_convert_type(w << 16, F32)
    hi = lax.bitcast_convert_type(w & jnp.uint32(0xFFFF0000), F32)
    return lo, hi


def _params(*sem):
    return pltpu.CompilerParams(dimension_semantics=sem, vmem_limit_bytes=VMEM_LIMIT)


def _mm_kernel(x_ref, w_ref, o_ref):
    o_ref[...] = jnp.dot(x_ref[...].astype(BF16), w_ref[...].astype(BF16),
                         preferred_element_type=F32)


def _in_proj_kernel(x_ref, w_ref, o_ref, wb_ref):
    @pl.when(pl.program_id(1) == 0)
    def _():
        wb_ref[...] = w_ref[...].astype(BF16)

    o_ref[...] = jnp.dot(x_ref[...].astype(BF16), wb_ref[...], preferred_element_type=F32)


def _in_proj(x, w, layer, n_cols, tn):
    T, K = x.shape
    tm = TM_PROJ
    return pl.pallas_call(
        _in_proj_kernel,
        grid=(n_cols // tn, T // tm),
        in_specs=[pl.BlockSpec((tm, K), lambda j, i: (i, 0)),
                  pl.BlockSpec((None, K, tn), lambda j, i: (layer, 0, j))],
        out_specs=pl.BlockSpec((tm, tn), lambda j, i: (i, j)),
        out_shape=jax.ShapeDtypeStruct((T, n_cols), F32),
        scratch_shapes=[pltpu.VMEM((K, tn), BF16)],
        compiler_params=_params("arbitrary", "arbitrary"),
        name="in_proj",
    )(x, w)


def _small_proj(x, w):
    T, K = x.shape
    N = w.shape[1]
    tm = TM_PROJ
    return pl.pallas_call(
        _mm_kernel,
        grid=(T // tm,),
        in_specs=[pl.BlockSpec((tm, K), lambda i: (i, 0)),
                  pl.BlockSpec((K, N), lambda i: (0, 0))],
        out_specs=pl.BlockSpec((tm, N), lambda i: (i, 0)),
        out_shape=jax.ShapeDtypeStruct((T, N), F32),
        compiler_params=_params("parallel"),
        name="dt_proj",
    )(x, w)


CONV_PAD = SUBLANE
CONV_HIST = CONV_PAD - (CONV_WIDTH - 1)


def _conv_tile(xx_ref, n, cw, cb):
    acc = cb + cw[CONV_WIDTH - 1:CONV_WIDTH] * xx_ref[CONV_PAD:CONV_PAD + n, :]
    for k in range(CONV_WIDTH - 1):
        off = CONV_HIST + k
        acc = acc + cw[k:k + 1] * xx_ref[off:off + n, :]
    return acc


def _lru_seq_kernel(gate_ref, xr_ref, cbuf_ref, h0_ref, cw_ref, cb_ref, wrg_ref, brg_ref,
                    wig_ref, big_ref, lam_ref, out_ref, nconv_ref, hlast_ref,
                    xx_ref, h_ref, a_ref, u_ref):
    c = pl.program_id(1)
    n = xr_ref.shape[0]
    d = xr_ref.shape[1]

    @pl.when(c == 0)
    def _():
        xx_ref[CONV_HIST:CONV_PAD, :] = cbuf_ref[0]
        h_ref[...] = h0_ref[0]

    xx_ref[CONV_PAD:CONV_PAD + n, :] = xr_ref[...]
    xr = _conv_tile(xx_ref, n, cw_ref[...], cb_ref[...])
    tail = xx_ref[n + CONV_HIST:n + CONV_PAD, :]
    xx_ref[CONV_HIST:CONV_PAD, :] = tail

    xb = xr.astype(BF16)
    nblk = d // MXU_DIM
    r_parts, i_parts = [], []
    for g in range(nblk):
        seg = xb[:, g * MXU_DIM:(g + 1) * MXU_DIM]
        r_parts.append(jnp.dot(seg, wrg_ref[g], preferred_element_type=F32))
        i_parts.append(jnp.dot(seg, wig_ref[g], preferred_element_type=F32))
    r = _sigmoid(jnp.concatenate(r_parts, axis=-1) + brg_ref[...])
    ig = _sigmoid(jnp.concatenate(i_parts, axis=-1) + big_ref[...])
    log_sig_lam = -_softplus(-lam_ref[...])
    log_a = LRU_C * r * log_sig_lam
    a = jnp.exp(log_a)
    a_ref[...] = a
    u_ref[...] = jnp.sqrt(-jnp.tanh(log_a) * (a * a + 1.0)) * (ig * xr)

    def step(t, h):
        h = a_ref[pl.ds(t, 1), :] * h + u_ref[pl.ds(t, 1), :]
        u_ref[pl.ds(t, 1), :] = h
        return h

    h = lax.fori_loop(0, n, step, h_ref[...], unroll=8)
    h_ref[...] = h
    out_ref[...] = (u_ref[...] * _gelu_tanh(gate_ref[...])).astype(BF16)

    @pl.when(c == pl.num_programs(1) - 1)
    def _():
        nconv_ref[0] = tail
        hlast_ref[0] = h


def _lru_seq(gx, row0, bsz, seqlen, tl, conv_buf, h0, cw, cb, wrg, brg, wig, big, lam):
    d = cw.shape[1]
    nt = seqlen // tl
    blk0 = row0 // tl
    row_map = lambda b, c: (blk0 + b * nt + c, 0)
    vec = lambda a: a.reshape(1, d)
    full2 = lambda b, c: (0, 0)
    full3 = lambda b, c: (0, 0, 0)
    return pl.pallas_call(
        _lru_seq_kernel,
        grid=(bsz, nt),
        in_specs=[
            pl.BlockSpec((tl, d), row_map),
            pl.BlockSpec((tl, d), lambda b, c: (blk0 + b * nt + c, 1)),
            pl.BlockSpec((1, CONV_WIDTH - 1, d), lambda b, c: (b, 0, 0)),
            pl.BlockSpec((1, 1, d), lambda b, c: (b, 0, 0)),
            pl.BlockSpec((CONV_WIDTH, d), full2),
            pl.BlockSpec((1, d), full2),
            pl.BlockSpec(wrg.shape, full3),
            pl.BlockSpec((1, d), full2),
            pl.BlockSpec(wig.shape, full3),
            pl.BlockSpec((1, d), full2),
            pl.BlockSpec((1, d), full2),
        ],
        out_specs=[
            pl.BlockSpec((tl, d), lambda b, c: (b * nt + c, 0)),
            pl.BlockSpec((1, CONV_WIDTH - 1, d), lambda b, c: (b, 0, 0)),
            pl.BlockSpec((1, 1, d), lambda b, c: (b, 0, 0)),
        ],
        out_shape=[
            jax.ShapeDtypeStruct((bsz * seqlen, d), BF16),
            jax.ShapeDtypeStruct((bsz, CONV_WIDTH - 1, d), F32),
            jax.ShapeDtypeStruct((bsz, 1, d), F32),
        ],
        scratch_shapes=[
            pltpu.VMEM((tl + CONV_PAD, d), F32),
            pltpu.VMEM((1, d), F32),
            pltpu.VMEM((tl, d), F32),
            pltpu.VMEM((tl, d), F32),
        ],
        compiler_params=_params("arbitrary", "arbitrary"),
        name="lru_seq",
    )(gx, gx, conv_buf, h0.reshape(bsz, 1, d), cw, vec(cb), wrg, vec(brg), wig, vec(big), vec(lam))


def _block_diag_gate(w):
    nb, bw, _ = w.shape
    per = MXU_DIM // bw
    w4 = w.reshape(nb // per, per, bw, bw)
    eye = jnp.eye(per, dtype=w.dtype)
    out = w4[:, :, :, None, :] * eye[None, :, None, :, None]
    return out.reshape(nb // per, MXU_DIM, MXU_DIM).astype(BF16)


def _dot_sel(sel, x, sel_on_left=False, dims=None):
    p1 = x.astype(BF16)
    r1 = x - p1.astype(F32)
    p2 = r1.astype(BF16)
    p3 = (r1 - p2.astype(F32)).astype(BF16)
    out = None
    for p in (p1, p2, p3):
        a, b = (sel, p) if sel_on_left else (p, sel)
        if dims is None:
            term = jnp.dot(a, b, preferred_element_type=F32)
        else:
            term = lax.dot_general(a, b, dims, preferred_element_type=F32)
        out = term if out is None else out + term
    return out


def _ssd_seq_kernel(has_eq, z_ref, xs_ref, bc_ref, dt_ref, cbuf_ref, h0_ref, cw_ref, cb_ref, dtb_ref,
                    alog_ref, dskx_ref, nw_ref, ex_ref, *rest):
    eq_ref = rest[0] if has_eq else None
    out_ref, nconv_ref, st_ref, xx_ref = rest[-4:]
    c = pl.program_id(1)
    q = xs_ref.shape[0]
    d_inner = xs_ref.shape[1]
    n_heads = st_ref.shape[1]
    p_dim = st_ref.shape[2]
    n_state = st_ref.shape[3]
    hpg = n_heads // SSD_GROUPS

    @pl.when(c == 0)
    def _():
        xx_ref[CONV_HIST:CONV_PAD, :] = cbuf_ref[0]
        st_ref[...] = h0_ref[...]

    xx_ref[CONV_PAD:CONV_PAD + q, 0:d_inner] = xs_ref[...]
    xx_ref[CONV_PAD:CONV_PAD + q, d_inner:] = bc_ref[...]
    act = _silu(_conv_tile(xx_ref, q, cw_ref[...], cb_ref[...]))
    tail = xx_ref[q + CONV_HIST:q + CONV_PAD, :]
    xx_ref[CONV_HIST:CONV_PAD, :] = tail

    @pl.when(c == pl.num_programs(1) - 1)
    def _():
        nconv_ref[0] = tail

    dt = _softplus(dt_ref[...] + dtb_ref[...])
    adt = dt * (-jnp.exp(alog_ref[...]))
    row = lax.broadcasted_iota(jnp.int32, (q, q), 0)
    col = lax.broadcasted_iota(jnp.int32, (q, q), 1)
    tril = (col <= row).astype(BF16)
    cs = _dot_sel(tril, adt, sel_on_left=True)
    both_x = _dot_sel(ex_ref[...], jnp.concatenate([dt, cs], axis=0))
    dt_x = both_x[:q]
    cs_x = both_x[q:]
    if eq_ref is None:
        cs_q = cs_x
    else:
        cs_q = _dot_sel(eq_ref[...], cs)

    xs = act[:, :d_inner]
    xd = xs * dt_x
    xd_b = xd.astype(BF16)
    xdec_b = (xd * jnp.exp(cs_x[q - 1:q, :] - cs_x)).astype(BF16)
    ecs_x = jnp.exp(cs_x)

    head = lax.broadcasted_iota(jnp.int32, (q, n_heads), 1)
    even = (head & 1) == 0
    cs_par = jnp.concatenate([jnp.where(even, cs, 0.0), jnp.where(even, 0.0, cs)], axis=0)
    pair_sel = ((lax.broadcasted_iota(jnp.int32, (n_heads // 2, n_heads), 1) >> 1)
                == lax.broadcasted_iota(jnp.int32, (n_heads // 2, n_heads), 0)).astype(BF16)
    rtab = _dot_sel(pair_sel, cs_par, sel_on_left=True, dims=NT_DIMS)
    eye_h = (lax.broadcasted_iota(jnp.int32, (n_heads, n_heads), 0)
             == lax.broadcasted_iota(jnp.int32, (n_heads, n_heads), 1)).astype(BF16)
    cs_t = _dot_sel(eye_h, cs, sel_on_left=True, dims=NT_DIMS)
    st_decay = jnp.exp(jnp.broadcast_to(cs_t[:, q - 1:q], (n_heads, n_state)))

    row2 = lax.broadcasted_iota(jnp.int32, (q, 2 * q), 0)
    col2 = lax.broadcasted_iota(jnp.int32, (q, 2 * q), 1)
    causal2 = jnp.where(col2 >= q, col2 - q, col2) <= row2
    lane_p = lax.broadcasted_iota(jnp.int32, (q, 2 * p_dim), 1)
    lo_half = lane_p < p_dim
    zero_b = jnp.zeros((q, 2 * p_dim), BF16)
    b_off = d_inner
    c_off = d_inner + SSD_GROUPS * n_state
    gp = hpg * p_dim
    pairs_per_group = hpg // 2

    y_diag_parts, y_off_parts = [], []
    for g in range(SSD_GROUPS):
        bg = act[:, b_off + g * n_state:b_off + (g + 1) * n_state].astype(BF16)
        cg = act[:, c_off + g * n_state:c_off + (g + 1) * n_state].astype(BF16)
        cb2 = lax.dot_general(cg, jnp.concatenate([bg, bg], axis=0), NT_DIMS,
                              preferred_element_type=F32)
        prev = st_ref[0, g * hpg:(g + 1) * hpg].reshape(gp, n_state)
        y_off_parts.append(lax.dot_general(cg, prev.astype(BF16), NT_DIMS, preferred_element_type=F32))
        for jj in range(pairs_per_group):
            j = g * pairs_per_group + jj
            lmat = jnp.exp(jnp.where(causal2, cs_q[:, 2 * j * q:2 * (j + 1) * q] - rtab[j:j + 1, :], -jnp.inf))
            m = (cb2 * lmat).astype(BF16)
            pp = xd_b[:, 2 * j * p_dim:2 * (j + 1) * p_dim]
            blockdiag = jnp.concatenate([jnp.where(lo_half, pp, zero_b), jnp.where(lo_half, zero_b, pp)], axis=0)
            y_diag_parts.append(jnp.dot(m, blockdiag, preferred_element_type=F32))
        st_new = lax.dot_general(xdec_b[:, g * gp:(g + 1) * gp], bg, TN_DIMS,
                                 preferred_element_type=F32)
        for e in range(hpg):
            h = g * hpg + e
            st_ref[0, h] = (st_decay[h:h + 1, :] * prev[e * p_dim:(e + 1) * p_dim, :]
                            + st_new[e * p_dim:(e + 1) * p_dim, :])

    y = (jnp.concatenate(y_diag_parts, axis=-1) + jnp.concatenate(y_off_parts, axis=-1) * ecs_x
         + dskx_ref[...] * xs)
    gated = y * _silu(z_ref[...])
    gw = d_inner // SSD_GROUPS
    nw = nw_ref[...]
    for g in range(SSD_GROUPS):
        seg = gated[:, g * gw:(g + 1) * gw]
        ms = jnp.mean(seg * seg, axis=-1, keepdims=True)
        out_ref[:, g * gw:(g + 1) * gw] = (seg * lax.rsqrt(ms + RMS_EPS) * nw[:, g * gw:(g + 1) * gw]).astype(BF16)


def _ssd_seq(zx, dt_raw, row0, bsz, seqlen, q, conv_buf, h0, cw, cb, dtb, alog, dsk, nw):
    n_heads, p_dim, n_state = h0.shape[1:]
    d_inner = n_heads * p_dim
    conv_dim = cw.shape[1]
    nt = seqlen // q
    blk0 = row0 // q
    full2 = lambda b, c: (0, 0)
    col = lambda j: (lambda b, c: (blk0 + b * nt + c, j))
    head_ids = jnp.arange(n_heads, dtype=jnp.int32)[:, None]
    expand = lambda width: (jnp.arange(n_heads * width, dtype=jnp.int32)[None, :] // width == head_ids).astype(BF16)
    has_eq = q != p_dim
    extra_in = [expand(q)] if has_eq else []
    extra_specs = [pl.BlockSpec((n_heads, n_heads * q), full2)] if has_eq else []
    return pl.pallas_call(
        functools.partial(_ssd_seq_kernel, has_eq),
        grid=(bsz, nt),
        in_specs=[
            pl.BlockSpec((q, d_inner), col(0)),
            pl.BlockSpec((q, d_inner), col(1)),
            pl.BlockSpec((q, d_inner), col(2)),
            pl.BlockSpec((q, n_heads), col(0)),
            pl.BlockSpec((1, CONV_WIDTH - 1, conv_dim), lambda b, c: (b, 0, 0)),
            pl.BlockSpec((1, n_heads, p_dim, n_state), lambda b, c: (b, 0, 0, 0)),
            pl.BlockSpec((CONV_WIDTH, conv_dim), full2),
            pl.BlockSpec((1, conv_dim), full2),
            pl.BlockSpec((1, n_heads), full2),
            pl.BlockSpec((1, n_heads), full2),
            pl.BlockSpec((1, d_inner), full2),
            pl.BlockSpec((1, d_inner), full2),
            pl.BlockSpec((n_heads, d_inner), full2),
        ] + extra_specs,
        out_specs=[
            pl.BlockSpec((q, d_inner), lambda b, c: (b * nt + c, 0)),
            pl.BlockSpec((1, CONV_WIDTH - 1, conv_dim), lambda b, c: (b, 0, 0)),
            pl.BlockSpec((1, n_heads, p_dim, n_state), lambda b, c: (b, 0, 0, 0)),
        ],
        out_shape=[
            jax.ShapeDtypeStruct((bsz * seqlen, d_inner), BF16),
            jax.ShapeDtypeStruct((bsz, CONV_WIDTH - 1, conv_dim), F32),
            jax.ShapeDtypeStruct((bsz, n_heads, p_dim, n_state), F32),
        ],
        scratch_shapes=[
            pltpu.VMEM((q + CONV_PAD, conv_dim), F32),
        ],
        compiler_params=_params("arbitrary", "arbitrary"),
        name="ssd_seq",
    )(zx, zx, zx, dt_raw, conv_buf, h0, cw, cb.reshape(1, -1), dtb.reshape(1, -1),
      alog.reshape(1, -1), jnp.repeat(dsk, p_dim).reshape(1, -1), nw.reshape(1, -1), expand(p_dim), *extra_in)


def _first_argmax(v, rid, n):
    m = jnp.max(v, axis=0, keepdims=True)
    idx = jnp.min(jnp.where(v == m, rid, float(n)), axis=0, keepdims=True)
    return m, idx


def _route_tile(scores_t, bias_col):
    n_exp, tm = scores_t.shape
    per = n_exp // N_EXPERT_GROUPS
    neg = -jnp.inf
    biased = scores_t + bias_col
    rid_g = lax.broadcasted_iota(jnp.int32, (per, tm), 0).astype(F32)
    tiles, gs_rows = [], []
    for g in range(N_EXPERT_GROUPS):
        v = biased[g * per:(g + 1) * per, :]
        tiles.append(v)
        m1, i1 = _first_argmax(v, rid_g, per)
        m2 = jnp.max(jnp.where(rid_g == i1, neg, v), axis=0, keepdims=True)
        gs_rows.append(m1 + m2)
    gs = jnp.concatenate(gs_rows, axis=0)
    rid_grp = lax.broadcasted_iota(jnp.int32, (N_EXPERT_GROUPS, tm), 0).astype(F32)
    gsel = jnp.zeros((N_EXPERT_GROUPS, tm), F32)
    for _ in range(TOPK_GROUPS):
        _, gi = _first_argmax(gs, rid_grp, N_EXPERT_GROUPS)
        hit = rid_grp == gi
        gsel = jnp.where(hit, 1.0, gsel)
        gs = jnp.where(hit, neg, gs)
    masked = jnp.concatenate(
        [jnp.where(gsel[g:g + 1, :] > 0.0, tiles[g], neg) for g in range(N_EXPERT_GROUPS)], axis=0)
    rid = lax.broadcasted_iota(jnp.int32, (n_exp, tm), 0).astype(F32)
    sel = jnp.zeros((n_exp, tm), F32)
    ids, picked = [], []
    for _ in range(TOP_K):
        _, ei = _first_argmax(masked, rid, n_exp)
        hit = rid == ei
        ids.append(ei)
        picked.append(jnp.sum(jnp.where(hit, scores_t, 0.0), axis=0, keepdims=True))
        sel = jnp.where(hit, 1.0, sel)
        masked = jnp.where(hit, neg, masked)
    total = picked[0]
    for p in picked[1:]:
        total = total + p
    gates = [p / total * ROUTE_SCALE for p in picked]
    return sel, rid, ids, gates


def _proj_ln_router_kernel(alpha, n_a, ha_ref, hb_ref, w_ref, x_ref, g_ref, b_ref, rwt_ref, rb_ref,
                           o_ref, op_ref, eid_ref, gate_ref, rank_ref, cnt_ref, wb_ref, tri_ref, mix_ref):
    i = pl.program_id(0)
    tm = x_ref.shape[0]

    @pl.when(i == 0)
    def _():
        wb_ref[...] = w_ref[...].astype(BF16)
        r = lax.broadcasted_iota(jnp.int32, (tm, tm), 0)
        c = lax.broadcasted_iota(jnp.int32, (tm, tm), 1)
        tri_ref[...] = jnp.where(r < c, 1.0, 0.0).astype(BF16)
        cnt_ref[...] = jnp.zeros_like(cnt_ref)

    @pl.when(i < n_a)
    def _():
        mix_ref[...] = jnp.dot(ha_ref[...], wb_ref[...], preferred_element_type=F32)

    @pl.when(i >= n_a)
    def _():
        mix_ref[...] = jnp.dot(hb_ref[...], wb_ref[...], preferred_element_type=F32)

    x1 = _layer_norm(alpha * x_ref[...] + mix_ref[...], g_ref[...], b_ref[...])
    o_ref[...] = x1
    half = x1.shape[1] // 2
    op_ref[...] = _pack_pair(x1[:, :half], x1[:, half:])
    logits_t = lax.dot_general(rwt_ref[...].astype(BF16), x1.astype(BF16), NT_DIMS,
                               preferred_element_type=F32)
    sel, rid, ids, gates = _route_tile(_sigmoid(logits_t), rb_ref[...])

    sel_b = sel.astype(BF16)
    before = jnp.dot(sel_b, tri_ref[...], preferred_element_type=F32) + cnt_ref[:, 0:1]
    pad_rows = eid_ref.shape[0] - TOP_K
    zrow = jnp.zeros((pad_rows, tm), F32)
    ranks = [jnp.sum(jnp.where(rid == ei, before, 0.0), axis=0, keepdims=True) for ei in ids]
    eid_ref[...] = jnp.concatenate(ids + [zrow], axis=0).astype(jnp.int32)
    gate_ref[...] = jnp.concatenate(gates + [zrow], axis=0)
    rank_ref[...] = jnp.concatenate(ranks + [zrow], axis=0).astype(jnp.int32)
    ones = jnp.ones((tm, cnt_ref.shape[1]), BF16)
    cnt_ref[...] = cnt_ref[...] + jnp.dot(sel_b, ones, preferred_element_type=F32)


def _proj_ln_router(ha, hb, w, wl, x, alpha, ln_g, ln_b, router_w_t, router_bias_col):
    K = ha.shape[1]
    T, D = x.shape
    E = router_w_t.shape[0]
    tm = TM_PROJ
    kp = SUBLANE
    n_a = ha.shape[0] // tm
    n_b = hb.shape[0] // tm
    assert ha.shape[0] == n_a * tm and hb.shape[0] == n_b * tm and (n_a + n_b) * tm == T
    row_spec = pl.BlockSpec((kp, tm), lambda i: (0, i))
    return pl.pallas_call(
        functools.partial(_proj_ln_router_kernel, alpha, n_a),
        grid=(T // tm,),
        in_specs=[
            pl.BlockSpec((tm, K), lambda i: (jnp.minimum(i, n_a - 1), 0)),
            pl.BlockSpec((tm, K), lambda i: (jnp.maximum(i - n_a, 0), 0)),
            pl.BlockSpec((None, K, D), lambda i: (wl, 0, 0)),
            pl.BlockSpec((tm, D), lambda i: (i, 0)),
            pl.BlockSpec((1, D), lambda i: (0, 0)),
            pl.BlockSpec((1, D), lambda i: (0, 0)),
            pl.BlockSpec((E, D), lambda i: (0, 0)),
            pl.BlockSpec((E, 1), lambda i: (0, 0)),
        ],
        out_specs=[pl.BlockSpec((tm, D), lambda i: (i, 0)), pl.BlockSpec((tm, D // 2), lambda i: (i, 0)),
                   row_spec, row_spec, row_spec, pl.BlockSpec((E, LANE), lambda i: (0, 0))],
        out_shape=[jax.ShapeDtypeStruct((T, D), F32),
                   jax.ShapeDtypeStruct((T, D // 2), jnp.uint32),
                   jax.ShapeDtypeStruct((kp, T), jnp.int32),
                   jax.ShapeDtypeStruct((kp, T), F32),
                   jax.ShapeDtypeStruct((kp, T), jnp.int32),
                   jax.ShapeDtypeStruct((E, LANE), F32)],
        scratch_shapes=[pltpu.VMEM((K, D), BF16), pltpu.VMEM((tm, tm), BF16), pltpu.VMEM((tm, D), F32)],
        compiler_params=_params("arbitrary"),
        name="proj_ln_router",
    )(ha, hb, w, x, ln_g.reshape(1, D), ln_b.reshape(1, D), router_w_t, router_bias_col)


def _start_row_gather(idx_ref, base, n, src_hbm, dst, sem):
    for i in range(n):
        r = idx_ref[base + i]
        pltpu.make_async_copy(src_hbm.at[pl.ds(r, 1), :], dst.at[pl.ds(i, 1), :], sem).start(priority=i % 2)


def _wait_row_gather(n, src_hbm, dst, sem):
    pltpu.make_async_copy(src_hbm.at[pl.ds(0, n), :], dst, sem).wait()


def _scatter_kernel(n_tiles, pos_ref, zb_ref, nu_ref, x_ref, xs_hbm, xbuf, zbuf, sem, zsem):
    i = pl.program_id(0)
    tm = x_ref.shape[0]
    n_rows = TOP_K * tm
    bm = zbuf.shape[0]
    n_experts = zb_ref.shape[0]
    n_blocks = xs_hbm.shape[0] // bm
    slot = lax.rem(i, 2)

    def zero_block(blk):
        start = pl.multiple_of(blk * bm, bm)
        return pltpu.make_async_copy(zbuf, xs_hbm.at[pl.ds(start, bm), :], zsem)

    def wait_rows(s):
        pltpu.make_async_copy(xs_hbm.at[pl.ds(0, n_rows), :], xs_hbm.at[pl.ds(0, n_rows), :], sem.at[s]).wait()

    @pl.when(i == 0)
    def _():
        zbuf[...] = jnp.zeros_like(zbuf)
        for e in range(n_experts):
            @pl.when(zb_ref[e] >= 0)
            def _():
                zero_block(zb_ref[e]).start()

        def start_tail(blk, carry):
            zero_block(blk).start()
            return carry

        def wait_tail(blk, carry):
            zero_block(blk).wait()
            return carry

        lax.fori_loop(nu_ref[0], n_blocks, start_tail, 0)
        for e in range(n_experts):
            @pl.when(zb_ref[e] >= 0)
            def _():
                zero_block(zb_ref[e]).wait()
        lax.fori_loop(nu_ref[0], n_blocks, wait_tail, 0)

    @pl.when(i >= 2)
    def _():
        wait_rows(slot)

    xbuf[slot] = x_ref[...]
    for k in range(TOP_K):
        for t in range(tm):
            r = pos_ref[i * n_rows + k * tm + t]
            pltpu.make_async_copy(xbuf.at[slot, pl.ds(t, 1), :], xs_hbm.at[pl.ds(r, 1), :],
                                  sem.at[slot]).start(priority=t % 2)

    @pl.when(i == n_tiles - 1)
    def _():
        wait_rows(slot)
        if n_tiles >= 2:
            wait_rows(1 - slot)


def _scatter_rows(x, pos_tiles, zero_blk, n_used, n_slots):
    T, D = x.shape
    tm = TM_COMBINE
    n_tiles = T // tm
    grid_spec = pltpu.PrefetchScalarGridSpec(
        num_scalar_prefetch=3,
        grid=(n_tiles,),
        in_specs=[pl.BlockSpec((tm, D), lambda i, pos, zb, nu: (i, 0))],
        out_specs=pl.BlockSpec(memory_space=pl.ANY),
        scratch_shapes=[
            pltpu.VMEM((2, tm, D), x.dtype),
            pltpu.VMEM((BM_EXPERT, D), x.dtype),
            pltpu.SemaphoreType.DMA((2,)),
            pltpu.SemaphoreType.DMA,
        ],
    )
    return pl.pallas_call(
        functools.partial(_scatter_kernel, n_tiles),
        grid_spec=grid_spec,
        out_shape=jax.ShapeDtypeStruct((n_slots, D), x.dtype),
        compiler_params=_params("arbitrary"),
        name="scatter_rows",
    )(pos_tiles, zero_blk, n_used, x)


def _expert_kernel(layer, be_ref, nu_ref, nx_ref, x_ref, wg_hbm, wu_hbm, wd_hbm, y_ref,
                   sg32, su32, sd32, wgb, wub, wdb, sem):
    b = pl.program_id(0)
    n_used = nu_ref[0]

    def weight_copies(e):
        return (pltpu.make_async_copy(wg_hbm.at[layer, e], sg32, sem),
                pltpu.make_async_copy(wu_hbm.at[layer, e], su32, sem),
                pltpu.make_async_copy(wd_hbm.at[layer, e], sd32, sem))

    @pl.when(jnp.logical_and(b == 0, n_used > 0))
    def _():
        for cp in weight_copies(be_ref[0]):
            cp.start()

    @pl.when(b < n_used)
    def _():
        new_expert = jnp.logical_or(b == 0, be_ref[b] != be_ref[jnp.maximum(b - 1, 0)])

        @pl.when(new_expert)
        def _():
            for cp in weight_copies(be_ref[b]):
                cp.wait()
            wgb[...] = sg32[...].astype(BF16)
            wub[...] = su32[...].astype(BF16)
            wdb[...] = sd32[...].astype(BF16)

            @pl.when(nx_ref[b] >= 0)
            def _():
                for cp in weight_copies(nx_ref[b]):
                    cp.start()

        half = x_ref.shape[1]
        x_lo, x_hi = _unpack_pair(x_ref[...])
        x_lo = x_lo.astype(BF16)
        x_hi = x_hi.astype(BF16)
        hg = (jnp.dot(x_lo, wgb[0:half, :], preferred_element_type=F32)
              + jnp.dot(x_hi, wgb[half:, :], preferred_element_type=F32))
        hu = (jnp.dot(x_lo, wub[0:half, :], preferred_element_type=F32)
              + jnp.dot(x_hi, wub[half:, :], preferred_element_type=F32))
        hh = (_silu(hg) * hu).astype(BF16)
        y = jnp.dot(hh, wdb[...], preferred_element_type=F32)
        y_ref[...] = _pack_pair(y[:, :half], y[:, half:])

    @pl.when(b >= n_used)
    def _():
        y_ref[...] = jnp.zeros_like(y_ref)


def _experts(x_sorted, block_e, n_used, next_e, wg, wu, wd, layer):
    n_slots, dh = x_sorted.shape
    D, DE = wg.shape[2:]
    bm = BM_EXPERT
    nb = n_slots // bm
    grid_spec = pltpu.PrefetchScalarGridSpec(
        num_scalar_prefetch=3,
        grid=(nb,),
        in_specs=[
            pl.BlockSpec((bm, dh), lambda b, be, nu, nx: (jnp.minimum(b, nu[0] - 1), 0)),
            pl.BlockSpec(memory_space=pl.ANY),
            pl.BlockSpec(memory_space=pl.ANY),
            pl.BlockSpec(memory_space=pl.ANY),
        ],
        out_specs=pl.BlockSpec((bm, dh), lambda b, be, nu, nx: (b, 0)),
        scratch_shapes=[
            pltpu.VMEM((D, DE), F32),
            pltpu.VMEM((D, DE), F32),
            pltpu.VMEM((DE, D), F32),
            pltpu.VMEM((D, DE), BF16),
            pltpu.VMEM((D, DE), BF16),
            pltpu.VMEM((DE, D), BF16),
            pltpu.SemaphoreType.DMA,
        ],
    )
    return pl.pallas_call(
        functools.partial(_expert_kernel, layer),
        grid_spec=grid_spec,
        out_shape=jax.ShapeDtypeStruct((n_slots, dh), jnp.uint32),
        compiler_params=_params("arbitrary"),
        name="experts",
    )(block_e, n_used, next_e, x_sorted, wg, wu, wd)


def _combine_kernel(alpha, n_tiles, pos_ref, gates_ref, x_ref, y_hbm, sg_ref, su_ref, sd_ref, g_ref, b_ref,
                    o_ref, ob_ref, ybuf_a, ybuf_b, sgb, sub, sdb, sem):
    i = pl.program_id(0)
    tm = x_ref.shape[0] // 2
    n_rows = ybuf_a.shape[0]
    first = 2 * i

    @pl.when(i == 0)
    def _():
        _start_row_gather(pos_ref, 0, n_rows, y_hbm, ybuf_a, sem.at[0])
        sgb[...] = sg_ref[...].astype(BF16)
        sub[...] = su_ref[...].astype(BF16)
        sdb[...] = sd_ref[...].astype(BF16)

    def tile(r0, ybuf):
        x = x_ref[r0:r0 + tm, :]
        xb = x.astype(BF16)
        hg = jnp.dot(xb, sgb[...], preferred_element_type=F32)
        hu = jnp.dot(xb, sub[...], preferred_element_type=F32)
        acc = jnp.dot((_silu(hg) * hu).astype(BF16), sdb[...], preferred_element_type=F32)
        gates = gates_ref[r0:r0 + tm, :]
        routed_lo = routed_hi = None
        for k in range(TOP_K):
            y_lo, y_hi = _unpack_pair(ybuf[k * tm:(k + 1) * tm, :])
            gk = gates[:, k:k + 1]
            routed_lo = gk * y_lo if routed_lo is None else routed_lo + gk * y_lo
            routed_hi = gk * y_hi if routed_hi is None else routed_hi + gk * y_hi
        acc = acc + jnp.concatenate([routed_lo, routed_hi], axis=-1)
        out = _layer_norm(alpha * x + acc, g_ref[...], b_ref[...])
        o_ref[r0:r0 + tm, :] = out
        ob_ref[r0:r0 + tm, :] = out.astype(BF16)

    _wait_row_gather(n_rows, y_hbm, ybuf_a, sem.at[0])
    _start_row_gather(pos_ref, (first + 1) * n_rows, n_rows, y_hbm, ybuf_b, sem.at[1])
    tile(0, ybuf_a)
    _wait_row_gather(n_rows, y_hbm, ybuf_b, sem.at[1])
    nxt = jnp.minimum(first + 2, n_tiles - 2)
    _start_row_gather(pos_ref, nxt * n_rows, n_rows, y_hbm, ybuf_a, sem.at[0])
    tile(tm, ybuf_b)

    @pl.when(i == pl.num_programs(0) - 1)
    def _():
        _wait_row_gather(n_rows, y_hbm, ybuf_a, sem.at[0])


def _combine(x, y_sorted, pos_tiles, gates, sg, su, sd, alpha, ln_g, ln_b, layer):
    T, D = x.shape
    DS = sg.shape[2]
    tm = TM_COMBINE
    n_tiles = T // tm
    assert n_tiles % 2 == 0, "combine walks token tiles in pairs"
    grid_spec = pltpu.PrefetchScalarGridSpec(
        num_scalar_prefetch=1,
        grid=(n_tiles // 2,),
        in_specs=[
            pl.BlockSpec((2 * tm, TOP_K), lambda i, pos: (i, 0)),
            pl.BlockSpec((2 * tm, D), lambda i, pos: (i, 0)),
            pl.BlockSpec(memory_space=pl.ANY),
            pl.BlockSpec((None, D, DS), lambda i, pos: (layer, 0, 0)),
            pl.BlockSpec((None, D, DS), lambda i, pos: (layer, 0, 0)),
            pl.BlockSpec((None, DS, D), lambda i, pos: (layer, 0, 0)),
            pl.BlockSpec((None, 1, D), lambda i, pos: (layer, 0, 0)),
            pl.BlockSpec((None, 1, D), lambda i, pos: (layer, 0, 0)),
        ],
        out_specs=[pl.BlockSpec((2 * tm, D), lambda i, pos: (i, 0)), pl.BlockSpec((2 * tm, D), lambda i, pos: (i, 0))],
        scratch_shapes=[
            pltpu.VMEM((TOP_K * tm, y_sorted.shape[1]), y_sorted.dtype),
            pltpu.VMEM((TOP_K * tm, y_sorted.shape[1]), y_sorted.dtype),
            pltpu.VMEM((D, DS), BF16),
            pltpu.VMEM((D, DS), BF16),
            pltpu.VMEM((DS, D), BF16),
            pltpu.SemaphoreType.DMA((2,)),
        ],
    )
    return pl.pallas_call(
        functools.partial(_combine_kernel, alpha, n_tiles),
        grid_spec=grid_spec,
        out_shape=[jax.ShapeDtypeStruct((T, D), F32), jax.ShapeDtypeStruct((T, D), BF16)],
        compiler_params=_params("arbitrary"),
        name="combine",
    )(pos_tiles, gates, x, y_sorted, sg, su, sd, ln_g.reshape(-1, 1, D), ln_b.reshape(-1, 1, D))


def _dispatch(eid_t, rank_t, counts):
    T = eid_t.shape[1]
    E = counts.shape[0]
    A = T * TOP_K
    bm = BM_EXPERT
    nb = -(-A // bm) + E
    cnt = counts[:, 0].astype(jnp.int32)
    padded = (cnt + bm - 1) // bm * bm
    pends = jnp.cumsum(padded)
    pstart = pends - padded
    eid = eid_t[:TOP_K]
    onehot = eid[:, :, None] == jnp.arange(E, dtype=jnp.int32)[None, None, :]
    dest = rank_t[:TOP_K] + jnp.sum(jnp.where(onehot, pstart[None, None, :], 0), axis=-1)
    block_start = jnp.arange(nb, dtype=jnp.int32) * bm
    block_e = jnp.minimum(jnp.sum((pends[None, :] <= block_start[:, None]).astype(jnp.int32), axis=1), E - 1)
    n_used = (pends[-1:] // bm).astype(jnp.int32)
    zero_blk = jnp.where(padded > 0, pends // bm - 1, -1).astype(jnp.int32)
    ids = jnp.where(padded > 0, jnp.arange(E, dtype=jnp.int32), E)
    later = jnp.concatenate([lax.cummin(ids[::-1])[::-1][1:], jnp.full((1,), E, jnp.int32)])
    next_e = jnp.where(later >= E, -1, later)[block_e].astype(jnp.int32)
    tm = TM_COMBINE
    pos_tiles = dest.reshape(TOP_K, T // tm, tm).transpose(1, 0, 2).reshape(-1)
    return nb * bm, block_e, n_used, next_e, zero_blk, pos_tiles


def _moe(x1, x1_packed, eid_t, gate_t, rank_t, counts, alpha, layer, wg, wu, wd, sg, su, sd, ln_g, ln_b):
    n_slots, block_e, n_used, next_e, zero_blk, pos_tiles = _dispatch(eid_t, rank_t, counts)
    x_sorted = _scatter_rows(x1_packed, pos_tiles, zero_blk, n_used, n_slots)
    y_sorted = _experts(x_sorted, block_e, n_used, next_e, wg, wu, wd, layer)
    gates = gate_t[:TOP_K].T
    return _combine(x1, y_sorted, pos_tiles, gates, sg, su, sd, alpha, ln_g, ln_b, layer)


def kernel(x_prompt, x_sample, state_lru_conv, state_lru_h, state_ssd_conv, state_ssd, lru_w_in, lru_conv_w, lru_conv_b, lru_w_rgate, lru_b_rgate, lru_w_igate, lru_b_igate, lru_lambda, lru_w_out, ssd_w_in, ssd_conv_w, ssd_conv_b, ssd_dt_bias, ssd_a_log, ssd_d, ssd_norm_w, ssd_w_out, ln_mix_g, ln_mix_b, ln_ffn_g, ln_ffn_b, router_w, router_bias, moe_w_gate, moe_w_up, moe_w_down, shared_w_gate, shared_w_up, shared_w_down):
    bp, lp, d_model = x_prompt.shape
    bs, ls, _ = x_sample.shape
    depth = ln_mix_g.shape[0]
    alpha = (2.0 * depth) ** 0.25
    tp, ts = bp * lp, bs * ls
    n_heads, p_dim, n_state = state_ssd.shape[2:]
    d_inner = n_heads * p_dim
    conv_dim = ssd_conv_w.shape[2]
    d_rnn = lru_conv_w.shape[2]

    x = jnp.concatenate([x_prompt.reshape(tp, d_model), x_sample.reshape(ts, d_model)], axis=0)
    tl_p = min(TL_LRU, lp)
    tl_s = min(TL_LRU, ls)
    q_p = min(SSD_CHUNK, lp)
    q_s = min(SSD_CHUNK, ls)

    x_mxu = x
    p_lru_conv, p_lru_h, p_ssd_conv, p_ssd = [], [], [], []
    s_lru_conv, s_lru_h, s_ssd_conv, s_ssd = [], [], [], []
    for i in range(depth):
        j = i // 2
        if i % 2 == 0:
            gx = _in_proj(x_mxu, lru_w_in, j, 2 * d_rnn, TN_LRU_IN)
            wrg = _block_diag_gate(lru_w_rgate[j])
            wig = _block_diag_gate(lru_w_igate[j])
            common = (lru_conv_w[j], lru_conv_b[j], wrg, lru_b_rgate[j], wig, lru_b_igate[j], lru_lambda[j])
            hp, cp, lp_h = _lru_seq(gx, 0, bp, lp, tl_p, jnp.zeros((bp, CONV_WIDTH - 1, d_rnn), F32),
                                    jnp.zeros((bp, d_rnn), F32), *common)
            hs, cs_, ls_h = _lru_seq(gx, tp, bs, ls, tl_s, state_lru_conv[j], state_lru_h[j], *common)
            p_lru_conv.append(cp)
            p_lru_h.append(lp_h.reshape(bp, d_rnn))
            s_lru_conv.append(cs_)
            s_lru_h.append(ls_h.reshape(bs, d_rnn))
            w_out, wl = lru_w_out, j
        else:
            zx = _in_proj(x_mxu, ssd_w_in, j, d_inner + conv_dim, TN_SSD_IN)
            dt_raw = _small_proj(x_mxu, ssd_w_in[j][:, d_inner + conv_dim:])
            common = (ssd_conv_w[j], ssd_conv_b[j], ssd_dt_bias[j], ssd_a_log[j], ssd_d[j], ssd_norm_w[j])
            hp, cp, sp = _ssd_seq(zx, dt_raw, 0, bp, lp, q_p, jnp.zeros((bp, CONV_WIDTH - 1, conv_dim), F32),
                                  jnp.zeros((bp, n_heads, p_dim, n_state), F32), *common)
            hs, cs_, ss = _ssd_seq(zx, dt_raw, tp, bs, ls, q_s, state_ssd_conv[j], state_ssd[j], *common)
            p_ssd_conv.append(cp)
            p_ssd.append(sp)
            s_ssd_conv.append(cs_)
            s_ssd.append(ss)
            w_out, wl = ssd_w_out, j
        x1, x1_packed, eid_t, gate_t, rank_t, counts = _proj_ln_router(
            hp, hs, w_out, wl, x, alpha, ln_mix_g[i], ln_mix_b[i], router_w[i].T, router_bias[i].reshape(-1, 1))
        x, x_mxu = _moe(x1, x1_packed, eid_t, gate_t, rank_t, counts, alpha, i, moe_w_gate, moe_w_up, moe_w_down,
                 shared_w_gate, shared_w_up, shared_w_down, ln_ffn_g, ln_ffn_b)

    y_prompt = x[:tp].reshape(bp, lp, d_model)
    y_sample = x[tp:].reshape(bs, ls, d_model)
    return (y_prompt, y_sample,
            jnp.stack(p_lru_conv), jnp.stack(p_lru_h), jnp.stack(p_ssd_conv), jnp.stack(p_ssd),
            jnp.stack(s_lru_conv), jnp.stack(s_lru_h), jnp.stack(s_ssd_conv), jnp.stack(s_ssd))
```

```python
import functools
import math

import jax
import jax.numpy as jnp
from jax import lax
from jax.experimental import pallas as pl
from jax.experimental.pallas import tpu as pltpu

F32 = jnp.float32
BF16 = jnp.bfloat16

LN_EPS = 1e-5
RMS_EPS = 1e-5
CONV_WIDTH = 4
LRU_C = 8.0
SSD_CHUNK = 64
SSD_GROUPS = 8
TOP_K = 6
N_EXPERT_GROUPS = 8
TOPK_GROUPS = 4
ROUTE_SCALE = 1.0

LANE = 128
SUBLANE = 8
MXU_DIM = 256
VMEM_LIMIT = 56 * 1024 * 1024

TM_PROJ = 512
TM_COMBINE = 128
BM_EXPERT = 512
TL_LRU = 256
TM_IN_PROJ = 1024
TN_LRU_IN = 2048
TN_SSD_IN = 2048

HIGHEST = lax.Precision.HIGHEST
NT_DIMS = (((1,), (1,)), ((), ()))
TN_DIMS = (((0,), (0,)), ((), ()))


def _sigmoid(x):
    return 1.0 / (1.0 + jnp.exp(-x))


def _silu(x):
    return x * _sigmoid(x)


def _softplus(x):
    return jnp.maximum(x, 0.0) + jnp.log1p(jnp.exp(-jnp.abs(x)))


def _gelu_tanh(x):
    c = math.sqrt(2.0 / math.pi)
    return 0.5 * x * (1.0 + jnp.tanh(c * (x + 0.044715 * (x * x * x))))


def _layer_norm(v, g, b):
    mu = jnp.mean(v, axis=-1, keepdims=True)
    d = v - mu
    var = jnp.mean(d * d, axis=-1, keepdims=True)
    return d * lax.rsqrt(var + LN_EPS) * g + b


def _pack_pair(lo, hi):
    lo_bits = lax.bitcast_convert_type(lo.astype(BF16).astype(F32), jnp.uint32)
    hi_bits = lax.bitcast_convert_type(hi.astype(BF16).astype(F32), jnp.uint32)
    return hi_bits | (lo_bits >> 16)


def _unpack_pair(w):
    lo = lax.bitcast_convert_type(w << 16, F32)
    hi = lax.bitcast_convert_type(w & jnp.uint32(0xFFFF0000), F32)
    return lo, hi


def _params(*sem):
    return pltpu.CompilerParams(dimension_semantics=sem, vmem_limit_bytes=VMEM_LIMIT)


def _mm_kernel(x_ref, w_ref, o_ref):
    o_ref[...] = jnp.dot(x_ref[...].astype(BF16), w_ref[...].astype(BF16),
                         preferred_element_type=F32)


def _in_proj_kernel(x_ref, w_ref, o_ref, wb_ref):
    @pl.when(pl.program_id(1) == 0)
    def _():
        wb_ref[...] = w_ref[...].astype(BF16)

    o_ref[...] = jnp.dot(x_ref[...].astype(BF16), wb_ref[...], preferred_element_type=F32)


def _in_proj(x, w, layer, n_cols, tn):
    T, K = x.shape
    tm = TM_IN_PROJ if T % TM_IN_PROJ == 0 else TM_PROJ
    return pl.pallas_call(
        _in_proj_kernel,
        grid=(n_cols // tn, T // tm),
        in_specs=[pl.BlockSpec((tm, K), lambda j, i: (i, 0)),
                  pl.BlockSpec((None, K, tn), lambda j, i: (layer, 0, j))],
        out_specs=pl.BlockSpec((tm, tn), lambda j, i: (i, j)),
        out_shape=jax.ShapeDtypeStruct((T, n_cols), F32),
        scratch_shapes=[pltpu.VMEM((K, tn), BF16)],
        compiler_params=_params("arbitrary", "arbitrary"),
        name="in_proj",
    )(x, w)


def _small_proj(x, w):
    T, K = x.shape
    N = w.shape[1]
    tm = TM_PROJ
    return pl.pallas_call(
        _mm_kernel,
        grid=(T // tm,),
        in_specs=[pl.BlockSpec((tm, K), lambda i: (i, 0)),
                  pl.BlockSpec((K, N), lambda i: (0, 0))],
        out_specs=pl.BlockSpec((tm, N), lambda i: (i, 0)),
        out_shape=jax.ShapeDtypeStruct((T, N), F32),
        compiler_params=_params("parallel"),
        name="dt_proj",
    )(x, w)


CONV_PAD = SUBLANE
CONV_HIST = CONV_PAD - (CONV_WIDTH - 1)


def _conv_tile(xx_ref, n, cw, cb):
    acc = cb + cw[CONV_WIDTH - 1:CONV_WIDTH] * xx_ref[CONV_PAD:CONV_PAD + n, :]
    for k in range(CONV_WIDTH - 1):
        off = CONV_HIST + k
        acc = acc + cw[k:k + 1] * xx_ref[off:off + n, :]
    return acc


def _lru_seq_kernel(gate_ref, xr_ref, cbuf_ref, h0_ref, cw_ref, cb_ref, wrg_ref, brg_ref,
                    wig_ref, big_ref, lam_ref, out_ref, nconv_ref, hlast_ref,
                    xx_ref, h_ref, a_ref, u_ref):
    c = pl.program_id(1)
    n = xr_ref.shape[0]
    d = xr_ref.shape[1]

    @pl.when(c == 0)
    def _():
        xx_ref[CONV_HIST:CONV_PAD, :] = cbuf_ref[0]
        h_ref[...] = h0_ref[0]

    xx_ref[CONV_PAD:CONV_PAD + n, :] = xr_ref[...]
    xr = _conv_tile(xx_ref, n, cw_ref[...], cb_ref[...])
    tail = xx_ref[n + CONV_HIST:n + CONV_PAD, :]
    xx_ref[CONV_HIST:CONV_PAD, :] = tail

    xb = xr.astype(BF16)
    nblk = d // MXU_DIM
    r_parts, i_parts = [], []
    for g in range(nblk):
        seg = xb[:, g * MXU_DIM:(g + 1) * MXU_DIM]
        r_parts.append(jnp.dot(seg, wrg_ref[g], preferred_element_type=F32))
        i_parts.append(jnp.dot(seg, wig_ref[g], preferred_element_type=F32))
    r = _sigmoid(jnp.concatenate(r_parts, axis=-1) + brg_ref[...])
    ig = _sigmoid(jnp.concatenate(i_parts, axis=-1) + big_ref[...])
    log_sig_lam = -_softplus(-lam_ref[...])
    log_a = LRU_C * r * log_sig_lam
    a = jnp.exp(log_a)
    a_ref[...] = a
    u_ref[...] = jnp.sqrt(-jnp.tanh(log_a) * (a * a + 1.0)) * (ig * xr)

    def step(t, h):
        h = a_ref[pl.ds(t, 1), :] * h + u_ref[pl.ds(t, 1), :]
        u_ref[pl.ds(t, 1), :] = h
        return h

    h = lax.fori_loop(0, n, step, h_ref[...], unroll=8)
    h_ref[...] = h
    out_ref[...] = (u_ref[...] * _gelu_tanh(gate_ref[...])).astype(BF16)

    @pl.when(c == pl.num_programs(1) - 1)
    def _():
        nconv_ref[0] = tail
        hlast_ref[0] = h


def _lru_seq(gx, row0, bsz, seqlen, tl, conv_buf, h0, cw, cb, wrg, brg, wig, big, lam):
    d = cw.shape[1]
    nt = seqlen // tl
    blk0 = row0 // tl
    row_map = lambda b, c: (blk0 + b * nt + c, 0)
    vec = lambda a: a.reshape(1, d)
    full2 = lambda b, c: (0, 0)
    full3 = lambda b, c: (0, 0, 0)
    return pl.pallas_call(
        _lru_seq_kernel,
        grid=(bsz, nt),
        in_specs=[
            pl.BlockSpec((tl, d), row_map),
            pl.BlockSpec((tl, d), lambda b, c: (blk0 + b * nt + c, 1)),
            pl.BlockSpec((1, CONV_WIDTH - 1, d), lambda b, c: (b, 0, 0)),
            pl.BlockSpec((1, 1, d), lambda b, c: (b, 0, 0)),
            pl.BlockSpec((CONV_WIDTH, d), full2),
            pl.BlockSpec((1, d), full2),
            pl.BlockSpec(wrg.shape, full3),
            pl.BlockSpec((1, d), full2),
            pl.BlockSpec(wig.shape, full3),
            pl.BlockSpec((1, d), full2),
            pl.BlockSpec((1, d), full2),
        ],
        out_specs=[
            pl.BlockSpec((tl, d), lambda b, c: (b * nt + c, 0)),
            pl.BlockSpec((1, CONV_WIDTH - 1, d), lambda b, c: (b, 0, 0)),
            pl.BlockSpec((1, 1, d), lambda b, c: (b, 0, 0)),
        ],
        out_shape=[
            jax.ShapeDtypeStruct((bsz * seqlen, d), BF16),
            jax.ShapeDtypeStruct((bsz, CONV_WIDTH - 1, d), F32),
            jax.ShapeDtypeStruct((bsz, 1, d), F32),
        ],
        scratch_shapes=[
            pltpu.VMEM((tl + CONV_PAD, d), F32),
            pltpu.VMEM((1, d), F32),
            pltpu.VMEM((tl, d), F32),
            pltpu.VMEM((tl, d), F32),
        ],
        compiler_params=_params("arbitrary", "arbitrary"),
        name="lru_seq",
    )(gx, gx, conv_buf, h0.reshape(bsz, 1, d), cw, vec(cb), wrg, vec(brg), wig, vec(big), vec(lam))


def _block_diag_gate(w):
    nb, bw, _ = w.shape
    per = MXU_DIM // bw
    w4 = w.reshape(nb // per, per, bw, bw)
    eye = jnp.eye(per, dtype=w.dtype)
    out = w4[:, :, :, None, :] * eye[None, :, None, :, None]
    return out.reshape(nb // per, MXU_DIM, MXU_DIM).astype(BF16)


def _dot_sel(sel, x, sel_on_left=False, dims=None):
    p1 = x.astype(BF16)
    r1 = x - p1.astype(F32)
    p2 = r1.astype(BF16)
    p3 = (r1 - p2.astype(F32)).astype(BF16)
    out = None
    for p in (p1, p2, p3):
        a, b = (sel, p) if sel_on_left else (p, sel)
        if dims is None:
            term = jnp.dot(a, b, preferred_element_type=F32)
        else:
            term = lax.dot_general(a, b, dims, preferred_element_type=F32)
        out = term if out is None else out + term
    return out


def _ssd_seq_kernel(has_eq, z_ref, xs_ref, bc_ref, dt_ref, cbuf_ref, h0_ref, cw_ref, cb_ref, dtb_ref,
                    alog_ref, dskx_ref, nw_ref, ex_ref, *rest):
    eq_ref = rest[0] if has_eq else None
    out_ref, nconv_ref, st_ref, xx_ref = rest[-4:]
    c = pl.program_id(1)
    q = xs_ref.shape[0]
    d_inner = xs_ref.shape[1]
    n_heads = st_ref.shape[1]
    p_dim = st_ref.shape[2]
    n_state = st_ref.shape[3]
    hpg = n_heads // SSD_GROUPS

    @pl.when(c == 0)
    def _():
        xx_ref[CONV_HIST:CONV_PAD, :] = cbuf_ref[0]
        st_ref[...] = h0_ref[...]

    xx_ref[CONV_PAD:CONV_PAD + q, 0:d_inner] = xs_ref[...]
    xx_ref[CONV_PAD:CONV_PAD + q, d_inner:] = bc_ref[...]
    act = _silu(_conv_tile(xx_ref, q, cw_ref[...], cb_ref[...]))
    tail = xx_ref[q + CONV_HIST:q + CONV_PAD, :]
    xx_ref[CONV_HIST:CONV_PAD, :] = tail

    @pl.when(c == pl.num_programs(1) - 1)
    def _():
        nconv_ref[0] = tail

    dt = _softplus(dt_ref[...] + dtb_ref[...])
    adt = dt * (-jnp.exp(alog_ref[...]))
    row = lax.broadcasted_iota(jnp.int32, (q, q), 0)
    col = lax.broadcasted_iota(jnp.int32, (q, q), 1)
    tril = (col <= row).astype(BF16)
    cs = _dot_sel(tril, adt, sel_on_left=True)
    both_x = _dot_sel(ex_ref[...], jnp.concatenate([dt, cs], axis=0))
    dt_x = both_x[:q]
    cs_x = both_x[q:]
    if eq_ref is None:
        cs_q = cs_x
    else:
        cs_q = _dot_sel(eq_ref[...], cs)

    xs = act[:, :d_inner]
    xd = xs * dt_x
    xd_b = xd.astype(BF16)
    xdec_b = (xd * jnp.exp(cs_x[q - 1:q, :] - cs_x)).astype(BF16)
    ecs_x = jnp.exp(cs_x)

    head = lax.broadcasted_iota(jnp.int32, (q, n_heads), 1)
    even = (head & 1) == 0
    cs_par = jnp.concatenate([jnp.where(even, cs, 0.0), jnp.where(even, 0.0, cs)], axis=0)
    pair_sel = ((lax.broadcasted_iota(jnp.int32, (n_heads // 2, n_heads), 1) >> 1)
                == lax.broadcasted_iota(jnp.int32, (n_heads // 2, n_heads), 0)).astype(BF16)
    rtab = _dot_sel(pair_sel, cs_par, sel_on_left=True, dims=NT_DIMS)
    eye_h = (lax.broadcasted_iota(jnp.int32, (n_heads, n_heads), 0)
             == lax.broadcasted_iota(jnp.int32, (n_heads, n_heads), 1)).astype(BF16)
    cs_t = _dot_sel(eye_h, cs, sel_on_left=True, dims=NT_DIMS)
    st_decay = jnp.exp(jnp.broadcast_to(cs_t[:, q - 1:q], (n_heads, n_state)))

    row2 = lax.broadcasted_iota(jnp.int32, (q, 2 * q), 0)
    col2 = lax.broadcasted_iota(jnp.int32, (q, 2 * q), 1)
    causal2 = jnp.where(col2 >= q, col2 - q, col2) <= row2
    lane_p = lax.broadcasted_iota(jnp.int32, (q, 2 * p_dim), 1)
    lo_half = lane_p < p_dim
    zero_b = jnp.zeros((q, 2 * p_dim), BF16)
    b_off = d_inner
    c_off = d_inner + SSD_GROUPS * n_state
    gp = hpg * p_dim
    pairs_per_group = hpg // 2

    y_diag_parts, y_off_parts = [], []
    for g in range(SSD_GROUPS):
        bg = act[:, b_off + g * n_state:b_off + (g + 1) * n_state].astype(BF16)
        cg = act[:, c_off + g * n_state:c_off + (g + 1) * n_state].astype(BF16)
        cb2 = lax.dot_general(cg, jnp.concatenate([bg, bg], axis=0), NT_DIMS,
                              preferred_element_type=F32)
        prev = st_ref[0, g * hpg:(g + 1) * hpg].reshape(gp, n_state)
        y_off_parts.append(lax.dot_general(cg, prev.astype(BF16), NT_DIMS, preferred_element_type=F32))
        for jj in range(pairs_per_group):
            j = g * pairs_per_group + jj
            lmat = jnp.exp(jnp.where(causal2, cs_q[:, 2 * j * q:2 * (j + 1) * q] - rtab[j:j + 1, :], -jnp.inf))
            m = (cb2 * lmat).astype(BF16)
            pp = xd_b[:, 2 * j * p_dim:2 * (j + 1) * p_dim]
            blockdiag = jnp.concatenate([jnp.where(lo_half, pp, zero_b), jnp.where(lo_half, zero_b, pp)], axis=0)
            y_diag_parts.append(jnp.dot(m, blockdiag, preferred_element_type=F32))
        st_new = lax.dot_general(xdec_b[:, g * gp:(g + 1) * gp], bg, TN_DIMS,
                                 preferred_element_type=F32)
        for e in range(hpg):
            h = g * hpg + e
            st_ref[0, h] = (st_decay[h:h + 1, :] * prev[e * p_dim:(e + 1) * p_dim, :]
                            + st_new[e * p_dim:(e + 1) * p_dim, :])

    y = (jnp.concatenate(y_diag_parts, axis=-1) + jnp.concatenate(y_off_parts, axis=-1) * ecs_x
         + dskx_ref[...] * xs)
    gated = y * _silu(z_ref[...])
    gw = d_inner // SSD_GROUPS
    nw = nw_ref[...]
    for g in range(SSD_GROUPS):
        seg = gated[:, g * gw:(g + 1) * gw]
        ms = jnp.mean(seg * seg, axis=-1, keepdims=True)
        out_ref[:, g * gw:(g + 1) * gw] = (seg * lax.rsqrt(ms + RMS_EPS) * nw[:, g * gw:(g + 1) * gw]).astype(BF16)


def _ssd_seq(zx, dt_raw, row0, bsz, seqlen, q, conv_buf, h0, cw, cb, dtb, alog, dsk, nw):
    n_heads, p_dim, n_state = h0.shape[1:]
    d_inner = n_heads * p_dim
    conv_dim = cw.shape[1]
    nt = seqlen // q
    blk0 = row0 // q
    full2 = lambda b, c: (0, 0)
    col = lambda j: (lambda b, c: (blk0 + b * nt + c, j))
    head_ids = jnp.arange(n_heads, dtype=jnp.int32)[:, None]
    expand = lambda width: (jnp.arange(n_heads * width, dtype=jnp.int32)[None, :] // width == head_ids).astype(BF16)
    has_eq = q != p_dim
    extra_in = [expand(q)] if has_eq else []
    extra_specs = [pl.BlockSpec((n_heads, n_heads * q), full2)] if has_eq else []
    return pl.pallas_call(
        functools.partial(_ssd_seq_kernel, has_eq),
        grid=(bsz, nt),
        in_specs=[
            pl.BlockSpec((q, d_inner), col(0)),
            pl.BlockSpec((q, d_inner), col(1)),
            pl.BlockSpec((q, d_inner), col(2)),
            pl.BlockSpec((q, n_heads), col(0)),
            pl.BlockSpec((1, CONV_WIDTH - 1, conv_dim), lambda b, c: (b, 0, 0)),
            pl.BlockSpec((1, n_heads, p_dim, n_state), lambda b, c: (b, 0, 0, 0)),
            pl.BlockSpec((CONV_WIDTH, conv_dim), full2),
            pl.BlockSpec((1, conv_dim), full2),
            pl.BlockSpec((1, n_heads), full2),
            pl.BlockSpec((1, n_heads), full2),
            pl.BlockSpec((1, d_inner), full2),
            pl.BlockSpec((1, d_inner), full2),
            pl.BlockSpec((n_heads, d_inner), full2),
        ] + extra_specs,
        out_specs=[
            pl.BlockSpec((q, d_inner), lambda b, c: (b * nt + c, 0)),
            pl.BlockSpec((1, CONV_WIDTH - 1, conv_dim), lambda b, c: (b, 0, 0)),
            pl.BlockSpec((1, n_heads, p_dim, n_state), lambda b, c: (b, 0, 0, 0)),
        ],
        out_shape=[
            jax.ShapeDtypeStruct((bsz * seqlen, d_inner), BF16),
            jax.ShapeDtypeStruct((bsz, CONV_WIDTH - 1, conv_dim), F32),
            jax.ShapeDtypeStruct((bsz, n_heads, p_dim, n_state), F32),
        ],
        scratch_shapes=[
            pltpu.VMEM((q + CONV_PAD, conv_dim), F32),
        ],
        compiler_params=_params("arbitrary", "arbitrary"),
        name="ssd_seq",
    )(zx, zx, zx, dt_raw, conv_buf, h0, cw, cb.reshape(1, -1), dtb.reshape(1, -1),
      alog.reshape(1, -1), jnp.repeat(dsk, p_dim).reshape(1, -1), nw.reshape(1, -1), expand(p_dim), *extra_in)


def _first_argmax(v, rid, n):
    m = jnp.max(v, axis=0, keepdims=True)
    idx = jnp.min(jnp.where(v == m, rid, float(n)), axis=0, keepdims=True)
    return m, idx


def _route_tile(scores_t, bias_col):
    n_exp, tm = scores_t.shape
    per = n_exp // N_EXPERT_GROUPS
    neg = -jnp.inf
    biased = scores_t + bias_col
    rid_g = lax.broadcasted_iota(jnp.int32, (per, tm), 0).astype(F32)
    tiles, gs_rows = [], []
    for g in range(N_EXPERT_GROUPS):
        v = biased[g * per:(g + 1) * per, :]
        tiles.append(v)
        m1, i1 = _first_argmax(v, rid_g, per)
        m2 = jnp.max(jnp.where(rid_g == i1, neg, v), axis=0, keepdims=True)
        gs_rows.append(m1 + m2)
    gs = jnp.concatenate(gs_rows, axis=0)
    rid_grp = lax.broadcasted_iota(jnp.int32, (N_EXPERT_GROUPS, tm), 0).astype(F32)
    gsel = jnp.zeros((N_EXPERT_GROUPS, tm), F32)
    for _ in range(TOPK_GROUPS):
        _, gi = _first_argmax(gs, rid_grp, N_EXPERT_GROUPS)
        hit = rid_grp == gi
        gsel = jnp.where(hit, 1.0, gsel)
        gs = jnp.where(hit, neg, gs)
    masked = jnp.concatenate(
        [jnp.where(gsel[g:g + 1, :] > 0.0, tiles[g], neg) for g in range(N_EXPERT_GROUPS)], axis=0)
    rid = lax.broadcasted_iota(jnp.int32, (n_exp, tm), 0).astype(F32)
    sel = jnp.zeros((n_exp, tm), F32)
    ids, picked = [], []
    for _ in range(TOP_K):
        _, ei = _first_argmax(masked, rid, n_exp)
        hit = rid == ei
        ids.append(ei)
        picked.append(jnp.sum(jnp.where(hit, scores_t, 0.0), axis=0, keepdims=True))
        sel = jnp.where(hit, 1.0, sel)
        masked = jnp.where(hit, neg, masked)
    total = picked[0]
    for p in picked[1:]:
        total = total + p
    gates = [p / total * ROUTE_SCALE for p in picked]
    return sel, rid, ids, gates


def _proj_ln_router_kernel(alpha, n_a, ha_ref, hb_ref, w_ref, x_ref, g_ref, b_ref, rwt_ref, rb_ref,
                           o_ref, op_ref, eid_ref, gate_ref, rank_ref, cnt_ref, wb_ref, tri_ref, mix_ref):
    i = pl.program_id(0)
    tm = x_ref.shape[0]

    @pl.when(i == 0)
    def _():
        wb_ref[...] = w_ref[...].astype(BF16)
        r = lax.broadcasted_iota(jnp.int32, (tm, tm), 0)
        c = lax.broadcasted_iota(jnp.int32, (tm, tm), 1)
        tri_ref[...] = jnp.where(r < c, 1.0, 0.0).astype(BF16)
        cnt_ref[...] = jnp.zeros_like(cnt_ref)

    @pl.when(i < n_a)
    def _():
        mix_ref[...] = jnp.dot(ha_ref[...], wb_ref[...], preferred_element_type=F32)

    @pl.when(i >= n_a)
    def _():
        mix_ref[...] = jnp.dot(hb_ref[...], wb_ref[...], preferred_element_type=F32)

    x1 = _layer_norm(alpha * x_ref[...] + mix_ref[...], g_ref[...], b_ref[...])
    o_ref[...] = x1
    half = x1.shape[1] // 2
    op_ref[...] = _pack_pair(x1[:, :half], x1[:, half:])
    logits_t = lax.dot_general(rwt_ref[...].astype(BF16), x1.astype(BF16), NT_DIMS,
                               preferred_element_type=F32)
    sel, rid, ids, gates = _route_tile(_sigmoid(logits_t), rb_ref[...])

    sel_b = sel.astype(BF16)
    before = jnp.dot(sel_b, tri_ref[...], preferred_element_type=F32) + cnt_ref[:, 0:1]
    pad_rows = eid_ref.shape[0] - TOP_K
    zrow = jnp.zeros((pad_rows, tm), F32)
    ranks = [jnp.sum(jnp.where(rid == ei, before, 0.0), axis=0, keepdims=True) for ei in ids]
    eid_ref[...] = jnp.concatenate(ids + [zrow], axis=0).astype(jnp.int32)
    gate_ref[...] = jnp.concatenate(gates + [zrow], axis=0)
    rank_ref[...] = jnp.concatenate(ranks + [zrow], axis=0).astype(jnp.int32)
    ones = jnp.ones((tm, cnt_ref.shape[1]), BF16)
    cnt_ref[...] = cnt_ref[...] + jnp.dot(sel_b, ones, preferred_element_type=F32)


def _proj_ln_router(ha, hb, w, wl, x, alpha, ln_g, ln_b, router_w_t, router_bias_col):
    K = ha.shape[1]
    T, D = x.shape
    E = router_w_t.shape[0]
    tm = TM_PROJ
    kp = SUBLANE
    n_a = ha.shape[0] // tm
    n_b = hb.shape[0] // tm
    assert ha.shape[0] == n_a * tm and hb.shape[0] == n_b * tm and (n_a + n_b) * tm == T
    row_spec = pl.BlockSpec((kp, tm), lambda i: (0, i))
    return pl.pallas_call(
        functools.partial(_proj_ln_router_kernel, alpha, n_a),
        grid=(T // tm,),
        in_specs=[
            pl.BlockSpec((tm, K), lambda i: (jnp.minimum(i, n_a - 1), 0)),
            pl.BlockSpec((tm, K), lambda i: (jnp.maximum(i - n_a, 0), 0)),
            pl.BlockSpec((None, K, D), lambda i: (wl, 0, 0)),
            pl.BlockSpec((tm, D), lambda i: (i, 0)),
            pl.BlockSpec((1, D), lambda i: (0, 0)),
            pl.BlockSpec((1, D), lambda i: (0, 0)),
            pl.BlockSpec((E, D), lambda i: (0, 0)),
            pl.BlockSpec((E, 1), lambda i: (0, 0)),
        ],
        out_specs=[pl.BlockSpec((tm, D), lambda i: (i, 0)), pl.BlockSpec((tm, D // 2), lambda i: (i, 0)),
                   row_spec, row_spec, row_spec, pl.BlockSpec((E, LANE), lambda i: (0, 0))],
        out_shape=[jax.ShapeDtypeStruct((T, D), F32),
                   jax.ShapeDtypeStruct((T, D // 2), jnp.uint32),
                   jax.ShapeDtypeStruct((kp, T), jnp.int32),
                   jax.ShapeDtypeStruct((kp, T), F32),
                   jax.ShapeDtypeStruct((kp, T), jnp.int32),
                   jax.ShapeDtypeStruct((E, LANE), F32)],
        scratch_shapes=[pltpu.VMEM((K, D), BF16), pltpu.VMEM((tm, tm), BF16), pltpu.VMEM((tm, D), F32)],
        compiler_params=_params("arbitrary"),
        name="proj_ln_router",
    )(ha, hb, w, x, ln_g.reshape(1, D), ln_b.reshape(1, D), router_w_t, router_bias_col)


def _start_row_gather(idx_ref, base, n, src_hbm, dst, sem):
    for i in range(n):
        r = idx_ref[base + i]
        pltpu.make_async_copy(src_hbm.at[pl.ds(r, 1), :], dst.at[pl.ds(i, 1), :], sem).start(priority=i % 2)


def _wait_row_gather(n, src_hbm, dst, sem):
    pltpu.make_async_copy(src_hbm.at[pl.ds(0, n), :], dst, sem).wait()


def _scatter_kernel(n_tiles, pos_ref, zb_ref, nu_ref, x_ref, xs_hbm, xbuf, zbuf, sem, zsem):
    i = pl.program_id(0)
    tm = x_ref.shape[0]
    n_rows = TOP_K * tm
    bm = zbuf.shape[0]
    n_experts = zb_ref.shape[0]
    n_blocks = xs_hbm.shape[0] // bm
    slot = lax.rem(i, 2)

    def zero_block(blk):
        start = pl.multiple_of(blk * bm, bm)
        return pltpu.make_async_copy(zbuf, xs_hbm.at[pl.ds(start, bm), :], zsem)

    def wait_rows(s):
        pltpu.make_async_copy(xs_hbm.at[pl.ds(0, n_rows), :], xs_hbm.at[pl.ds(0, n_rows), :], sem.at[s]).wait()

    @pl.when(i == 0)
    def _():
        zbuf[...] = jnp.zeros_like(zbuf)
        for e in range(n_experts):
            @pl.when(zb_ref[e] >= 0)
            def _():
                zero_block(zb_ref[e]).start()

        def start_tail(blk, carry):
            zero_block(blk).start()
            return carry

        def wait_tail(blk, carry):
            zero_block(blk).wait()
            return carry

        lax.fori_loop(nu_ref[0], n_blocks, start_tail, 0)
        for e in range(n_experts):
            @pl.when(zb_ref[e] >= 0)
            def _():
                zero_block(zb_ref[e]).wait()
        lax.fori_loop(nu_ref[0], n_blocks, wait_tail, 0)

    @pl.when(i >= 2)
    def _():
        wait_rows(slot)

    xbuf[slot] = x_ref[...]
    for k in range(TOP_K):
        for t in range(tm):
            r = pos_ref[i * n_rows + k * tm + t]
            pltpu.make_async_copy(xbuf.at[slot, pl.ds(t, 1), :], xs_hbm.at[pl.ds(r, 1), :],
                                  sem.at[slot]).start(priority=t % 2)

    @pl.when(i == n_tiles - 1)
    def _():
        wait_rows(slot)
        if n_tiles >= 2:
            wait_rows(1 - slot)


def _scatter_rows(x, pos_tiles, zero_blk, n_used, n_slots):
    T, D = x.shape
    tm = TM_COMBINE
    n_tiles = T // tm
    grid_spec = pltpu.PrefetchScalarGridSpec(
        num_scalar_prefetch=3,
        grid=(n_tiles,),
        in_specs=[pl.BlockSpec((tm, D), lambda i, pos, zb, nu: (i, 0))],
        out_specs=pl.BlockSpec(memory_space=pl.ANY),
        scratch_shapes=[
            pltpu.VMEM((2, tm, D), x.dtype),
            pltpu.VMEM((BM_EXPERT, D), x.dtype),
            pltpu.SemaphoreType.DMA((2,)),
            pltpu.SemaphoreType.DMA,
        ],
    )
    return pl.pallas_call(
        functools.partial(_scatter_kernel, n_tiles),
        grid_spec=grid_spec,
        out_shape=jax.ShapeDtypeStruct((n_slots, D), x.dtype),
        compiler_params=_params("arbitrary"),
        name="scatter_rows",
    )(pos_tiles, zero_blk, n_used, x)


def _expert_kernel(layer, be_ref, nu_ref, nx_ref, x_ref, wg_hbm, wu_hbm, wd_hbm, y_ref,
                   sg32, su32, sd32, wgb, wub, wdb, sem):
    b = pl.program_id(0)
    n_used = nu_ref[0]

    def weight_copies(e):
        return (pltpu.make_async_copy(wg_hbm.at[layer, e], sg32, sem),
                pltpu.make_async_copy(wu_hbm.at[layer, e], su32, sem),
                pltpu.make_async_copy(wd_hbm.at[layer, e], sd32, sem))

    @pl.when(jnp.logical_and(b == 0, n_used > 0))
    def _():
        for cp in weight_copies(be_ref[0]):
            cp.start()

    @pl.when(b < n_used)
    def _():
        new_expert = jnp.logical_or(b == 0, be_ref[b] != be_ref[jnp.maximum(b - 1, 0)])

        @pl.when(new_expert)
        def _():
            for cp in weight_copies(be_ref[b]):
                cp.wait()
            wgb[...] = sg32[...].astype(BF16)
            wub[...] = su32[...].astype(BF16)
            wdb[...] = sd32[...].astype(BF16)

            @pl.when(nx_ref[b] >= 0)
            def _():
                for cp in weight_copies(nx_ref[b]):
                    cp.start()

        half = x_ref.shape[1]
        x_lo, x_hi = _unpack_pair(x_ref[...])
        x_lo = x_lo.astype(BF16)
        x_hi = x_hi.astype(BF16)
        hg = (jnp.dot(x_lo, wgb[0:half, :], preferred_element_type=F32)
              + jnp.dot(x_hi, wgb[half:, :], preferred_element_type=F32))
        hu = (jnp.dot(x_lo, wub[0:half, :], preferred_element_type=F32)
              + jnp.dot(x_hi, wub[half:, :], preferred_element_type=F32))
        hh = (_silu(hg) * hu).astype(BF16)
        y = jnp.dot(hh, wdb[...], preferred_element_type=F32)
        y_ref[...] = _pack_pair(y[:, :half], y[:, half:])

    @pl.when(b >= n_used)
    def _():
        y_ref[...] = jnp.zeros_like(y_ref)


def _experts(x_sorted, block_e, n_used, next_e, wg, wu, wd, layer):
    n_slots, dh = x_sorted.shape
    D, DE = wg.shape[2:]
    bm = BM_EXPERT
    nb = n_slots // bm
    grid_spec = pltpu.PrefetchScalarGridSpec(
        num_scalar_prefetch=3,
        grid=(nb,),
        in_specs=[
            pl.BlockSpec((bm, dh), lambda b, be, nu, nx: (jnp.minimum(b, nu[0] - 1), 0)),
            pl.BlockSpec(memory_space=pl.ANY),
            pl.BlockSpec(memory_space=pl.ANY),
            pl.BlockSpec(memory_space=pl.ANY),
        ],
        out_specs=pl.BlockSpec((bm, dh), lambda b, be, nu, nx: (b, 0)),
        scratch_shapes=[
            pltpu.VMEM((D, DE), F32),
            pltpu.VMEM((D, DE), F32),
            pltpu.VMEM((DE, D), F32),
            pltpu.VMEM((D, DE), BF16),
            pltpu.VMEM((D, DE), BF16),
            pltpu.VMEM((DE, D), BF16),
            pltpu.SemaphoreType.DMA,
        ],
    )
    return pl.pallas_call(
        functools.partial(_expert_kernel, layer),
        grid_spec=grid_spec,
        out_shape=jax.ShapeDtypeStruct((n_slots, dh), jnp.uint32),
        compiler_params=_params("arbitrary"),
        name="experts",
    )(block_e, n_used, next_e, x_sorted, wg, wu, wd)


def _combine_kernel(alpha, n_tiles, pos_ref, gates_ref, x_ref, y_hbm, sg_ref, su_ref, sd_ref, g_ref, b_ref,
                    o_ref, ob_ref, ybuf_a, ybuf_b, sgb, sub, sdb, sem):
    i = pl.program_id(0)
    tm = x_ref.shape[0] // 2
    n_rows = ybuf_a.shape[0]
    first = 2 * i

    @pl.when(i == 0)
    def _():
        _start_row_gather(pos_ref, 0, n_rows, y_hbm, ybuf_a, sem.at[0])
        sgb[...] = sg_ref[...].astype(BF16)
        sub[...] = su_ref[...].astype(BF16)
        sdb[...] = sd_ref[...].astype(BF16)

    def tile(r0, ybuf):
        x = x_ref[r0:r0 + tm, :]
        xb = x.astype(BF16)
        hg = jnp.dot(xb, sgb[...], preferred_element_type=F32)
        hu = jnp.dot(xb, sub[...], preferred_element_type=F32)
        acc = jnp.dot((_silu(hg) * hu).astype(BF16), sdb[...], preferred_element_type=F32)
        gates = gates_ref[r0:r0 + tm, :]
        routed_lo = routed_hi = None
        for k in range(TOP_K):
            y_lo, y_hi = _unpack_pair(ybuf[k * tm:(k + 1) * tm, :])
            gk = gates[:, k:k + 1]
            routed_lo = gk * y_lo if routed_lo is None else routed_lo + gk * y_lo
            routed_hi = gk * y_hi if routed_hi is None else routed_hi + gk * y_hi
        acc = acc + jnp.concatenate([routed_lo, routed_hi], axis=-1)
        out = _layer_norm(alpha * x + acc, g_ref[...], b_ref[...])
        o_ref[r0:r0 + tm, :] = out
        ob_ref[r0:r0 + tm, :] = out.astype(BF16)

    _wait_row_gather(n_rows, y_hbm, ybuf_a, sem.at[0])
    _start_row_gather(pos_ref, (first + 1) * n_rows, n_rows, y_hbm, ybuf_b, sem.at[1])
    tile(0, ybuf_a)
    _wait_row_gather(n_rows, y_hbm, ybuf_b, sem.at[1])
    nxt = jnp.minimum(first + 2, n_tiles - 2)
    _start_row_gather(pos_ref, nxt * n_rows, n_rows, y_hbm, ybuf_a, sem.at[0])
    tile(tm, ybuf_b)

    @pl.when(i == pl.num_programs(0) - 1)
    def _():
        _wait_row_gather(n_rows, y_hbm, ybuf_a, sem.at[0])


def _combine(x, y_sorted, pos_tiles, gates, sg, su, sd, alpha, ln_g, ln_b, layer):
    T, D = x.shape
    DS = sg.shape[2]
    tm = TM_COMBINE
    n_tiles = T // tm
    assert n_tiles % 2 == 0, "combine walks token tiles in pairs"
    grid_spec = pltpu.PrefetchScalarGridSpec(
        num_scalar_prefetch=1,
        grid=(n_tiles // 2,),
        in_specs=[
            pl.BlockSpec((2 * tm, TOP_K), lambda i, pos: (i, 0)),
            pl.BlockSpec((2 * tm, D), lambda i, pos: (i, 0)),
            pl.BlockSpec(memory_space=pl.ANY),
            pl.BlockSpec((None, D, DS), lambda i, pos: (layer, 0, 0)),
            pl.BlockSpec((None, D, DS), lambda i, pos: (layer, 0, 0)),
            pl.BlockSpec((None, DS, D), lambda i, pos: (layer, 0, 0)),
            pl.BlockSpec((None, 1, D), lambda i, pos: (layer, 0, 0)),
            pl.BlockSpec((None, 1, D), lambda i, pos: (layer, 0, 0)),
        ],
        out_specs=[pl.BlockSpec((2 * tm, D), lambda i, pos: (i, 0)), pl.BlockSpec((2 * tm, D), lambda i, pos: (i, 0))],
        scratch_shapes=[
            pltpu.VMEM((TOP_K * tm, y_sorted.shape[1]), y_sorted.dtype),
            pltpu.VMEM((TOP_K * tm, y_sorted.shape[1]), y_sorted.dtype),
            pltpu.VMEM((D, DS), BF16),
            pltpu.VMEM((D, DS), BF16),
            pltpu.VMEM((DS, D), BF16),
            pltpu.SemaphoreType.DMA((2,)),
        ],
    )
    return pl.pallas_call(
        functools.partial(_combine_kernel, alpha, n_tiles),
        grid_spec=grid_spec,
        out_shape=[jax.ShapeDtypeStruct((T, D), F32), jax.ShapeDtypeStruct((T, D), BF16)],
        compiler_params=_params("arbitrary"),
        name="combine",
    )(pos_tiles, gates, x, y_sorted, sg, su, sd, ln_g.reshape(-1, 1, D), ln_b.reshape(-1, 1, D))


def _dispatch(eid_t, rank_t, counts):
    T = eid_t.shape[1]
    E = counts.shape[0]
    A = T * TOP_K
    bm = BM_EXPERT
    nb = -(-A // bm) + E
    cnt = counts[:, 0].astype(jnp.int32)
    padded = (cnt + bm - 1) // bm * bm
    pends = jnp.cumsum(padded)
    pstart = pends - padded
    eid = eid_t[:TOP_K]
    onehot = eid[:, :, None] == jnp.arange(E, dtype=jnp.int32)[None, None, :]
    dest = rank_t[:TOP_K] + jnp.sum(jnp.where(onehot, pstart[None, None, :], 0), axis=-1)
    block_start = jnp.arange(nb, dtype=jnp.int32) * bm
    block_e = jnp.minimum(jnp.sum((pends[None, :] <= block_start[:, None]).astype(jnp.int32), axis=1), E - 1)
    n_used = (pends[-1:] // bm).astype(jnp.int32)
    zero_blk = jnp.where(padded > 0, pends // bm - 1, -1).astype(jnp.int32)
    ids = jnp.where(padded > 0, jnp.arange(E, dtype=jnp.int32), E)
    later = jnp.concatenate([lax.cummin(ids[::-1])[::-1][1:], jnp.full((1,), E, jnp.int32)])
    next_e = jnp.where(later >= E, -1, later)[block_e].astype(jnp.int32)
    tm = TM_COMBINE
    pos_tiles = dest.reshape(TOP_K, T // tm, tm).transpose(1, 0, 2).reshape(-1)
    return nb * bm, block_e, n_used, next_e, zero_blk, pos_tiles


def _moe(x1, x1_packed, eid_t, gate_t, rank_t, counts, alpha, layer, wg, wu, wd, sg, su, sd, ln_g, ln_b):
    n_slots, block_e, n_used, next_e, zero_blk, pos_tiles = _dispatch(eid_t, rank_t, counts)
    x_sorted = _scatter_rows(x1_packed, pos_tiles, zero_blk, n_used, n_slots)
    y_sorted = _experts(x_sorted, block_e, n_used, next_e, wg, wu, wd, layer)
    gates = gate_t[:TOP_K].T
    return _combine(x1, y_sorted, pos_tiles, gates, sg, su, sd, alpha, ln_g, ln_b, layer)


def kernel(x_prompt, x_sample, state_lru_conv, state_lru_h, state_ssd_conv, state_ssd, lru_w_in, lru_conv_w, lru_conv_b, lru_w_rgate, lru_b_rgate, lru_w_igate, lru_b_igate, lru_lambda, lru_w_out, ssd_w_in, ssd_conv_w, ssd_conv_b, ssd_dt_bias, ssd_a_log, ssd_d, ssd_norm_w, ssd_w_out, ln_mix_g, ln_mix_b, ln_ffn_g, ln_ffn_b, router_w, router_bias, moe_w_gate, moe_w_up, moe_w_down, shared_w_gate, shared_w_up, shared_w_down):
    bp, lp, d_model = x_prompt.shape
    bs, ls, _ = x_sample.shape
    depth = ln_mix_g.shape[0]
    alpha = (2.0 * depth) ** 0.25
    tp, ts = bp * lp, bs * ls
    n_heads, p_dim, n_state = state_ssd.shape[2:]
    d_inner = n_heads * p_dim
    conv_dim = ssd_conv_w.shape[2]
    d_rnn = lru_conv_w.shape[2]

    x = jnp.concatenate([x_prompt.reshape(tp, d_model), x_sample.reshape(ts, d_model)], axis=0)
    tl_p = min(TL_LRU, lp)
    tl_s = min(TL_LRU, ls)
    q_p = min(SSD_CHUNK, lp)
    q_s = min(SSD_CHUNK, ls)

    x_mxu = x
    p_lru_conv, p_lru_h, p_ssd_conv, p_ssd = [], [], [], []
    s_lru_conv, s_lru_h, s_ssd_conv, s_ssd = [], [], [], []
    for i in range(depth):
        j = i // 2
        if i % 2 == 0:
            gx = _in_proj(x_mxu, lru_w_in, j, 2 * d_rnn, TN_LRU_IN)
            wrg = _block_diag_gate(lru_w_rgate[j])
            wig = _block_diag_gate(lru_w_igate[j])
            common = (lru_conv_w[j], lru_conv_b[j], wrg, lru_b_rgate[j], wig, lru_b_igate[j], lru_lambda[j])
            hp, cp, lp_h = _lru_seq(gx, 0, bp, lp, tl_p, jnp.zeros((bp, CONV_WIDTH - 1, d_rnn), F32),
                                    jnp.zeros((bp, d_rnn), F32), *common)
            hs, cs_, ls_h = _lru_seq(gx, tp, bs, ls, tl_s, state_lru_conv[j], state_lru_h[j], *common)
            p_lru_conv.append(cp)
            p_lru_h.append(lp_h.reshape(bp, d_rnn))
            s_lru_conv.append(cs_)
            s_lru_h.append(ls_h.reshape(bs, d_rnn))
            w_out, wl = lru_w_out, j
        else:
            zx = _in_proj(x_mxu, ssd_w_in, j, d_inner + conv_dim, TN_SSD_IN)
            dt_raw = _small_proj(x_mxu, ssd_w_in[j][:, d_inner + conv_dim:])
            common = (ssd_conv_w[j], ssd_conv_b[j], ssd_dt_bias[j], ssd_a_log[j], ssd_d[j], ssd_norm_w[j])
            hp, cp, sp = _ssd_seq(zx, dt_raw, 0, bp, lp, q_p, jnp.zeros((bp, CONV_WIDTH - 1, conv_dim), F32),
                                  jnp.zeros((bp, n_heads, p_dim, n_state), F32), *common)
            hs, cs_, ss = _ssd_seq(zx, dt_raw, tp, bs, ls, q_s, state_ssd_conv[j], state_ssd[j], *common)
            p_ssd_conv.append(cp)
            p_ssd.append(sp)
            s_ssd_conv.append(cs_)
            s_ssd.append(ss)
            w_out, wl = ssd_w_out, j
        x1, x1_packed, eid_t, gate_t, rank_t, counts = _proj_ln_router(
            hp, hs, w_out, wl, x, alpha, ln_mix_g[i], ln_mix_b[i], router_w[i].T, router_bias[i].reshape(-1, 1))
        x, x_mxu = _moe(x1, x1_packed, eid_t, gate_t, rank_t, counts, alpha, i, moe_w_gate, moe_w_up, moe_w_down,
                 shared_w_gate, shared_w_up, shared_w_down, ln_ffn_g, ln_ffn_b)

    y_prompt = x[:tp].reshape(bp, lp, d_model)
    y_sample = x[tp:].reshape(bs, ls, d_model)
    return (y_prompt, y_sample,
            jnp.stack(p_lru_conv), jnp.stack(p_lru_h), jnp.stack(p_ssd_conv), jnp.stack(p_ssd),
            jnp.stack(s_lru_conv), jnp.stack(s_lru_h), jnp.stack(s_ssd_conv), jnp.stack(s_ssd))
```
